```python
import jax, jax.numpy as jnp
from jax import lax
import numpy as np

D_MODEL = 2048
BATCH = 8
SEQ = 4096
DEPTH = 2

PLE_DIM = 256
N_HEADS = 16
KV_HEADS = 2
Q_PER_KV = N_HEADS // KV_HEADS
HEAD_DIM = 64
ATTN_WIDTH = N_HEADS * HEAD_DIM
KV_WIDTH = KV_HEADS * HEAD_DIM
WINDOW = 128
ATTN_BLOCK = 128
ROT_DIM = HEAD_DIM // 4
ROPE_THETA = 500000.0
HG_HEADS = 4
HG_HEAD_DIM = 128
HG_WIDTH = HG_HEADS * HG_HEAD_DIM
HG_CHUNK = 64
LRU_HEADS = 8
LRU_HEAD_DIM = 64
LRU_WIDTH = LRU_HEADS * LRU_HEAD_DIM
CONV_WIDTH = 4
LRU_C = 8.0
IN_SIZES = (ATTN_WIDTH, KV_WIDTH, KV_WIDTH, HG_WIDTH, HG_WIDTH, HG_WIDTH, HG_WIDTH, LRU_WIDTH, LRU_WIDTH)
IN_COLS = sum(IN_SIZES)
MIX_WIDTH = ATTN_WIDTH + HG_WIDTH + LRU_WIDTH
N_EXPERTS = 64
TOPK = 8
N_GROUPS = 8
TOPK_GROUPS = 4
EXPERTS_PER_GROUP = N_EXPERTS // N_GROUPS
EXPERT_DIM = 512
ROUTED_SCALE = 2.5
MOE_BLOCK = 256
DEEPNORM_ALPHA = (2 * DEPTH) ** 0.25
DEEPNORM_BETA = (8 * DEPTH) ** -0.25
RMS_EPS = 1e-6
LN_EPS = 1e-5
NEG_BIG = -1e30
TINY = 1e-30

kernel_name = "hybrid_swa_hgrn2_rglru_moe_deepnorm"


def rms_norm(x, gain):
    xf = x.astype(jnp.float32)
    y = xf * lax.rsqrt(jnp.mean(xf * xf, axis=-1, keepdims=True) + RMS_EPS) * gain
    return y.astype(x.dtype)


def layer_norm(x, gain, bias):
    xf = x.astype(jnp.float32)
    mu = jnp.mean(xf, axis=-1, keepdims=True)
    xc = xf - mu
    var = jnp.mean(xc * xc, axis=-1, keepdims=True)
    return (xc * lax.rsqrt(var + LN_EPS) * gain + bias).astype(x.dtype)


def rotary_tables(positions):
    inv_freq = ROPE_THETA ** (-jnp.arange(0, ROT_DIM, 2, dtype=jnp.float32) / ROT_DIM)
    ang = positions.astype(jnp.float32)[..., None] * inv_freq
    return jnp.cos(ang)[:, :, None, :], jnp.sin(ang)[:, :, None, :]


def apply_partial_rotary(t, cos, sin):
    tr = t[..., :ROT_DIM].astype(jnp.float32)
    t1, t2 = tr[..., :ROT_DIM // 2], tr[..., ROT_DIM // 2:]
    rot = jnp.concatenate([t1 * cos - t2 * sin, t2 * cos + t1 * sin], axis=-1)
    return jnp.concatenate([rot.astype(t.dtype), t[..., ROT_DIM:]], axis=-1)


def sliding_window_sink_attention(q, k, v, sinks):
    Bsz, S = q.shape[:2]
    nb = S // ATTN_BLOCK
    qb = q.reshape(Bsz, nb, ATTN_BLOCK, KV_HEADS, Q_PER_KV, HEAD_DIM)

    def band(t):
        tb = t.reshape(Bsz, nb, ATTN_BLOCK, KV_HEADS, HEAD_DIM)
        prev = jnp.pad(tb, ((0, 0), (1, 0), (0, 0), (0, 0), (0, 0)))[:, :-1]
        return jnp.concatenate([prev, tb], axis=2)

    kb, vb = band(k), band(v)
    s = jnp.einsum('bnqhgd,bnkhd->bnhgqk', qb, kb, preferred_element_type=jnp.float32) * (HEAD_DIM ** -0.5)
    blk = jnp.arange(nb)[:, None] * ATTN_BLOCK
    qpos = blk + jnp.arange(ATTN_BLOCK)[None]
    kpos = blk - ATTN_BLOCK + jnp.arange(2 * ATTN_BLOCK)[None]
    rel = qpos[:, :, None] - kpos[:, None, :]
    allowed = (rel >= 0) & (rel < WINDOW) & (kpos[:, None, :] >= 0)
    s = jnp.where(allowed[None, :, None, None], s, NEG_BIG)
    sink = sinks.astype(jnp.float32).reshape(KV_HEADS, Q_PER_KV)[None, None, :, :, None, None]
    m = jnp.maximum(jnp.max(s, axis=-1, keepdims=True), sink)
    pr = jnp.exp(s - m)
    probs = pr / (jnp.sum(pr, axis=-1, keepdims=True) + jnp.exp(sink - m))
    o = jnp.einsum('bnhgqk,bnkhd->bnqhgd', probs.astype(v.dtype), vb)
    return o.reshape(Bsz, S, ATTN_WIDTH)


def hgrn2_mixer(q, f_pre, v, g, lb, gain):
    Bsz, S, _ = q.shape
    nc = S // HG_CHUNK
    qf = jax.nn.silu(q.astype(jnp.float32))
    fp = f_pre.astype(jnp.float32)
    vf = v.astype(jnp.float32)
    f = lb + (1.0 - lb) * jax.nn.sigmoid(fp)
    log_f = jnp.log(jnp.maximum(f, TINY))
    kf = (1.0 - lb) * jax.nn.sigmoid(-fp)

    def to_chunks(t):
        return t.reshape(Bsz, nc, HG_CHUNK, HG_HEADS, HG_HEAD_DIM).transpose(1, 0, 3, 2, 4)

    causal = jnp.tril(jnp.ones((HG_CHUNK, HG_CHUNK), dtype=bool))[:, :, None]

    def step(state, xs):
        qc, kc, vc, lfc = xs
        b = jnp.cumsum(lfc, axis=2)
        diff = b[:, :, :, None, :] - b[:, :, None, :, :]
        decay = jnp.where(causal, jnp.exp(jnp.where(causal, diff, 0.0)), 0.0)
        scores = jnp.einsum('bhtk,bhsk,bhtsk->bhts', qc, kc, decay)
        o = jnp.einsum('bhts,bhsv->bhtv', scores, vc) + jnp.einsum('bhtk,bhkv->bhtv', qc * jnp.exp(b), state)
        b_last = b[:, :, -1:, :]
        state = jnp.exp(b_last[:, :, 0, :, None]) * state + jnp.einsum('bhsk,bhsv->bhkv', kc * jnp.exp(b_last - b), vc)
        return state, o

    state0 = jnp.zeros((Bsz, HG_HEADS, HG_HEAD_DIM, HG_HEAD_DIM), jnp.float32)
    _, o = lax.scan(step, state0, (to_chunks(qf), to_chunks(kf), to_chunks(vf), to_chunks(log_f)))
    o = o.transpose(1, 0, 3, 2, 4).reshape(Bsz, S, HG_HEADS, HG_HEAD_DIM)
    o = rms_norm(o, gain).reshape(Bsz, S, HG_WIDTH)
    return (o * jax.nn.silu(g.astype(jnp.float32))).astype(g.dtype)


def rglru_mixer(xr, gr, conv_w, conv_b, w_a, b_a, w_x, b_x, lam, gain):
    Bsz, S, _ = xr.shape
    xc = lax.conv_general_dilated(xr, conv_w[:, None, :], window_strides=(1,), padding=[(CONV_WIDTH - 1, 0)],
                                  dimension_numbers=('NWC', 'WIO', 'NWC'), feature_group_count=LRU_WIDTH) + conv_b
    xc = xc.astype(jnp.float32)
    xh = xc.reshape(Bsz, S, LRU_HEADS, LRU_HEAD_DIM)
    r = jax.nn.sigmoid(jnp.einsum('bshi,hij->bshj', xh, w_a) + b_a).reshape(Bsz, S, LRU_WIDTH)
    ig = jax.nn.sigmoid(jnp.einsum('bshi,hij->bshj', xh, w_x) + b_x).reshape(Bsz, S, LRU_WIDTH)
    log_a = -LRU_C * r * jax.nn.softplus(-lam.astype(jnp.float32))
    a = jnp.exp(log_a)
    u = jnp.sqrt(jnp.maximum(-jnp.expm1(2.0 * log_a), 0.0)) * (ig * xc)

    def combine(left, right):
        a1, b1 = left
        a2, b2 = right
        return a1 * a2, a2 * b1 + b2

    _, h = lax.associative_scan(combine, (a, u), axis=1)
    y = rms_norm(h, gain) * jax.nn.gelu(gr.astype(jnp.float32))
    return y.astype(xr.dtype)


def swiglu(x, w1, w3, w2):
    return (jax.nn.silu(x @ w1) * (x @ w3)) @ w2


def routed_experts(xf, idx, gates, w1, w3, w2):
    T, D = xf.shape
    A = T * TOPK
    e_flat = idx.reshape(-1)
    order = jnp.argsort(e_flat)
    e_sorted = e_flat[order]
    tok_sorted = (order // TOPK).astype(jnp.int32)
    gate_sorted = gates.reshape(-1)[order]
    counts = jnp.bincount(e_flat, length=N_EXPERTS)
    padded = ((counts + MOE_BLOCK - 1) // MOE_BLOCK) * MOE_BLOCK
    start = jnp.cumsum(counts) - counts
    pend = jnp.cumsum(padded)
    dest = (pend - padded)[e_sorted] + jnp.arange(A) - start[e_sorted]
    n_blocks = -(-A // MOE_BLOCK) + N_EXPERTS
    rows = n_blocks * MOE_BLOCK
    row_tok = jnp.full((rows,), T, jnp.int32).at[dest].set(tok_sorted)
    row_gate = jnp.zeros((rows,), xf.dtype).at[dest].set(gate_sorted)
    block_expert = jnp.clip(jnp.searchsorted(pend, jnp.arange(n_blocks) * MOE_BLOCK, side='right'), 0, N_EXPERTS - 1)
    xpad = jnp.concatenate([xf, jnp.zeros((1, D), xf.dtype)], axis=0)

    def step(acc, blk):
        tok, g, e = blk
        xb = xpad[tok]
        yb = swiglu(xb, w1[e], w3[e], w2[e]) * g[:, None]
        return acc.at[tok].add(yb), None

    acc, _ = lax.scan(step, jnp.zeros_like(xpad),
                      (row_tok.reshape(n_blocks, MOE_BLOCK), row_gate.reshape(n_blocks, MOE_BLOCK), block_expert))
    return acc[:T]


def moe_ffn(x, router_w, router_b, w1, w3, w2, sw1, sw3, sw2):
    Bsz, S, D = x.shape
    T = Bsz * S
    xf = x.reshape(T, D)
    scores = jax.nn.sigmoid((xf @ router_w).astype(jnp.float32))
    sel = scores + router_b.astype(jnp.float32)
    group_score = jnp.sum(lax.top_k(sel.reshape(T, N_GROUPS, EXPERTS_PER_GROUP), 2)[0], axis=-1)
    _, gidx = lax.top_k(group_score, TOPK_GROUPS)
    gmask = jnp.sum(jax.nn.one_hot(gidx, N_GROUPS, dtype=jnp.float32), axis=1) > 0
    sel = jnp.where(jnp.repeat(gmask, EXPERTS_PER_GROUP, axis=1), sel, NEG_BIG)
    _, idx = lax.top_k(sel, TOPK)
    w = jnp.take_along_axis(scores, idx, axis=1)
    gates = (w / jnp.sum(w, axis=-1, keepdims=True) * ROUTED_SCALE).astype(x.dtype)
    y = routed_experts(xf, idx, gates, w1, w3, w2) + swiglu(xf, sw1, sw3, sw2)
    return y.reshape(Bsz, S, D)


def setup_inputs(seed: int = 0) -> dict:
    key = jax.random.key(seed)
    ks = iter(jax.random.split(key, 48))
    L, D, E, F = DEPTH, D_MODEL, N_EXPERTS, EXPERT_DIM

    def nrm(shape, scale):
        return jax.random.normal(next(ks), shape, jnp.float32) * scale

    def gain(shape):
        return 1.0 + nrm(shape, 0.02)

    x = nrm((BATCH, SEQ, D), 1.0)
    p = nrm((DEPTH, BATCH, SEQ, PLE_DIM), 1.0)
    offs = jax.random.randint(next(ks), (BATCH, 1), 0, 1024, jnp.int32)
    positions = offs + jnp.arange(SEQ, dtype=jnp.int32)[None, :]
    u = jax.random.uniform(next(ks), (L, LRU_WIDTH), jnp.float32, 0.9, 0.999)
    a0 = u ** (1.0 / LRU_C)
    lru_lambda = jnp.log(a0) - jnp.log1p(-a0)
    return {
        "x": x,
        "p": p,
        "positions": positions,
        "w_in": nrm((L, D, IN_COLS), D ** -0.5),
        "w_out": nrm((L, MIX_WIDTH, D), MIX_WIDTH ** -0.5 * DEEPNORM_BETA),
        "attn_sinks": nrm((L, N_HEADS), 0.5),
        "attn_norm": gain((L, ATTN_WIDTH)),
        "hg_lb_logits": nrm((L, HG_WIDTH), 0.5),
        "hg_norm": gain((L, HG_HEADS, HG_HEAD_DIM)),
        "lru_conv_w": nrm((L, CONV_WIDTH, LRU_WIDTH), CONV_WIDTH ** -0.5),
        "lru_conv_b": nrm((L, LRU_WIDTH), 0.02),
        "lru_wa": nrm((L, LRU_HEADS, LRU_HEAD_DIM, LRU_HEAD_DIM), LRU_HEAD_DIM ** -0.5),
        "lru_ba": nrm((L, LRU_HEADS, LRU_HEAD_DIM), 0.02),
        "lru_wx": nrm((L, LRU_HEADS, LRU_HEAD_DIM, LRU_HEAD_DIM), LRU_HEAD_DIM ** -0.5),
        "lru_bx": nrm((L, LRU_HEADS, LRU_HEAD_DIM), 0.02),
        "lru_lambda": lru_lambda,
        "lru_norm": gain((L, LRU_WIDTH)),
        "ln1_g": gain((L, D)),
        "ln1_b": nrm((L, D), 0.02),
        "router_w": nrm((L, D, E), D ** -0.5),
        "router_b": nrm((L, E), 0.01),
        "exp_w1": nrm((L, E, D, F), D ** -0.5),
        "exp_w3": nrm((L, E, D, F), D ** -0.5),
        "exp_w2": nrm((L, E, F, D), F ** -0.5 * DEEPNORM_BETA),
        "sh_w1": nrm((L, D, F), D ** -0.5),
        "sh_w3": nrm((L, D, F), D ** -0.5),
        "sh_w2": nrm((L, F, D), F ** -0.5 * DEEPNORM_BETA),
        "ple_w": nrm((L, PLE_DIM, D), PLE_DIM ** -0.5 * DEEPNORM_BETA),
        "ple_gate_w": nrm((L, D, D), D ** -0.5),
        "ple_gate_b": nrm((L, D), 0.02),
        "ln2_g": gain((L, D)),
        "ln2_b": nrm((L, D), 0.02),
    }


def reference(x, p, positions, w_in, w_out, attn_sinks, attn_norm, hg_lb_logits, hg_norm,
              lru_conv_w, lru_conv_b, lru_wa, lru_ba, lru_wx, lru_bx, lru_lambda, lru_norm,
              ln1_g, ln1_b, router_w, router_b, exp_w1, exp_w3, exp_w2, sh_w1, sh_w3, sh_w2,
              ple_w, ple_gate_w, ple_gate_b, ln2_g, ln2_b):
    Bsz, S, _ = x.shape
    lb_sm = jax.nn.softmax(hg_lb_logits.astype(jnp.float32), axis=0)
    hg_lb = jnp.maximum(jnp.cumsum(lb_sm, axis=0) - lb_sm[0], 0.0)
    cos, sin = rotary_tables(positions)
    split_at = np.cumsum(IN_SIZES)[:-1].tolist()
    for i in range(DEPTH):
        h = x @ w_in[i]
        aq, ak, av, hq, hf, hi, hg, lx, lg = jnp.split(h, split_at, axis=-1)
        aq = apply_partial_rotary(aq.reshape(Bsz, S, N_HEADS, HEAD_DIM), cos, sin)
        ak = apply_partial_rotary(ak.reshape(Bsz, S, KV_HEADS, HEAD_DIM), cos, sin)
        av = av.reshape(Bsz, S, KV_HEADS, HEAD_DIM)
        y_attn = rms_norm(sliding_window_sink_attention(aq, ak, av, attn_sinks[i]), attn_norm[i])
        y_hg = hgrn2_mixer(hq, hf, hi, hg, hg_lb[i], hg_norm[i])
        y_lru = rglru_mixer(lx, lg, lru_conv_w[i], lru_conv_b[i], lru_wa[i], lru_ba[i],
                            lru_wx[i], lru_bx[i], lru_lambda[i], lru_norm[i])
        mixed = jnp.concatenate([y_attn, y_hg, y_lru], axis=-1) @ w_out[i]
        x = layer_norm(DEEPNORM_ALPHA * x + mixed, ln1_g[i], ln1_b[i])
        y_moe = moe_ffn(x, router_w[i], router_b[i], exp_w1[i], exp_w3[i], exp_w2[i], sh_w1[i], sh_w3[i], sh_w2[i])
        y_ple = jax.nn.sigmoid(x @ ple_gate_w[i] + ple_gate_b[i]) * (p[i] @ ple_w[i])
        x = layer_norm(DEEPNORM_ALPHA * x + y_moe + y_ple, ln2_g[i], ln2_b[i])
    return x
```

```python
import functools

import jax
import jax.numpy as jnp
from jax import lax
from jax.experimental import pallas as pl
from jax.experimental.pallas import tpu as pltpu

F32 = jnp.float32
BF16 = jnp.bfloat16

D_MODEL = 2048
PLE_DIM = 256
N_HEADS = 16
KV_HEADS = 2
Q_PER_KV = N_HEADS // KV_HEADS
HEAD_DIM = 64
ATTN_WIDTH = N_HEADS * HEAD_DIM
KV_WIDTH = KV_HEADS * HEAD_DIM
ATTN_BLOCK = 128
ROT_DIM = HEAD_DIM // 4
ROPE_THETA = 500000.0
HG_HEADS = 4
HG_HEAD_DIM = 128
HG_WIDTH = HG_HEADS * HG_HEAD_DIM
HG_CHUNK = 64
HG_SUB = 8
LRU_HEADS = 8
LRU_HEAD_DIM = 64
LRU_WIDTH = LRU_HEADS * LRU_HEAD_DIM
CONV_WIDTH = 4
LRU_C = 8.0
IN_SIZES = (ATTN_WIDTH, KV_WIDTH, KV_WIDTH, HG_WIDTH, HG_WIDTH, HG_WIDTH, HG_WIDTH, LRU_WIDTH, LRU_WIDTH)
IN_COLS = sum(IN_SIZES)
MIX_WIDTH = ATTN_WIDTH + HG_WIDTH + LRU_WIDTH
N_EXPERTS = 64
TOPK = 8
N_GROUPS = 8
TOPK_GROUPS = 4
EXPERTS_PER_GROUP = N_EXPERTS // N_GROUPS
EXPERT_DIM = 512
ROUTED_SCALE = 2.5
DEPTH = 2
DEEPNORM_ALPHA = (2 * DEPTH) ** 0.25
RMS_EPS = 1e-6
LN_EPS = 1e-5
NEG_BIG = -1e30
NEG_PICKED = -3e38
TINY = 1e-30

LANES = 128
VMEM_LIMIT = 56 * 1024 * 1024

COL_Q = 0
COL_HG = ATTN_WIDTH
COL_LRU = COL_HG + 4 * HG_WIDTH
COL_KV = COL_LRU + 2 * LRU_WIDTH

INPROJ_TM = 512
INPROJ_TN = IN_COLS // 2
HG_ROWS = 512
LRU_ROWS = 256
OUT_TM = 256
MOE_BM = 256
FIN_TM = 128


def _cparams(sem):
    return pltpu.CompilerParams(dimension_semantics=sem, vmem_limit_bytes=VMEM_LIMIT)


def _inproj_kernel(x_ref, w_ref, o_ref):
    o_ref[...] = jnp.dot(x_ref[...].astype(BF16), w_ref[...], preferred_element_type=F32)


def _inproj(x2d, w):
    t = x2d.shape[0]
    tm = min(INPROJ_TM, t)
    return pl.pallas_call(
        _inproj_kernel,
        grid=(IN_COLS // INPROJ_TN, t // tm),
        in_specs=[pl.BlockSpec((tm, D_MODEL), lambda j, i: (i, 0)),
                  pl.BlockSpec((D_MODEL, INPROJ_TN), lambda j, i: (0, j))],
        out_specs=pl.BlockSpec((tm, INPROJ_TN), lambda j, i: (i, j)),
        out_shape=jax.ShapeDtypeStruct((t, IN_COLS), F32),
        compiler_params=_cparams(("arbitrary", "arbitrary")),
        name="inproj",
    )(x2d, w)


def _attn_kernel(sink_ref, q_ref, kv_ref, cos_ref, sa_ref, sb_ref, gain_ref, o_ref, kprev, vprev):
    n = pl.program_id(1)

    @pl.when(n == 0)
    def _():
        kprev[...] = jnp.zeros_like(kprev)
        vprev[...] = jnp.zeros_like(vprev)

    cos = cos_ref[...]
    sa = sa_ref[...]
    sb = sb_ref[...]

    def rot(t):
        return t * cos + pltpu.roll(t, 8, axis=1) * sa + pltpu.roll(t, LANES - 8, axis=1) * sb

    kc = rot(kv_ref[:, 0:KV_WIDTH])
    vc = kv_ref[:, KV_WIDTH:2 * KV_WIDTH]
    kb = jnp.concatenate([kprev[...], kc], axis=0).astype(BF16)
    vb = jnp.concatenate([vprev[...], vc], axis=0).astype(BF16)

    row = lax.broadcasted_iota(jnp.int32, (ATTN_BLOCK, 2 * ATTN_BLOCK), 0)
    col = lax.broadcasted_iota(jnp.int32, (ATTN_BLOCK, 2 * ATTN_BLOCK), 1)
    rel = row + ATTN_BLOCK - col
    allowed = (rel >= 0) & (rel < ATTN_BLOCK) & ((col >= ATTN_BLOCK) | (n > 0))

    outs = []
    for pair in range(N_HEADS // 2):
        qt = rot(q_ref[:, pair * LANES:(pair + 1) * LANES])
        for sub in range(2):
            h = pair * 2 + sub
            kvh = h // Q_PER_KV
            qh = qt[:, sub * HEAD_DIM:(sub + 1) * HEAD_DIM].astype(BF16)
            kh = kb[:, kvh * HEAD_DIM:(kvh + 1) * HEAD_DIM]
            vh = vb[:, kvh * HEAD_DIM:(kvh + 1) * HEAD_DIM]
            s = lax.dot_general(qh, kh, (((1,), (1,)), ((), ())), preferred_element_type=F32) * (HEAD_DIM ** -0.5)
            s = jnp.where(allowed, s, NEG_BIG)
            sink = sink_ref[h]
            m = jnp.maximum(jnp.max(s, axis=-1, keepdims=True), sink)
            pr = jnp.exp(s - m)
            den = jnp.sum(pr, axis=-1, keepdims=True) + jnp.exp(sink - m)
            probs = pr / den
            outs.append(jnp.dot(probs.astype(BF16), vh, preferred_element_type=F32))
    o = jnp.concatenate(outs, axis=1)
    y = o * lax.rsqrt(jnp.mean(o * o, axis=-1, keepdims=True) + RMS_EPS) * gain_ref[...]
    o_ref[...] = y.astype(o_ref.dtype)
    kprev[...] = kc
    vprev[...] = vc


def _attention(h, sinks, cos_t, sa_t, sb_t, gain, bsz, seq):
    t = bsz * seq
    nb = seq // ATTN_BLOCK
    rowmap = lambda b, n: (b * nb + n, 0)
    return pl.pallas_call(
        _attn_kernel,
        grid=(bsz, nb),
        in_specs=[pl.BlockSpec(memory_space=pltpu.SMEM),
                  pl.BlockSpec((ATTN_BLOCK, ATTN_WIDTH), lambda b, n: (b * nb + n, COL_Q // ATTN_WIDTH)),
                  pl.BlockSpec((ATTN_BLOCK, 2 * KV_WIDTH), lambda b, n: (b * nb + n, COL_KV // (2 * KV_WIDTH))),
                  pl.BlockSpec((ATTN_BLOCK, LANES), rowmap),
                  pl.BlockSpec((ATTN_BLOCK, LANES), rowmap),
                  pl.BlockSpec((ATTN_BLOCK, LANES), rowmap),
                  pl.BlockSpec((1, ATTN_WIDTH), lambda b, n: (0, 0))],
        out_specs=pl.BlockSpec((ATTN_BLOCK, ATTN_WIDTH), rowmap),
        out_shape=jax.ShapeDtypeStruct((t, ATTN_WIDTH), BF16),
        scratch_shapes=[pltpu.VMEM((ATTN_BLOCK, KV_WIDTH), F32), pltpu.VMEM((ATTN_BLOCK, KV_WIDTH), F32)],
        compiler_params=_cparams(("arbitrary", "arbitrary")),
        name="attn",
    )(sinks, h, h, cos_t, sa_t, sb_t, gain)


def _cumsum_rows(x):
    rows = x.shape[0]
    row = lax.broadcasted_iota(jnp.int32, x.shape, 0)
    d = 1
    while d < rows:
        x = x + jnp.where(row >= d, pltpu.roll(x, d, axis=0), 0.0)
        d *= 2
    return x


def _hgrn_kernel(in_ref, lb_ref, gain_ref, o_ref, st_ref):
    r = pl.program_id(2)

    @pl.when(r == 0)
    def _():
        st_ref[...] = jnp.zeros_like(st_ref)

    c = HG_CHUNK
    nsub = c // HG_SUB
    lb = lb_ref[...]
    gain = gain_ref[...]
    ones_kk = jnp.ones((HG_HEAD_DIM, HG_HEAD_DIM), BF16)
    row_c = lax.broadcasted_iota(jnp.int32, (c, HG_HEAD_DIM), 0)
    row_s = lax.broadcasted_iota(jnp.int32, (HG_SUB, HG_HEAD_DIM), 0)
    nt = (((1,), (1,)), ((), ()))
    tn = (((0,), (0,)), ((), ()))

    def chunk(ci, carry):
        r0 = pl.multiple_of(ci * c, c)
        q = in_ref[pl.ds(r0, c), 0:HG_HEAD_DIM]
        fp = in_ref[pl.ds(r0, c), HG_HEAD_DIM:2 * HG_HEAD_DIM]
        v = in_ref[pl.ds(r0, c), 2 * HG_HEAD_DIM:3 * HG_HEAD_DIM]
        g = in_ref[pl.ds(r0, c), 3 * HG_HEAD_DIM:4 * HG_HEAD_DIM]
        qf = jax.nn.silu(q)
        f = lb + (1.0 - lb) * jax.nn.sigmoid(fp)
        logf = jnp.log(jnp.maximum(f, TINY))
        kf = (1.0 - lb) * jax.nn.sigmoid(-fp)
        b = _cumsum_rows(logf)
        vb = v.astype(BF16)

        s_rows = [jnp.zeros((HG_SUB, c), F32)]
        for i in range(1, nsub):
            lo = i * HG_SUB
            ref_b = b[lo - 1:lo, :]
            qi = qf[lo:lo + HG_SUB, :] * jnp.exp(b[lo:lo + HG_SUB, :] - ref_b)
            ki = kf * jnp.exp(jnp.where(row_c < lo, ref_b - b, NEG_BIG))
            s_rows.append(lax.dot_general(qi.astype(BF16), ki.astype(BF16), nt, preferred_element_type=F32))
        scores = jnp.concatenate(s_rows, axis=0)
        o = jnp.dot(scores.astype(BF16), vb, preferred_element_type=F32)

        o_diag = []
        for j in range(nsub):
            lo = j * HG_SUB
            bs = b[lo:lo + HG_SUB, :]
            ks = kf[lo:lo + HG_SUB, :]
            qs = qf[lo:lo + HG_SUB, :]
            vs = v[lo:lo + HG_SUB, :]
            tiles = []
            for tt in range(HG_SUB):
                e = jnp.where(row_s <= tt, bs[tt:tt + 1, :] - bs, NEG_BIG)
                tiles.append(jnp.exp(e) * ks * qs[tt:tt + 1, :])
            w = jnp.concatenate(tiles, axis=0).astype(BF16)
            dfull = jnp.dot(w, ones_kk, preferred_element_type=F32)
            contrib = dfull.reshape(HG_SUB, HG_SUB, HG_HEAD_DIM) * vs[None, :, :]
            o_diag.append(jnp.sum(contrib, axis=1))
        o = o + jnp.concatenate(o_diag, axis=0)

        st = st_ref[...]
        qb = (qf * jnp.exp(b)).astype(BF16)
        o = o + lax.dot_general(qb, st.astype(BF16), nt, preferred_element_type=F32)
        b_last = b[c - 1:c, :]
        kn = (kf * jnp.exp(b_last - b)).astype(BF16)
        st_ref[...] = st * jnp.exp(b_last) + lax.dot_general(vb, kn, tn, preferred_element_type=F32)

        y = o * lax.rsqrt(jnp.mean(o * o, axis=-1, keepdims=True) + RMS_EPS) * gain
        o_ref[pl.ds(r0, c), :] = (y * jax.nn.silu(g)).astype(o_ref.dtype)
        return carry

    lax.fori_loop(0, in_ref.shape[0] // c, chunk, 0)


def _hgrn2(h, lb, gain, bsz, seq):
    t = bsz * seq
    rows = min(HG_ROWS, seq)
    nr = seq // rows
    hd_block0 = COL_HG // (4 * HG_HEAD_DIM)
    return pl.pallas_call(
        _hgrn_kernel,
        grid=(bsz, HG_HEADS, nr),
        in_specs=[pl.BlockSpec((rows, 4 * HG_HEAD_DIM), lambda b, hd, r: (b * nr + r, hd_block0 + hd)),
                  pl.BlockSpec((1, HG_HEAD_DIM), lambda b, hd, r: (0, hd)),
                  pl.BlockSpec((1, HG_HEAD_DIM), lambda b, hd, r: (0, hd))],
        out_specs=pl.BlockSpec((rows, HG_HEAD_DIM), lambda b, hd, r: (b * nr + r, hd)),
        out_shape=jax.ShapeDtypeStruct((t, HG_WIDTH), BF16),
        scratch_shapes=[pltpu.VMEM((HG_HEAD_DIM, HG_HEAD_DIM), F32)],
        compiler_params=_cparams(("arbitrary", "arbitrary", "arbitrary")),
        name="hgrn2",
    )(h, lb, gain)


def _lru_kernel(in_ref, cw_ref, cb_ref, wa_ref, ba_ref, wx_ref, bx_ref, lam_ref, gain_ref, o_ref,
                tail_ref, h_ref):
    r = pl.program_id(1)

    @pl.when(r == 0)
    def _():
        tail_ref[...] = jnp.zeros_like(tail_ref)
        h_ref[...] = jnp.zeros_like(h_ref)

    rows = in_ref.shape[0]
    x = in_ref[:, 0:LRU_WIDTH]
    gr = in_ref[:, LRU_WIDTH:2 * LRU_WIDTH]
    tail = tail_ref[...]
    row8 = lax.broadcasted_iota(jnp.int32, (8, LRU_WIDTH), 0)

    xc = x * cw_ref[CONV_WIDTH - 1:CONV_WIDTH, :] + cb_ref[...]
    for k in range(1, CONV_WIDTH):
        xs = pltpu.roll(x, k, axis=0)
        head = jnp.where(row8 < k, pltpu.roll(tail, k, axis=0), xs[0:8, :])
        xs = jnp.concatenate([head, xs[8:, :]], axis=0)
        xc = xc + xs * cw_ref[CONV_WIDTH - 1 - k:CONV_WIDTH - k, :]
    tail_ref[...] = x[rows - 8:rows, :]

    xcb = xc.astype(BF16)
    rg = jax.nn.sigmoid(jnp.dot(xcb, wa_ref[...], preferred_element_type=F32) + ba_ref[...])
    ig = jax.nn.sigmoid(jnp.dot(xcb, wx_ref[...], preferred_element_type=F32) + bx_ref[...])
    log_a = -LRU_C * rg * jax.nn.softplus(-lam_ref[...])
    a = jnp.exp(log_a)
    th = jnp.tanh(log_a)
    neg_expm1 = -2.0 * th / (1.0 - th)
    u = jnp.sqrt(jnp.maximum(neg_expm1, 0.0)) * (ig * xc)

    row = lax.broadcasted_iota(jnp.int32, (rows, LRU_WIDTH), 0)
    d = 1
    while d < rows:
        keep = row >= d
        a_s = jnp.where(keep, pltpu.roll(a, d, axis=0), 1.0)
        u_s = jnp.where(keep, pltpu.roll(u, d, axis=0), 0.0)
        u = a * u_s + u
        a = a * a_s
        d *= 2
    hcur = u + a * h_ref[0:1, :]
    h_ref[...] = jnp.broadcast_to(hcur[rows - 1:rows, :], h_ref.shape)

    y = hcur * lax.rsqrt(jnp.mean(hcur * hcur, axis=-1, keepdims=True) + RMS_EPS) * gain_ref[...]
    o_ref[...] = (y * jax.nn.gelu(gr)).astype(o_ref.dtype)


def _rglru(h, cw, cb, wa, ba, wx, bx, lam, gain, bsz, seq):
    t = bsz * seq
    rows = min(LRU_ROWS, seq)
    nr = seq // rows
    vec = pl.BlockSpec((1, LRU_WIDTH), lambda b, r: (0, 0))
    mat = pl.BlockSpec((LRU_WIDTH, LRU_WIDTH), lambda b, r: (0, 0))
    return pl.pallas_call(
        _lru_kernel,
        grid=(bsz, nr),
        in_specs=[pl.BlockSpec((rows, 2 * LRU_WIDTH), lambda b, r: (b * nr + r, COL_LRU // (2 * LRU_WIDTH))),
                  pl.BlockSpec((CONV_WIDTH, LRU_WIDTH), lambda b, r: (0, 0)),
                  vec, mat, vec, mat, vec, vec, vec],
        out_specs=pl.BlockSpec((rows, LRU_WIDTH), lambda b, r: (b * nr + r, 0)),
        out_shape=jax.ShapeDtypeStruct((t, LRU_WIDTH), BF16),
        scratch_shapes=[pltpu.VMEM((8, LRU_WIDTH), F32), pltpu.VMEM((8, LRU_WIDTH), F32)],
        compiler_params=_cparams(("arbitrary", "arbitrary")),
        name="rglru",
    )(h, cw, cb, wa, ba, wx, bx, lam, gain)


def _layer_norm(z, g, b):
    mu = jnp.mean(z, axis=-1, keepdims=True)
    zc = z - mu
    var = jnp.mean(zc * zc, axis=-1, keepdims=True)
    return zc * lax.rsqrt(var + LN_EPS) * g + b


def _outproj_kernel(ya_ref, yh_ref, yl_ref, x_ref, wa_ref, wh_ref, wl_ref, g_ref, b_ref, rw_ref, rb_ref,
                    x1_ref, idx_ref, gate_ref):
    mixed = jnp.dot(ya_ref[...], wa_ref[...], preferred_element_type=F32)
    mixed = mixed + jnp.dot(yh_ref[...], wh_ref[...], preferred_element_type=F32)
    mixed = mixed + jnp.dot(yl_ref[...], wl_ref[...], preferred_element_type=F32)
    x1 = _layer_norm(DEEPNORM_ALPHA * x_ref[...] + mixed, g_ref[...], b_ref[...])
    x1_ref[...] = x1

    tm = x1.shape[0]
    logits = lax.dot_general(rw_ref[...], x1, (((1,), (1,)), ((), ())),
                             precision=lax.Precision.HIGHEST, preferred_element_type=F32)
    scores = jax.nn.sigmoid(logits)
    sel = scores + rb_ref[...]
    shp = (N_GROUPS, EXPERTS_PER_GROUP, tm)
    s3 = sel.reshape(shp)
    sc3 = scores.reshape(shp)
    e_in = lax.broadcasted_iota(jnp.int32, shp, 1)
    g_id = lax.broadcasted_iota(jnp.int32, shp, 0)
    e_id = g_id * EXPERTS_PER_GROUP + e_in

    m1 = jnp.max(s3, axis=1, keepdims=True)
    i1 = jnp.min(jnp.where(s3 == m1, e_in, EXPERTS_PER_GROUP), axis=1, keepdims=True)
    m2 = jnp.max(jnp.where(e_in == i1, NEG_PICKED, s3), axis=1, keepdims=True)
    gs = m1 + m2
    g1 = lax.broadcasted_iota(jnp.int32, gs.shape, 0)
    gsel = jnp.zeros(gs.shape, jnp.int32)
    cur = gs
    for _ in range(TOPK_GROUPS):
        m = jnp.max(cur, axis=0, keepdims=True)
        pick = g1 == jnp.min(jnp.where(cur == m, g1, N_GROUPS), axis=0, keepdims=True)
        gsel = jnp.where(pick, 1, gsel)
        cur = jnp.where(pick, NEG_PICKED, cur)
    cur = jnp.where(gsel > 0, s3, NEG_BIG)

    idx_rows, w_rows = [], []
    for _ in range(TOPK):
        m = jnp.max(jnp.max(cur, axis=1, keepdims=True), axis=0, keepdims=True)
        cand = jnp.where(cur == m, e_id, N_EXPERTS)
        ii = jnp.min(jnp.min(cand, axis=1, keepdims=True), axis=0, keepdims=True)
        pick = e_id == ii
        wsel = jnp.where(pick, sc3, 0.0)
        w_rows.append(jnp.sum(jnp.sum(wsel, axis=1, keepdims=True), axis=0, keepdims=True).reshape(1, tm))
        idx_rows.append(ii.reshape(1, tm))
        cur = jnp.where(pick, NEG_PICKED, cur)
    w = jnp.concatenate(w_rows, axis=0)
    idx_ref[...] = jnp.concatenate(idx_rows, axis=0)
    gate_ref[...] = w / jnp.sum(w, axis=0, keepdims=True) * ROUTED_SCALE


def _outproj_router(ya, yh, yl, x2d, w_out, ln_g, ln_b, rw_t, rb):
    t = x2d.shape[0]
    tm = min(OUT_TM, t)
    rowb = lambda w: pl.BlockSpec((tm, w), lambda i: (i, 0))
    full = lambda a, b: pl.BlockSpec((a, b), lambda i: (0, 0))
    wa = pl.BlockSpec((ATTN_WIDTH, D_MODEL), lambda i: (0, 0))
    wh = pl.BlockSpec((HG_WIDTH, D_MODEL), lambda i: (ATTN_WIDTH // HG_WIDTH, 0))
    wl = pl.BlockSpec((LRU_WIDTH, D_MODEL), lambda i: ((ATTN_WIDTH + HG_WIDTH) // LRU_WIDTH, 0))
    return pl.pallas_call(
        _outproj_kernel,
        grid=(t // tm,),
        in_specs=[rowb(ATTN_WIDTH), rowb(HG_WIDTH), rowb(LRU_WIDTH), rowb(D_MODEL), wa, wh, wl,
                  full(1, D_MODEL), full(1, D_MODEL), full(N_EXPERTS, D_MODEL), full(N_EXPERTS, 1)],
        out_specs=[rowb(D_MODEL), pl.BlockSpec((TOPK, tm), lambda i: (0, i)),
                   pl.BlockSpec((TOPK, tm), lambda i: (0, i))],
        out_shape=[jax.ShapeDtypeStruct((t, D_MODEL), F32), jax.ShapeDtypeStruct((TOPK, t), jnp.int32),
                   jax.ShapeDtypeStruct((TOPK, t), F32)],
        compiler_params=_cparams(("arbitrary",)),
        name="outproj_router",
    )(ya, yh, yl, x2d, w_out, w_out, w_out, ln_g, ln_b, rw_t, rb)


def _moe_kernel(be_ref, nused_ref, tok_hbm, gate_ref, x_hbm, w1_ref, w3_ref, w2_ref, y_ref,
                tok_smem, xbuf, w1b, w3b, w2b, sem_idx, sem_rows):
    n = pl.program_id(0)
    bm = xbuf.shape[0]

    @pl.when(n < nused_ref[0])
    def _():
        idx_cp = pltpu.make_async_copy(tok_hbm.at[n], tok_smem, sem_idx)
        idx_cp.start()
        idx_cp.wait()

        def issue(r, carry):
            tok = tok_smem[r]
            pltpu.make_async_copy(x_hbm.at[pl.ds(tok, 1)], xbuf.at[pl.ds(r, 1)], sem_rows).start()
            return carry

        lax.fori_loop(0, bm, issue, 0)

        e = be_ref[n]
        e_prev = be_ref[jnp.maximum(n - 1, 0)]

        @pl.when((n == 0) | (e != e_prev))
        def _():
            w1b[...] = w1_ref[...].astype(BF16)
            w3b[...] = w3_ref[...].astype(BF16)
            w2b[...] = w2_ref[...].astype(BF16)

        pltpu.make_async_copy(x_hbm.at[pl.ds(0, bm)], xbuf, sem_rows).wait()
        xb = xbuf[...].astype(BF16)
        h1 = jnp.dot(xb, w1b[...], preferred_element_type=F32)
        h3 = jnp.dot(xb, w3b[...], preferred_element_type=F32)
        act = (jax.nn.silu(h1) * h3).astype(BF16)
        y_ref[...] = jnp.dot(act, w2b[...], preferred_element_type=F32) * gate_ref[...]

    @pl.when(n >= nused_ref[0])
    def _():
        y_ref[...] = jnp.zeros_like(y_ref)


def _moe(layer, block_expert, n_used, row_tok, row_gate, x1, w1, w3, w2):
    n_blocks, bm = row_tok.shape
    wspec = lambda a, b: pl.BlockSpec((None, None, a, b), lambda n, be, nu: (layer, be[n], 0, 0))
    grid_spec = pltpu.PrefetchScalarGridSpec(
        num_scalar_prefetch=2,
        grid=(n_blocks,),
        in_specs=[pl.BlockSpec(memory_space=pl.ANY),
                  pl.BlockSpec((bm, 1), lambda n, be, nu: (n, 0)),
                  pl.BlockSpec(memory_space=pl.ANY),
                  wspec(D_MODEL, EXPERT_DIM), wspec(D_MODEL, EXPERT_DIM), wspec(EXPERT_DIM, D_MODEL)],
        out_specs=pl.BlockSpec((bm, D_MODEL), lambda n, be, nu: (n, 0)),
        scratch_shapes=[pltpu.SMEM((bm,), jnp.int32),
                        pltpu.VMEM((bm, D_MODEL), F32),
                        pltpu.VMEM((D_MODEL, EXPERT_DIM), BF16),
                        pltpu.VMEM((D_MODEL, EXPERT_DIM), BF16),
                        pltpu.VMEM((EXPERT_DIM, D_MODEL), BF16),
                        pltpu.SemaphoreType.DMA,
                        pltpu.SemaphoreType.DMA],
    )
    return pl.pallas_call(
        _moe_kernel,
        grid_spec=grid_spec,
        out_shape=jax.ShapeDtypeStruct((n_blocks * bm, D_MODEL), F32),
        compiler_params=_cparams(("arbitrary",)),
        name="moe_experts",
    )(block_expert, n_used, row_tok, row_gate, x1, w1, w3, w2)


def _final_kernel(dest_hbm, y_hbm, x_ref, p_ref, sw1_ref, sw3_ref, sw2_ref, pw_ref, gw_ref, gb_ref, g_ref, b_ref,
                  o_ref, dest_smem, ybuf, sem_idx, sem_rows):
    i = pl.program_id(0)
    tm = x_ref.shape[0]
    idx_cp = pltpu.make_async_copy(dest_hbm.at[i], dest_smem, sem_idx)
    idx_cp.start()
    idx_cp.wait()

    def issue(r, carry):
        for j in range(TOPK):
            d = dest_smem[j * tm + r]
            pltpu.make_async_copy(y_hbm.at[pl.ds(d, 1)], ybuf.at[j, pl.ds(r, 1)], sem_rows).start()
        return carry

    lax.fori_loop(0, tm, issue, 0)

    x = x_ref[...]
    xb = x.astype(BF16)
    h1 = jnp.dot(xb, sw1_ref[...], preferred_element_type=F32)
    h3 = jnp.dot(xb, sw3_ref[...], preferred_element_type=F32)
    shared = jnp.dot((jax.nn.silu(h1) * h3).astype(BF16), sw2_ref[...], preferred_element_type=F32)
    gate = jax.nn.sigmoid(jnp.dot(xb, gw_ref[...], preferred_element_type=F32) + gb_ref[...])
    ple = gate * jnp.dot(p_ref[...].astype(BF16), pw_ref[...], preferred_element_type=F32)

    for j in range(TOPK):
        pltpu.make_async_copy(y_hbm.at[pl.ds(0, tm)], ybuf.at[j], sem_rows).wait()
    routed = ybuf[0]
    for j in range(1, TOPK):
        routed = routed + ybuf[j]
    z = DEEPNORM_ALPHA * x + (routed + shared) + ple
    o_ref[...] = _layer_norm(z, g_ref[...], b_ref[...])


def _final(layer, dest_tiles, y_sorted, x1, p3, sw1, sw3, sw2, pw, gw, gb, ln_g, ln_b):
    t = x1.shape[0]
    tm = dest_tiles.shape[1] // TOPK
    full = lambda a, b: pl.BlockSpec((a, b), lambda i: (0, 0))
    return pl.pallas_call(
        _final_kernel,
        grid=(t // tm,),
        in_specs=[pl.BlockSpec(memory_space=pl.ANY), pl.BlockSpec(memory_space=pl.ANY),
                  pl.BlockSpec((tm, D_MODEL), lambda i: (i, 0)),
                  pl.BlockSpec((None, tm, PLE_DIM), lambda i: (layer, i, 0)),
                  full(D_MODEL, EXPERT_DIM), full(D_MODEL, EXPERT_DIM), full(EXPERT_DIM, D_MODEL),
                  full(PLE_DIM, D_MODEL), full(D_MODEL, D_MODEL), full(1, D_MODEL),
                  full(1, D_MODEL), full(1, D_MODEL)],
        out_specs=pl.BlockSpec((tm, D_MODEL), lambda i: (i, 0)),
        out_shape=jax.ShapeDtypeStruct((t, D_MODEL), F32),
        scratch_shapes=[pltpu.SMEM((TOPK * tm,), jnp.int32),
                        pltpu.VMEM((TOPK, tm, D_MODEL), F32),
                        pltpu.SemaphoreType.DMA,
                        pltpu.SemaphoreType.DMA],
        compiler_params=_cparams(("arbitrary",)),
        name="shared_ple_combine",
    )(dest_tiles, y_sorted, x1, p3, sw1, sw3, sw2, pw, gw, gb, ln_g, ln_b)


def _dispatch_plan(idx_t, gate_t, bm, tm_final):
    t = idx_t.shape[1]
    a = t * TOPK
    n_blocks = a // bm + N_EXPERTS
    onehot = jnp.zeros((N_EXPERTS, t), jnp.int32)
    for j in range(TOPK):
        onehot = onehot + (idx_t[j][None, :] == jnp.arange(N_EXPERTS, dtype=jnp.int32)[:, None]).astype(jnp.int32)
    cum = jnp.cumsum(onehot, axis=1)
    counts = cum[:, -1]
    rank = jnp.take_along_axis(cum - onehot, idx_t, axis=0)
    padded = ((counts + bm - 1) // bm) * bm
    pend = jnp.cumsum(padded)
    dest = (pend - padded)[idx_t] + rank
    tok = jnp.broadcast_to(jnp.arange(t, dtype=jnp.int32)[None, :], (TOPK, t))
    flat = dest.reshape(-1)
    row_tok = jnp.zeros((n_blocks * bm,), jnp.int32).at[flat].set(tok.reshape(-1))
    row_gate = jnp.zeros((n_blocks * bm,), F32).at[flat].set(gate_t.reshape(-1))
    block_expert = jnp.clip(jnp.searchsorted(pend, jnp.arange(n_blocks, dtype=jnp.int32) * bm, side='right'),
                            0, N_EXPERTS - 1).astype(jnp.int32)
    n_used = (pend[-1] // bm).astype(jnp.int32).reshape(1)
    dest_tiles = dest.reshape(TOPK, t // tm_final, tm_final).transpose(1, 0, 2).reshape(t // tm_final, TOPK * tm_final)
    return block_expert, n_used, row_tok.reshape(n_blocks, bm), row_gate.reshape(n_blocks * bm, 1), dest_tiles


def _rotary_lane_tables(positions):
    inv_freq = ROPE_THETA ** (-jnp.arange(0, ROT_DIM, 2, dtype=F32) / ROT_DIM)
    ang = positions.astype(F32).reshape(-1)[:, None] * inv_freq
    cos, sin = jnp.cos(ang), jnp.sin(ang)
    half = ROT_DIM // 2
    t = ang.shape[0]
    one = jnp.ones((t, HEAD_DIM - ROT_DIM), F32)
    zero = jnp.zeros((t, HEAD_DIM - ROT_DIM), F32)
    zh = jnp.zeros((t, half), F32)
    cos64 = jnp.concatenate([cos, cos, one], axis=1)
    sa64 = jnp.concatenate([zh, sin, zero], axis=1)
    sb64 = jnp.concatenate([-sin, zh, zero], axis=1)
    tile2 = lambda m: jnp.concatenate([m, m], axis=1)
    return tile2(cos64), tile2(sa64), tile2(sb64)


def _permute_in_cols(w):
    off = [0]
    for s in IN_SIZES:
        off.append(off[-1] + s)
    aq, ak, av, hq, hf, hi, hg, lx, lg = [w[:, off[k]:off[k + 1]] for k in range(9)]
    parts = [aq]
    for hd in range(HG_HEADS):
        sl = slice(hd * HG_HEAD_DIM, (hd + 1) * HG_HEAD_DIM)
        parts += [hq[:, sl], hf[:, sl], hi[:, sl], hg[:, sl]]
    parts += [lx, lg, ak, av]
    return jnp.concatenate(parts, axis=1)


def _block_diag(w):
    hds, d, _ = w.shape
    eye = jnp.eye(hds, dtype=w.dtype)
    return (eye[:, None, :, None] * w[:, :, None, :]).reshape(hds * d, hds * d)


def kernel(x, p, positions, w_in, w_out, attn_sinks, attn_norm, hg_lb_logits, hg_norm, lru_conv_w, lru_conv_b, lru_wa, lru_ba, lru_wx, lru_bx, lru_lambda, lru_norm, ln1_g, ln1_b, router_w, router_b, exp_w1, exp_w3, exp_w2, sh_w1, sh_w3, sh_w2, ple_w, ple_gate_w, ple_gate_b, ln2_g, ln2_b):
    bsz, seq, _ = x.shape
    t = bsz * seq
    depth = w_in.shape[0]
    lb_sm = jax.nn.softmax(hg_lb_logits.astype(F32), axis=0)
    hg_lb = jnp.maximum(jnp.cumsum(lb_sm, axis=0) - lb_sm[0], 0.0)
    cos_t, sa_t, sb_t = _rotary_lane_tables(positions)
    p3 = p.reshape(depth, t, PLE_DIM)
    fin_tm = min(FIN_TM, t)
    bm = min(MOE_BM, t)
    row = lambda v: v.reshape(1, -1)

    xc = x.reshape(t, D_MODEL)
    for i in range(depth):
        h = _inproj(xc, _permute_in_cols(w_in[i].astype(BF16)))
        ya = _attention(h, attn_sinks[i], cos_t, sa_t, sb_t, row(attn_norm[i]), bsz, seq)
        yh = _hgrn2(h, row(hg_lb[i]), row(hg_norm[i]), bsz, seq)
        yl = _rglru(h, lru_conv_w[i], row(lru_conv_b[i]), _block_diag(lru_wa[i]).astype(BF16), row(lru_ba[i]),
                    _block_diag(lru_wx[i]).astype(BF16), row(lru_bx[i]), row(lru_lambda[i]), row(lru_norm[i]),
                    bsz, seq)
        x1, idx_t, gate_t = _outproj_router(ya, yh, yl, xc, w_out[i].astype(BF16), row(ln1_g[i]), row(ln1_b[i]),
                                            router_w[i].T, router_b[i].reshape(N_EXPERTS, 1))
        block_expert, n_used, row_tok, row_gate, dest_tiles = _dispatch_plan(idx_t, gate_t, bm, fin_tm)
        y_sorted = _moe(i, block_expert, n_used, row_tok, row_gate, x1, exp_w1, exp_w3, exp_w2)
        xc = _final(i, dest_tiles, y_sorted, x1, p3, sh_w1[i].astype(BF16), sh_w3[i].astype(BF16),
                    sh_w2[i].astype(BF16), ple_w[i].astype(BF16), ple_gate_w[i].astype(BF16),
                    row(ple_gate_b[i]), row(ln2_g[i]), row(ln2_b[i]))
    return xc.reshape(bsz, seq, D_MODEL)
```

```python
import jax
import jax.numpy as jnp
from jax import lax
from jax.experimental import pallas as pl
from jax.experimental.pallas import tpu as pltpu

F32 = jnp.float32
BF16 = jnp.bfloat16

D_MODEL = 2048
PLE_DIM = 256
N_HEADS = 16
KV_HEADS = 2
Q_PER_KV = N_HEADS // KV_HEADS
HEAD_DIM = 64
ATTN_WIDTH = N_HEADS * HEAD_DIM
KV_WIDTH = KV_HEADS * HEAD_DIM
ATTN_BLOCK = 128
ROT_DIM = HEAD_DIM // 4
ROPE_THETA = 500000.0
HG_HEADS = 4
HG_HEAD_DIM = 128
HG_WIDTH = HG_HEADS * HG_HEAD_DIM
HG_CHUNK = 64
HG_SUB = 8
LRU_HEADS = 8
LRU_HEAD_DIM = 64
LRU_WIDTH = LRU_HEADS * LRU_HEAD_DIM
CONV_WIDTH = 4
LRU_C = 8.0
IN_SIZES = (ATTN_WIDTH, KV_WIDTH, KV_WIDTH, HG_WIDTH, HG_WIDTH, HG_WIDTH, HG_WIDTH, LRU_WIDTH, LRU_WIDTH)
IN_COLS = sum(IN_SIZES)
MIX_WIDTH = ATTN_WIDTH + HG_WIDTH + LRU_WIDTH
N_EXPERTS = 64
TOPK = 8
N_GROUPS = 8
TOPK_GROUPS = 4
EXPERTS_PER_GROUP = N_EXPERTS // N_GROUPS
EXPERT_DIM = 512
ROUTED_SCALE = 2.5
DEPTH = 2
DEEPNORM_ALPHA = (2 * DEPTH) ** 0.25
RMS_EPS = 1e-6
LN_EPS = 1e-5
NEG_BIG = -1e30
NEG_PICKED = -3e38
TINY = 1e-30

LANES = 128
ROW_TILES = D_MODEL // LANES
VMEM_LIMIT = 56 * 1024 * 1024

COL_Q = 0
COL_HG = ATTN_WIDTH
COL_LRU = COL_HG + 4 * HG_WIDTH
COL_KV = COL_LRU + 2 * LRU_WIDTH

INPROJ_TM = 512
INPROJ_TN = IN_COLS // 2
HG_ROWS = 512
LRU_ROWS = 256
OUT_TM = 256
MOE_BM = 256
IDX_TILE = 128
DISP_TM = 256
FIN_TM = IDX_TILE


def _cparams(sem):
    return pltpu.CompilerParams(dimension_semantics=sem, vmem_limit_bytes=VMEM_LIMIT)


def _to_row_tiles(ref, val):
    for s in range(ROW_TILES):
        ref[:, s, :] = val[:, s * LANES:(s + 1) * LANES]


def _from_row_tiles(ref):
    return jnp.concatenate([ref[:, s, :] for s in range(ROW_TILES)], axis=1)


def _inproj_kernel(x_ref, w_ref, o_ref):
    o_ref[...] = jnp.dot(x_ref[...].astype(BF16), w_ref[...], preferred_element_type=F32)


def _inproj(x2d, w):
    t = x2d.shape[0]
    tm = min(INPROJ_TM, t)
    return pl.pallas_call(
        _inproj_kernel,
        grid=(IN_COLS // INPROJ_TN, t // tm),
        in_specs=[pl.BlockSpec((tm, D_MODEL), lambda j, i: (i, 0)),
                  pl.BlockSpec((D_MODEL, INPROJ_TN), lambda j, i: (0, j))],
        out_specs=pl.BlockSpec((tm, INPROJ_TN), lambda j, i: (i, j)),
        out_shape=jax.ShapeDtypeStruct((t, IN_COLS), F32),
        compiler_params=_cparams(("arbitrary", "arbitrary")),
        name="inproj",
    )(x2d, w)


def _attn_kernel(sink_ref, q_ref, kv_ref, cos_ref, sa_ref, sb_ref, gain_ref, o_ref, kprev, vprev):
    n = pl.program_id(1)

    @pl.when(n == 0)
    def _():
        kprev[...] = jnp.zeros_like(kprev)
        vprev[...] = jnp.zeros_like(vprev)

    cos = cos_ref[...]
    sa = sa_ref[...]
    sb = sb_ref[...]

    def rot(t):
        return t * cos + pltpu.roll(t, 8, axis=1) * sa + pltpu.roll(t, LANES - 8, axis=1) * sb

    kc = rot(kv_ref[:, 0:KV_WIDTH])
    vc = kv_ref[:, KV_WIDTH:2 * KV_WIDTH]
    kb = jnp.concatenate([kprev[...], kc], axis=0).astype(BF16)
    vb = jnp.concatenate([vprev[...], vc], axis=0).astype(BF16)

    row = lax.broadcasted_iota(jnp.int32, (ATTN_BLOCK, 2 * ATTN_BLOCK), 0)
    col = lax.broadcasted_iota(jnp.int32, (ATTN_BLOCK, 2 * ATTN_BLOCK), 1)
    rel = row + ATTN_BLOCK - col
    allowed = (rel >= 0) & (rel < ATTN_BLOCK) & ((col >= ATTN_BLOCK) | (n > 0))

    outs = []
    for pair in range(N_HEADS // 2):
        qt = rot(q_ref[:, pair * LANES:(pair + 1) * LANES])
        for sub in range(2):
            h = pair * 2 + sub
            kvh = h // Q_PER_KV
            qh = qt[:, sub * HEAD_DIM:(sub + 1) * HEAD_DIM].astype(BF16)
            kh = kb[:, kvh * HEAD_DIM:(kvh + 1) * HEAD_DIM]
            vh = vb[:, kvh * HEAD_DIM:(kvh + 1) * HEAD_DIM]
            s = lax.dot_general(qh, kh, (((1,), (1,)), ((), ())), preferred_element_type=F32) * (HEAD_DIM ** -0.5)
            s = jnp.where(allowed, s, NEG_BIG)
            sink = sink_ref[h]
            m = jnp.maximum(jnp.max(s, axis=-1, keepdims=True), sink)
            pr = jnp.exp(s - m)
            den = jnp.sum(pr, axis=-1, keepdims=True) + jnp.exp(sink - m)
            probs = pr / den
            outs.append(jnp.dot(probs.astype(BF16), vh, preferred_element_type=F32))
    o = jnp.concatenate(outs, axis=1)
    y = o * lax.rsqrt(jnp.mean(o * o, axis=-1, keepdims=True) + RMS_EPS) * gain_ref[...]
    o_ref[...] = y.astype(o_ref.dtype)
    kprev[...] = kc
    vprev[...] = vc


def _attention(h, sinks, cos_t, sa_t, sb_t, gain, bsz, seq):
    t = bsz * seq
    nb = seq // ATTN_BLOCK
    rowmap = lambda b, n: (b * nb + n, 0)
    return pl.pallas_call(
        _attn_kernel,
        grid=(bsz, nb),
        in_specs=[pl.BlockSpec(memory_space=pltpu.SMEM),
                  pl.BlockSpec((ATTN_BLOCK, ATTN_WIDTH), lambda b, n: (b * nb + n, COL_Q // ATTN_WIDTH)),
                  pl.BlockSpec((ATTN_BLOCK, 2 * KV_WIDTH), lambda b, n: (b * nb + n, COL_KV // (2 * KV_WIDTH))),
                  pl.BlockSpec((ATTN_BLOCK, LANES), rowmap),
                  pl.BlockSpec((ATTN_BLOCK, LANES), rowmap),
                  pl.BlockSpec((ATTN_BLOCK, LANES), rowmap),
                  pl.BlockSpec((1, ATTN_WIDTH), lambda b, n: (0, 0))],
        out_specs=pl.BlockSpec((ATTN_BLOCK, ATTN_WIDTH), rowmap),
        out_shape=jax.ShapeDtypeStruct((t, ATTN_WIDTH), BF16),
        scratch_shapes=[pltpu.VMEM((ATTN_BLOCK, KV_WIDTH), F32), pltpu.VMEM((ATTN_BLOCK, KV_WIDTH), F32)],
        compiler_params=_cparams(("arbitrary", "arbitrary")),
        name="attn",
    )(sinks, h, h, cos_t, sa_t, sb_t, gain)


def _cumsum_rows(x):
    rows = x.shape[0]
    row = lax.broadcasted_iota(jnp.int32, x.shape, 0)
    d = 1
    while d < rows:
        x = x + jnp.where(row >= d, pltpu.roll(x, d, axis=0), 0.0)
        d *= 2
    return x


def _hgrn_kernel(in_ref, lb_ref, gain_ref, o_ref, st_ref):
    r = pl.program_id(2)

    @pl.when(r == 0)
    def _():
        st_ref[...] = jnp.zeros_like(st_ref)

    c = HG_CHUNK
    nsub = c // HG_SUB
    lb = lb_ref[...]
    gain = gain_ref[...]
    ones_kk = jnp.ones((HG_HEAD_DIM, HG_HEAD_DIM), BF16)
    row_c = lax.broadcasted_iota(jnp.int32, (c, HG_HEAD_DIM), 0)
    row_s = lax.broadcasted_iota(jnp.int32, (HG_SUB, HG_HEAD_DIM), 0)
    nt = (((1,), (1,)), ((), ()))
    tn = (((0,), (0,)), ((), ()))

    def chunk(ci, carry):
        r0 = pl.multiple_of(ci * c, c)
        q = in_ref[pl.ds(r0, c), 0:HG_HEAD_DIM]
        fp = in_ref[pl.ds(r0, c), HG_HEAD_DIM:2 * HG_HEAD_DIM]
        v = in_ref[pl.ds(r0, c), 2 * HG_HEAD_DIM:3 * HG_HEAD_DIM]
        g = in_ref[pl.ds(r0, c), 3 * HG_HEAD_DIM:4 * HG_HEAD_DIM]
        qf = jax.nn.silu(q)
        f = lb + (1.0 - lb) * jax.nn.sigmoid(fp)
        logf = jnp.log(jnp.maximum(f, TINY))
        kf = (1.0 - lb) * jax.nn.sigmoid(-fp)
        b = _cumsum_rows(logf)
        vb = v.astype(BF16)

        s_rows = [jnp.zeros((HG_SUB, c), F32)]
        for i in range(1, nsub):
            lo = i * HG_SUB
            ref_b = b[lo - 1:lo, :]
            qi = qf[lo:lo + HG_SUB, :] * jnp.exp(b[lo:lo + HG_SUB, :] - ref_b)
            ki = kf * jnp.exp(jnp.where(row_c < lo, ref_b - b, NEG_BIG))
            s_rows.append(lax.dot_general(qi.astype(BF16), ki.astype(BF16), nt, preferred_element_type=F32))
        scores = jnp.concatenate(s_rows, axis=0)
        o = jnp.dot(scores.astype(BF16), vb, preferred_element_type=F32)

        o_diag = []
        for j in range(nsub):
            lo = j * HG_SUB
            bs = b[lo:lo + HG_SUB, :]
            ks = kf[lo:lo + HG_SUB, :]
            qs = qf[lo:lo + HG_SUB, :]
            vs = v[lo:lo + HG_SUB, :]
            tiles = []
            for tt in range(HG_SUB):
                e = jnp.where(row_s <= tt, bs[tt:tt + 1, :] - bs, NEG_BIG)
                tiles.append(jnp.exp(e) * ks * qs[tt:tt + 1, :])
            w = jnp.concatenate(tiles, axis=0).astype(BF16)
            dfull = jnp.dot(w, ones_kk, preferred_element_type=F32)
            contrib = dfull.reshape(HG_SUB, HG_SUB, HG_HEAD_DIM) * vs[None, :, :]
            o_diag.append(jnp.sum(contrib, axis=1))
        o = o + jnp.concatenate(o_diag, axis=0)

        st = st_ref[...]
        qb = (qf * jnp.exp(b)).astype(BF16)
        o = o + lax.dot_general(qb, st.astype(BF16), nt, preferred_element_type=F32)
        b_last = b[c - 1:c, :]
        kn = (kf * jnp.exp(b_last - b)).astype(BF16)
        st_ref[...] = st * jnp.exp(b_last) + lax.dot_general(vb, kn, tn, preferred_element_type=F32)

        y = o * lax.rsqrt(jnp.mean(o * o, axis=-1, keepdims=True) + RMS_EPS) * gain
        o_ref[pl.ds(r0, c), :] = (y * jax.nn.silu(g)).astype(o_ref.dtype)
        return carry

    lax.fori_loop(0, in_ref.shape[0] // c, chunk, 0)


def _hgrn2(h, lb, gain, bsz, seq):
    t = bsz * seq
    rows = min(HG_ROWS, seq)
    nr = seq // rows
    hd_block0 = COL_HG // (4 * HG_HEAD_DIM)
    return pl.pallas_call(
        _hgrn_kernel,
        grid=(bsz, HG_HEADS, nr),
        in_specs=[pl.BlockSpec((rows, 4 * HG_HEAD_DIM), lambda b, hd, r: (b * nr + r, hd_block0 + hd)),
                  pl.BlockSpec((1, HG_HEAD_DIM), lambda b, hd, r: (0, hd)),
                  pl.BlockSpec((1, HG_HEAD_DIM), lambda b, hd, r: (0, hd))],
        out_specs=pl.BlockSpec((rows, HG_HEAD_DIM), lambda b, hd, r: (b * nr + r, hd)),
        out_shape=jax.ShapeDtypeStruct((t, HG_WIDTH), BF16),
        scratch_shapes=[pltpu.VMEM((HG_HEAD_DIM, HG_HEAD_DIM), F32)],
        compiler_params=_cparams(("arbitrary", "arbitrary", "arbitrary")),
        name="hgrn2",
    )(h, lb, gain)


def _lru_kernel(in_ref, cw_ref, cb_ref, wa_ref, ba_ref, wx_ref, bx_ref, lam_ref, gain_ref, o_ref,
                tail_ref, h_ref):
    r = pl.program_id(1)

    @pl.when(r == 0)
    def _():
        tail_ref[...] = jnp.zeros_like(tail_ref)
        h_ref[...] = jnp.zeros_like(h_ref)

    rows = in_ref.shape[0]
    x = in_ref[:, 0:LRU_WIDTH]
    gr = in_ref[:, LRU_WIDTH:2 * LRU_WIDTH]
    tail = tail_ref[...]
    row8 = lax.broadcasted_iota(jnp.int32, (8, LRU_WIDTH), 0)

    xc = x * cw_ref[CONV_WIDTH - 1:CONV_WIDTH, :] + cb_ref[...]
    for k in range(1, CONV_WIDTH):
        xs = pltpu.roll(x, k, axis=0)
        head = jnp.where(row8 < k, pltpu.roll(tail, k, axis=0), xs[0:8, :])
        xs = jnp.concatenate([head, xs[8:, :]], axis=0)
        xc = xc + xs * cw_ref[CONV_WIDTH - 1 - k:CONV_WIDTH - k, :]
    tail_ref[...] = x[rows - 8:rows, :]

    xcb = xc.astype(BF16)
    rg = jax.nn.sigmoid(jnp.dot(xcb, wa_ref[...], preferred_element_type=F32) + ba_ref[...])
    ig = jax.nn.sigmoid(jnp.dot(xcb, wx_ref[...], preferred_element_type=F32) + bx_ref[...])
    log_a = -LRU_C * rg * jax.nn.softplus(-lam_ref[...])
    a = jnp.exp(log_a)
    th = jnp.tanh(log_a)
    neg_expm1 = -2.0 * th / (1.0 - th)
    u = jnp.sqrt(jnp.maximum(neg_expm1, 0.0)) * (ig * xc)

    row = lax.broadcasted_iota(jnp.int32, (rows, LRU_WIDTH), 0)
    d = 1
    while d < rows:
        keep = row >= d
        a_s = jnp.where(keep, pltpu.roll(a, d, axis=0), 1.0)
        u_s = jnp.where(keep, pltpu.roll(u, d, axis=0), 0.0)
        u = a * u_s + u
        a = a * a_s
        d *= 2
    hcur = u + a * h_ref[0:1, :]
    h_ref[...] = jnp.broadcast_to(hcur[rows - 1:rows, :], h_ref.shape)

    y = hcur * lax.rsqrt(jnp.mean(hcur * hcur, axis=-1, keepdims=True) + RMS_EPS) * gain_ref[...]
    o_ref[...] = (y * jax.nn.gelu(gr)).astype(o_ref.dtype)


def _rglru(h, cw, cb, wa, ba, wx, bx, lam, gain, bsz, seq):
    t = bsz * seq
    rows = min(LRU_ROWS, seq)
    nr = seq // rows
    vec = pl.BlockSpec((1, LRU_WIDTH), lambda b, r: (0, 0))
    mat = pl.BlockSpec((LRU_WIDTH, LRU_WIDTH), lambda b, r: (0, 0))
    return pl.pallas_call(
        _lru_kernel,
        grid=(bsz, nr),
        in_specs=[pl.BlockSpec((rows, 2 * LRU_WIDTH), lambda b, r: (b * nr + r, COL_LRU // (2 * LRU_WIDTH))),
                  pl.BlockSpec((CONV_WIDTH, LRU_WIDTH), lambda b, r: (0, 0)),
                  vec, mat, vec, mat, vec, vec, vec],
        out_specs=pl.BlockSpec((rows, LRU_WIDTH), lambda b, r: (b * nr + r, 0)),
        out_shape=jax.ShapeDtypeStruct((t, LRU_WIDTH), BF16),
        scratch_shapes=[pltpu.VMEM((8, LRU_WIDTH), F32), pltpu.VMEM((8, LRU_WIDTH), F32)],
        compiler_params=_cparams(("arbitrary", "arbitrary")),
        name="rglru",
    )(h, cw, cb, wa, ba, wx, bx, lam, gain)


def _layer_norm(z, g, b):
    mu = jnp.mean(z, axis=-1, keepdims=True)
    zc = z - mu
    var = jnp.mean(zc * zc, axis=-1, keepdims=True)
    return zc * lax.rsqrt(var + LN_EPS) * g + b


def _outproj_kernel(ya_ref, yh_ref, yl_ref, x_ref, wa_ref, wh_ref, wl_ref, g_ref, b_ref, rw_ref, rb_ref,
                    x1_ref, x3_ref, idx_ref, rank_ref, gate_ref, counts_ref, carry_ref):
    i = pl.program_id(0)

    @pl.when(i == 0)
    def _():
        carry_ref[...] = jnp.zeros_like(carry_ref)

    mixed = jnp.dot(ya_ref[...], wa_ref[...], preferred_element_type=F32)
    mixed = mixed + jnp.dot(yh_ref[...], wh_ref[...], preferred_element_type=F32)
    mixed = mixed + jnp.dot(yl_ref[...], wl_ref[...], preferred_element_type=F32)
    x1 = _layer_norm(DEEPNORM_ALPHA * x_ref[...] + mixed, g_ref[...], b_ref[...])
    x1_ref[...] = x1
    _to_row_tiles(x3_ref, x1)

    tm = x1.shape[0]
    logits = lax.dot_general(rw_ref[...], x1, (((1,), (1,)), ((), ())),
                             precision=lax.Precision.HIGHEST, preferred_element_type=F32)
    scores = jax.nn.sigmoid(logits)
    sel = scores + rb_ref[...]
    shp = (N_GROUPS, EXPERTS_PER_GROUP, tm)
    s3 = sel.reshape(shp)
    sc3 = scores.reshape(shp)
    e_in = lax.broadcasted_iota(jnp.int32, shp, 1)
    g_id = lax.broadcasted_iota(jnp.int32, shp, 0)
    e_id = g_id * EXPERTS_PER_GROUP + e_in

    m1 = jnp.max(s3, axis=1, keepdims=True)
    i1 = jnp.min(jnp.where(s3 == m1, e_in, EXPERTS_PER_GROUP), axis=1, keepdims=True)
    m2 = jnp.max(jnp.where(e_in == i1, NEG_PICKED, s3), axis=1, keepdims=True)
    gs = m1 + m2
    g1 = lax.broadcasted_iota(jnp.int32, gs.shape, 0)
    gsel = jnp.zeros(gs.shape, jnp.int32)
    cur = gs
    for _ in range(TOPK_GROUPS):
        m = jnp.max(cur, axis=0, keepdims=True)
        pick = g1 == jnp.min(jnp.where(cur == m, g1, N_GROUPS), axis=0, keepdims=True)
        gsel = jnp.where(pick, 1, gsel)
        cur = jnp.where(pick, NEG_PICKED, cur)
    cur = jnp.where(gsel > 0, s3, NEG_BIG)

    def pick_sum(pick, vals):
        return jnp.sum(jnp.sum(jnp.where(pick, vals, 0.0), axis=1, keepdims=True), axis=0, keepdims=True).reshape(1, tm)

    idx_rows, w_rows, picks = [], [], []
    onehot = jnp.zeros(shp, F32)
    for _ in range(TOPK):
        m = jnp.max(jnp.max(cur, axis=1, keepdims=True), axis=0, keepdims=True)
        cand = jnp.where(cur == m, e_id, N_EXPERTS)
        ii = jnp.min(jnp.min(cand, axis=1, keepdims=True), axis=0, keepdims=True)
        pick = e_id == ii
        picks.append(pick)
        onehot = jnp.where(pick, 1.0, onehot)
        w_rows.append(pick_sum(pick, sc3))
        idx_rows.append(ii.reshape(1, tm))
        cur = jnp.where(pick, NEG_PICKED, cur)
    w = jnp.concatenate(w_rows, axis=0)
    idx = jnp.concatenate(idx_rows, axis=0)
    gates = w / jnp.sum(w, axis=0, keepdims=True) * ROUTED_SCALE

    oh = onehot.reshape(N_EXPERTS, tm).astype(BF16)
    r_i = lax.broadcasted_iota(jnp.int32, (tm, tm), 0)
    c_i = lax.broadcasted_iota(jnp.int32, (tm, tm), 1)
    before = jnp.where(r_i < c_i, 1.0, 0.0).astype(BF16)
    carry = carry_ref[...]
    prefix3 = (jnp.dot(oh, before, preferred_element_type=F32) + carry).reshape(shp)
    rank = jnp.concatenate([pick_sum(pk, prefix3) for pk in picks], axis=0).astype(jnp.int32)
    carry = carry + jnp.dot(oh, jnp.ones((tm, tm), BF16), preferred_element_type=F32)
    carry_ref[...] = carry
    counts_ref[...] = carry.astype(jnp.int32)

    for c in range(tm // IDX_TILE):
        sl = slice(c * IDX_TILE, (c + 1) * IDX_TILE)
        idx_ref[c] = idx[:, sl]
        rank_ref[c] = rank[:, sl]
        gate_ref[c] = gates[:, sl]


def _outproj_router(ya, yh, yl, x2d, w_out, ln_g, ln_b, rw_t, rb):
    t = x2d.shape[0]
    tm = min(OUT_TM, t)
    nt = tm // IDX_TILE
    rowb = lambda w: pl.BlockSpec((tm, w), lambda i: (i, 0))
    full = lambda a, b: pl.BlockSpec((a, b), lambda i: (0, 0))
    tiles = pl.BlockSpec((nt, TOPK, IDX_TILE), lambda i: (i, 0, 0))
    wa = pl.BlockSpec((ATTN_WIDTH, D_MODEL), lambda i: (0, 0))
    wh = pl.BlockSpec((HG_WIDTH, D_MODEL), lambda i: (ATTN_WIDTH // HG_WIDTH, 0))
    wl = pl.BlockSpec((LRU_WIDTH, D_MODEL), lambda i: ((ATTN_WIDTH + HG_WIDTH) // LRU_WIDTH, 0))
    tile_shape = lambda dt: jax.ShapeDtypeStruct((t // IDX_TILE, TOPK, IDX_TILE), dt)
    return pl.pallas_call(
        _outproj_kernel,
        grid=(t // tm,),
        in_specs=[rowb(ATTN_WIDTH), rowb(HG_WIDTH), rowb(LRU_WIDTH), rowb(D_MODEL), wa, wh, wl,
                  full(1, D_MODEL), full(1, D_MODEL), full(N_EXPERTS, D_MODEL), full(N_EXPERTS, 1)],
        out_specs=[rowb(D_MODEL), pl.BlockSpec((tm, ROW_TILES, LANES), lambda i: (i, 0, 0)),
                   tiles, tiles, tiles, full(N_EXPERTS, tm)],
        out_shape=[jax.ShapeDtypeStruct((t, D_MODEL), F32), jax.ShapeDtypeStruct((t, ROW_TILES, LANES), F32),
                   tile_shape(jnp.int32), tile_shape(jnp.int32), tile_shape(F32),
                   jax.ShapeDtypeStruct((N_EXPERTS, tm), jnp.int32)],
        scratch_shapes=[pltpu.VMEM((N_EXPERTS, tm), F32)],
        compiler_params=_cparams(("arbitrary",)),
        name="outproj_router",
    )(ya, yh, yl, x2d, w_out, w_out, w_out, ln_g, ln_b, rw_t, rb)


def _dispatch_kernel(pend_ref, padded_ref, dest_hbm, x3_hbm, x_ref, p_ref, sw1_ref, sw3_ref, sw2_ref, pw_ref,
                     gw_ref, gb_ref, xs_hbm, sp_ref, dest_smem, zeros_ref, sem_idx, sem_rows, sem_zero):
    i = pl.program_id(0)
    tm = x_ref.shape[0]
    bm = zeros_ref.shape[0]
    nt = tm // IDX_TILE

    @pl.when(i == 0)
    def _():
        zeros_ref[...] = jnp.zeros_like(zeros_ref)

        def fill(e, carry):
            @pl.when(padded_ref[e] > 0)
            def _():
                start = pl.multiple_of(pend_ref[e] - bm, bm)
                pltpu.make_async_copy(zeros_ref, xs_hbm.at[pl.ds(start, bm)], sem_zero).start()
            return carry

        def fill_wait(e, carry):
            @pl.when(padded_ref[e] > 0)
            def _():
                pltpu.make_async_copy(zeros_ref, xs_hbm.at[pl.ds(0, bm)], sem_zero).wait()
            return carry

        def fill_tail(blk, carry):
            start = pl.multiple_of(blk * bm, bm)
            pltpu.make_async_copy(zeros_ref, xs_hbm.at[pl.ds(start, bm)], sem_zero).start()
            return carry

        def fill_tail_wait(blk, carry):
            pltpu.make_async_copy(zeros_ref, xs_hbm.at[pl.ds(0, bm)], sem_zero).wait()
            return carry

        n_used = pend_ref[N_EXPERTS - 1] // bm
        n_blocks = xs_hbm.shape[0] // bm
        lax.fori_loop(0, N_EXPERTS, fill, 0)
        lax.fori_loop(n_used, n_blocks, fill_tail, 0)
        lax.fori_loop(0, N_EXPERTS, fill_wait, 0)
        lax.fori_loop(n_used, n_blocks, fill_tail_wait, 0)

    idx_cp = pltpu.make_async_copy(dest_hbm.at[pl.ds(i * nt, nt)], dest_smem, sem_idx)
    idx_cp.start()
    idx_cp.wait()

    for c in range(nt):
        def issue(r, carry, c=c):
            tok = i * tm + c * IDX_TILE + r
            for j in range(TOPK):
                d = dest_smem[c, j, r]
                pltpu.make_async_copy(x3_hbm.at[pl.ds(tok, 1)], xs_hbm.at[pl.ds(d, 1)], sem_rows).start()
            return carry

        lax.fori_loop(0, IDX_TILE, issue, 0)

    x = x_ref[...]
    xb = x.astype(BF16)
    h1 = jnp.dot(xb, sw1_ref[...], preferred_element_type=F32)
    h3 = jnp.dot(xb, sw3_ref[...], preferred_element_type=F32)
    shared = jnp.dot((jax.nn.silu(h1) * h3).astype(BF16), sw2_ref[...], preferred_element_type=F32)
    gate = jax.nn.sigmoid(jnp.dot(xb, gw_ref[...], preferred_element_type=F32) + gb_ref[...])
    ple = gate * jnp.dot(p_ref[...].astype(BF16), pw_ref[...], preferred_element_type=F32)
    sp_ref[...] = shared + ple

    pltpu.make_async_copy(xs_hbm.at[pl.ds(0, TOPK * tm)], xs_hbm.at[pl.ds(0, TOPK * tm)], sem_rows).wait()


def _dispatch(layer, pend, padded, dest_tiles, x3, x1, p3, sw1, sw3, sw2, pw, gw, gb, n_rows, bm):
    t = x1.shape[0]
    tm = min(DISP_TM, t)
    full = lambda a, b: pl.BlockSpec((a, b), lambda i, pe, pa: (0, 0))
    grid_spec = pltpu.PrefetchScalarGridSpec(
        num_scalar_prefetch=2,
        grid=(t // tm,),
        in_specs=[pl.BlockSpec(memory_space=pl.ANY), pl.BlockSpec(memory_space=pl.ANY),
                  pl.BlockSpec((tm, D_MODEL), lambda i, pe, pa: (i, 0)),
                  pl.BlockSpec((None, tm, PLE_DIM), lambda i, pe, pa: (layer, i, 0)),
                  full(D_MODEL, EXPERT_DIM), full(D_MODEL, EXPERT_DIM), full(EXPERT_DIM, D_MODEL),
                  full(PLE_DIM, D_MODEL), full(D_MODEL, D_MODEL), full(1, D_MODEL)],
        out_specs=[pl.BlockSpec(memory_space=pl.ANY),
                   pl.BlockSpec((tm, D_MODEL), lambda i, pe, pa: (i, 0))],
        scratch_shapes=[pltpu.SMEM((tm // IDX_TILE, TOPK, IDX_TILE), jnp.int32),
                        pltpu.VMEM((bm, ROW_TILES, LANES), F32),
                        pltpu.SemaphoreType.DMA, pltpu.SemaphoreType.DMA, pltpu.SemaphoreType.DMA],
    )
    return pl.pallas_call(
        _dispatch_kernel,
        grid_spec=grid_spec,
        out_shape=[jax.ShapeDtypeStruct((n_rows, ROW_TILES, LANES), F32),
                   jax.ShapeDtypeStruct((t, D_MODEL), F32)],
        compiler_params=_cparams(("arbitrary",)),
        name="dispatch_shared_ple",
    )(pend, padded, dest_tiles, x3, x1, p3, sw1, sw3, sw2, pw, gw, gb)


def _moe_kernel(be_ref, nused_ref, x_ref, w1_ref, w3_ref, w2_ref, y_ref, w1b, w3b, w2b):
    n = pl.program_id(0)

    @pl.when(n < nused_ref[0])
    def _():
        e = be_ref[n]
        e_prev = be_ref[jnp.maximum(n - 1, 0)]

        @pl.when((n == 0) | (e != e_prev))
        def _():
            w1b[...] = w1_ref[...].astype(BF16)
            w3b[...] = w3_ref[...].astype(BF16)
            w2b[...] = w2_ref[...].astype(BF16)

        xb = _from_row_tiles(x_ref).astype(BF16)
        h1 = jnp.dot(xb, w1b[...], preferred_element_type=F32)
        h3 = jnp.dot(xb, w3b[...], preferred_element_type=F32)
        act = (jax.nn.silu(h1) * h3).astype(BF16)
        _to_row_tiles(y_ref, jnp.dot(act, w2b[...], preferred_element_type=F32))

    @pl.when(n >= nused_ref[0])
    def _():
        y_ref[...] = jnp.zeros_like(y_ref)


def _moe(layer, block_expert, n_used, xs, w1, w3, w2, bm):
    n_blocks = xs.shape[0] // bm
    wspec = lambda a, b: pl.BlockSpec((None, None, a, b), lambda n, be, nu: (layer, be[n], 0, 0))
    grid_spec = pltpu.PrefetchScalarGridSpec(
        num_scalar_prefetch=2,
        grid=(n_blocks,),
        in_specs=[pl.BlockSpec((bm, ROW_TILES, LANES), lambda n, be, nu: (jnp.minimum(n, nu[0] - 1), 0, 0)),
                  wspec(D_MODEL, EXPERT_DIM), wspec(D_MODEL, EXPERT_DIM), wspec(EXPERT_DIM, D_MODEL)],
        out_specs=pl.BlockSpec((bm, ROW_TILES, LANES), lambda n, be, nu: (n, 0, 0)),
        scratch_shapes=[pltpu.VMEM((D_MODEL, EXPERT_DIM), BF16),
                        pltpu.VMEM((D_MODEL, EXPERT_DIM), BF16),
                        pltpu.VMEM((EXPERT_DIM, D_MODEL), BF16)],
    )
    return pl.pallas_call(
        _moe_kernel,
        grid_spec=grid_spec,
        out_shape=jax.ShapeDtypeStruct(xs.shape, F32),
        compiler_params=_cparams(("arbitrary",)),
        name="moe_experts",
    )(block_expert, n_used, xs, w1, w3, w2)


def _final_kernel(dest_hbm, gate_hbm, y_hbm, x_ref, sp_ref, g_ref, b_ref, o_ref,
                  dest_smem, gate_smem, ybuf, routed, sem_idx, sem_rows):
    i = pl.program_id(0)
    n = pl.num_programs(0)
    tm = x_ref.shape[0]
    slot = i % 2
    nslot = 1 - slot

    def idx_copies(tile, s):
        return (pltpu.make_async_copy(dest_hbm.at[tile], dest_smem.at[s], sem_idx.at[s]),
                pltpu.make_async_copy(gate_hbm.at[tile], gate_smem.at[s], sem_idx.at[s]))

    def start_idx(tile, s):
        for cp in idx_copies(tile, s):
            cp.start()

    def wait_idx(tile, s):
        for cp in idx_copies(tile, s):
            cp.wait()

    def issue_rows(s):
        def issue(r, carry):
            for j in range(TOPK):
                d = dest_smem[s, j, r]
                pltpu.make_async_copy(y_hbm.at[pl.ds(d, 1)], ybuf.at[s, j, pl.ds(r, 1)], sem_rows.at[s]).start()
            return carry

        lax.fori_loop(0, tm, issue, 0)

    @pl.when(i == 0)
    def _():
        start_idx(0, 0)
        wait_idx(0, 0)
        issue_rows(0)

        @pl.when(n > 1)
        def _():
            start_idx(1, 1)

    @pl.when(i + 1 < n)
    def _():
        wait_idx(i + 1, nslot)
        issue_rows(nslot)

    for j in range(TOPK):
        pltpu.make_async_copy(y_hbm.at[pl.ds(0, tm)], ybuf.at[slot, j], sem_rows.at[slot]).wait()

    def combine(r, carry):
        acc = gate_smem[slot, 0, r] * ybuf[slot, 0, r]
        for j in range(1, TOPK):
            acc = acc + gate_smem[slot, j, r] * ybuf[slot, j, r]
        routed[r] = acc
        return carry

    lax.fori_loop(0, tm, combine, 0)

    @pl.when(i + 2 < n)
    def _():
        start_idx(i + 2, slot)

    z = DEEPNORM_ALPHA * x_ref[...] + _from_row_tiles(routed) + sp_ref[...]
    o_ref[...] = _layer_norm(z, g_ref[...], b_ref[...])


def _final(dest_tiles, gate_tiles, y_sorted, x1, sp, ln_g, ln_b):
    t = x1.shape[0]
    tm = min(FIN_TM, t)
    full = lambda a, b: pl.BlockSpec((a, b), lambda i: (0, 0))
    rowb = pl.BlockSpec((tm, D_MODEL), lambda i: (i, 0))
    return pl.pallas_call(
        _final_kernel,
        grid=(t // tm,),
        in_specs=[pl.BlockSpec(memory_space=pl.ANY), pl.BlockSpec(memory_space=pl.ANY),
                  pl.BlockSpec(memory_space=pl.ANY), rowb, rowb, full(1, D_MODEL), full(1, D_MODEL)],
        out_specs=rowb,
        out_shape=jax.ShapeDtypeStruct((t, D_MODEL), F32),
        scratch_shapes=[pltpu.SMEM((2, TOPK, tm), jnp.int32),
                        pltpu.SMEM((2, TOPK, tm), F32),
                        pltpu.VMEM((2, TOPK, tm, ROW_TILES, LANES), F32),
                        pltpu.VMEM((tm, ROW_TILES, LANES), F32),
                        pltpu.SemaphoreType.DMA((2,)),
                        pltpu.SemaphoreType.DMA((2,))],
        compiler_params=_cparams(("arbitrary",)),
        name="combine_ln",
    )(dest_tiles, gate_tiles, y_sorted, x1, sp, ln_g, ln_b)


def _dispatch_plan(idx_tiles, rank_tiles, counts, bm, n_blocks):
    padded = ((counts + bm - 1) // bm) * bm
    pend = jnp.cumsum(padded).astype(jnp.int32)
    pstart = pend - padded
    experts = jnp.arange(N_EXPERTS, dtype=jnp.int32)
    start_of = jnp.sum(jnp.where(idx_tiles[..., None] == experts, pstart, 0), axis=-1)
    dest_tiles = (start_of + rank_tiles).astype(jnp.int32)
    block_row = jnp.arange(n_blocks, dtype=jnp.int32) * bm
    block_expert = jnp.minimum(jnp.sum((pend[None, :] <= block_row[:, None]).astype(jnp.int32), axis=1),
                               N_EXPERTS - 1)
    n_used = (pend[-1] // bm).reshape(1)
    return pend, padded.astype(jnp.int32), dest_tiles, block_expert, n_used


def _rotary_lane_tables(positions):
    inv_freq = ROPE_THETA ** (-jnp.arange(0, ROT_DIM, 2, dtype=F32) / ROT_DIM)
    ang = positions.astype(F32).reshape(-1)[:, None] * inv_freq
    cos, sin = jnp.cos(ang), jnp.sin(ang)
    half = ROT_DIM // 2
    t = ang.shape[0]
    one = jnp.ones((t, HEAD_DIM - ROT_DIM), F32)
    zero = jnp.zeros((t, HEAD_DIM - ROT_DIM), F32)
    zh = jnp.zeros((t, half), F32)
    cos64 = jnp.concatenate([cos, cos, one], axis=1)
    sa64 = jnp.concatenate([zh, sin, zero], axis=1)
    sb64 = jnp.concatenate([-sin, zh, zero], axis=1)
    tile2 = lambda m: jnp.concatenate([m, m], axis=1)
    return tile2(cos64), tile2(sa64), tile2(sb64)


def _permute_in_cols(w):
    off = [0]
    for s in IN_SIZES:
        off.append(off[-1] + s)
    aq, ak, av, hq, hf, hi, hg, lx, lg = [w[:, off[k]:off[k + 1]] for k in range(9)]
    parts = [aq]
    for hd in range(HG_HEADS):
        sl = slice(hd * HG_HEAD_DIM, (hd + 1) * HG_HEAD_DIM)
        parts += [hq[:, sl], hf[:, sl], hi[:, sl], hg[:, sl]]
    parts += [lx, lg, ak, av]
    return jnp.concatenate(parts, axis=1)


def _block_diag(w):
    hds, d, _ = w.shape
    eye = jnp.eye(hds, dtype=w.dtype)
    return (eye[:, None, :, None] * w[:, :, None, :]).reshape(hds * d, hds * d)


def kernel(x, p, positions, w_in, w_out, attn_sinks, attn_norm, hg_lb_logits, hg_norm, lru_conv_w, lru_conv_b, lru_wa, lru_ba, lru_wx, lru_bx, lru_lambda, lru_norm, ln1_g, ln1_b, router_w, router_b, exp_w1, exp_w3, exp_w2, sh_w1, sh_w3, sh_w2, ple_w, ple_gate_w, ple_gate_b, ln2_g, ln2_b):
    bsz, seq, _ = x.shape
    t = bsz * seq
    depth = w_in.shape[0]
    lb_sm = jax.nn.softmax(hg_lb_logits.astype(F32), axis=0)
    hg_lb = jnp.maximum(jnp.cumsum(lb_sm, axis=0) - lb_sm[0], 0.0)
    cos_t, sa_t, sb_t = _rotary_lane_tables(positions)
    p3 = p.reshape(depth, t, PLE_DIM)
    bm = min(MOE_BM, t)
    n_blocks = t * TOPK // bm + N_EXPERTS
    row = lambda v: v.reshape(1, -1)

    xc = x.reshape(t, D_MODEL)
    for i in range(depth):
        h = _inproj(xc, _permute_in_cols(w_in[i].astype(BF16)))
        ya = _attention(h, attn_sinks[i], cos_t, sa_t, sb_t, row(attn_norm[i]), bsz, seq)
        yh = _hgrn2(h, row(hg_lb[i]), row(hg_norm[i]), bsz, seq)
        yl = _rglru(h, lru_conv_w[i], row(lru_conv_b[i]), _block_diag(lru_wa[i]).astype(BF16), row(lru_ba[i]),
                    _block_diag(lru_wx[i]).astype(BF16), row(lru_bx[i]), row(lru_lambda[i]), row(lru_norm[i]),
                    bsz, seq)
        x1, x3, idx_tiles, rank_tiles, gate_tiles, counts = _outproj_router(
            ya, yh, yl, xc, w_out[i].astype(BF16), row(ln1_g[i]), row(ln1_b[i]),
            router_w[i].T, router_b[i].reshape(N_EXPERTS, 1))
        pend, padded, dest_tiles, block_expert, n_used = _dispatch_plan(idx_tiles, rank_tiles, counts[:, 0], bm, n_blocks)
        xs, sp = _dispatch(i, pend, padded, dest_tiles, x3, x1, p3, sh_w1[i].astype(BF16), sh_w3[i].astype(BF16),
                           sh_w2[i].astype(BF16), ple_w[i].astype(BF16), ple_gate_w[i].astype(BF16),
                           row(ple_gate_b[i]), n_blocks * bm, bm)
        y_sorted = _moe(i, block_expert, n_used, xs, exp_w1, exp_w3, exp_w2, bm)
        xc = _final(dest_tiles, gate_tiles, y_sorted, x1, sp, row(ln2_g[i]), row(ln2_b[i]))
    return xc.reshape(bsz, seq, D_MODEL)
```

```python
import jax
import jax.numpy as jnp
from jax import lax
from jax.experimental import pallas as pl
from jax.experimental.pallas import tpu as pltpu

F32 = jnp.float32
BF16 = jnp.bfloat16

D_MODEL = 2048
PLE_DIM = 256
N_HEADS = 16
KV_HEADS = 2
Q_PER_KV = N_HEADS // KV_HEADS
HEAD_DIM = 64
ATTN_WIDTH = N_HEADS * HEAD_DIM
KV_WIDTH = KV_HEADS * HEAD_DIM
ATTN_BLOCK = 128
ROT_DIM = HEAD_DIM // 4
ROPE_THETA = 500000.0
HG_HEADS = 4
HG_HEAD_DIM = 128
HG_WIDTH = HG_HEADS * HG_HEAD_DIM
HG_CHUNK = 64
HG_SUB = 8
LRU_HEADS = 8
LRU_HEAD_DIM = 64
LRU_WIDTH = LRU_HEADS * LRU_HEAD_DIM
CONV_WIDTH = 4
LRU_C = 8.0
IN_SIZES = (ATTN_WIDTH, KV_WIDTH, KV_WIDTH, HG_WIDTH, HG_WIDTH, HG_WIDTH, HG_WIDTH, LRU_WIDTH, LRU_WIDTH)
IN_COLS = sum(IN_SIZES)
MIX_WIDTH = ATTN_WIDTH + HG_WIDTH + LRU_WIDTH
N_EXPERTS = 64
TOPK = 8
N_GROUPS = 8
TOPK_GROUPS = 4
EXPERTS_PER_GROUP = N_EXPERTS // N_GROUPS
EXPERT_DIM = 512
ROUTED_SCALE = 2.5
DEPTH = 2
DEEPNORM_ALPHA = (2 * DEPTH) ** 0.25
RMS_EPS = 1e-6
LN_EPS = 1e-5
NEG_BIG = -1e30
NEG_PICKED = -3e38
TINY = 1e-30

LANES = 128
ROW_TILES = D_MODEL // LANES
VMEM_LIMIT = 56 * 1024 * 1024

COL_Q = 0
COL_HG = ATTN_WIDTH
COL_LRU = COL_HG + 4 * HG_WIDTH
COL_KV = COL_LRU + 2 * LRU_WIDTH

INPROJ_TM = 512
INPROJ_TN = IN_COLS // 2
HG_ROWS = 512
LRU_ROWS = 256
OUT_TM = 256
MOE_BM = 256
IDX_TILE = 128
DISP_TM = 256
FIN_TM = IDX_TILE


def _cparams(sem):
    return pltpu.CompilerParams(dimension_semantics=sem, vmem_limit_bytes=VMEM_LIMIT)


def _to_row_tiles(ref, val):
    for s in range(ROW_TILES):
        ref[:, s, :] = val[:, s * LANES:(s + 1) * LANES]


def _from_row_tiles(ref):
    return jnp.concatenate([ref[:, s, :] for s in range(ROW_TILES)], axis=1)


def _inproj_kernel(x_ref, w_ref, o_ref):
    o_ref[...] = jnp.dot(x_ref[...].astype(BF16), w_ref[...], preferred_element_type=F32)


def _inproj(x2d, w):
    t = x2d.shape[0]
    tm = min(INPROJ_TM, t)
    return pl.pallas_call(
        _inproj_kernel,
        grid=(IN_COLS // INPROJ_TN, t // tm),
        in_specs=[pl.BlockSpec((tm, D_MODEL), lambda j, i: (i, 0)),
                  pl.BlockSpec((D_MODEL, INPROJ_TN), lambda j, i: (0, j))],
        out_specs=pl.BlockSpec((tm, INPROJ_TN), lambda j, i: (i, j)),
        out_shape=jax.ShapeDtypeStruct((t, IN_COLS), F32),
        compiler_params=_cparams(("arbitrary", "arbitrary")),
        name="inproj",
    )(x2d, w)


def _attn_kernel(sink_ref, q_ref, kv_ref, cos_ref, sa_ref, sb_ref, gain_ref, o_ref, kprev, vprev):
    n = pl.program_id(1)

    @pl.when(n == 0)
    def _():
        kprev[...] = jnp.zeros_like(kprev)
        vprev[...] = jnp.zeros_like(vprev)

    cos = cos_ref[...]
    sa = sa_ref[...]
    sb = sb_ref[...]

    def rot(t):
        return t * cos + pltpu.roll(t, 8, axis=1) * sa + pltpu.roll(t, LANES - 8, axis=1) * sb

    kc = rot(kv_ref[:, 0:KV_WIDTH])
    vc = kv_ref[:, KV_WIDTH:2 * KV_WIDTH]
    kb = jnp.concatenate([kprev[...], kc], axis=0).astype(BF16)
    vb = jnp.concatenate([vprev[...], vc], axis=0).astype(BF16)

    row = lax.broadcasted_iota(jnp.int32, (ATTN_BLOCK, 2 * ATTN_BLOCK), 0)
    col = lax.broadcasted_iota(jnp.int32, (ATTN_BLOCK, 2 * ATTN_BLOCK), 1)
    rel = row + ATTN_BLOCK - col
    allowed = (rel >= 0) & (rel < ATTN_BLOCK) & ((col >= ATTN_BLOCK) | (n > 0))

    outs = []
    for pair in range(N_HEADS // 2):
        qt = rot(q_ref[:, pair * LANES:(pair + 1) * LANES])
        for sub in range(2):
            h = pair * 2 + sub
            kvh = h // Q_PER_KV
            qh = qt[:, sub * HEAD_DIM:(sub + 1) * HEAD_DIM].astype(BF16)
            kh = kb[:, kvh * HEAD_DIM:(kvh + 1) * HEAD_DIM]
            vh = vb[:, kvh * HEAD_DIM:(kvh + 1) * HEAD_DIM]
            s = lax.dot_general(qh, kh, (((1,), (1,)), ((), ())), preferred_element_type=F32) * (HEAD_DIM ** -0.5)
            s = jnp.where(allowed, s, NEG_BIG)
            sink = sink_ref[h]
            m = jnp.maximum(jnp.max(s, axis=-1, keepdims=True), sink)
            pr = jnp.exp(s - m)
            den = jnp.sum(pr, axis=-1, keepdims=True) + jnp.exp(sink - m)
            probs = pr / den
            outs.append(jnp.dot(probs.astype(BF16), vh, preferred_element_type=F32))
    o = jnp.concatenate(outs, axis=1)
    y = o * lax.rsqrt(jnp.mean(o * o, axis=-1, keepdims=True) + RMS_EPS) * gain_ref[...]
    o_ref[...] = y.astype(o_ref.dtype)
    kprev[...] = kc
    vprev[...] = vc


def _attention(h, sinks, cos_t, sa_t, sb_t, gain, bsz, seq):
    t = bsz * seq
    nb = seq // ATTN_BLOCK
    rowmap = lambda b, n: (b * nb + n, 0)
    return pl.pallas_call(
        _attn_kernel,
        grid=(bsz, nb),
        in_specs=[pl.BlockSpec(memory_space=pltpu.SMEM),
                  pl.BlockSpec((ATTN_BLOCK, ATTN_WIDTH), lambda b, n: (b * nb + n, COL_Q // ATTN_WIDTH)),
                  pl.BlockSpec((ATTN_BLOCK, 2 * KV_WIDTH), lambda b, n: (b * nb + n, COL_KV // (2 * KV_WIDTH))),
                  pl.BlockSpec((ATTN_BLOCK, LANES), rowmap),
                  pl.BlockSpec((ATTN_BLOCK, LANES), rowmap),
                  pl.BlockSpec((ATTN_BLOCK, LANES), rowmap),
                  pl.BlockSpec((1, ATTN_WIDTH), lambda b, n: (0, 0))],
        out_specs=pl.BlockSpec((ATTN_BLOCK, ATTN_WIDTH), rowmap),
        out_shape=jax.ShapeDtypeStruct((t, ATTN_WIDTH), BF16),
        scratch_shapes=[pltpu.VMEM((ATTN_BLOCK, KV_WIDTH), F32), pltpu.VMEM((ATTN_BLOCK, KV_WIDTH), F32)],
        compiler_params=_cparams(("arbitrary", "arbitrary")),
        name="attn",
    )(sinks, h, h, cos_t, sa_t, sb_t, gain)


def _cumsum_rows(x):
    rows = x.shape[0]
    row = lax.broadcasted_iota(jnp.int32, x.shape, 0)
    d = 1
    while d < rows:
        x = x + jnp.where(row >= d, pltpu.roll(x, d, axis=0), 0.0)
        d *= 2
    return x


def _hgrn_kernel(in_ref, lb_ref, gain_ref, o_ref, st_ref):
    r = pl.program_id(2)

    @pl.when(r == 0)
    def _():
        st_ref[...] = jnp.zeros_like(st_ref)

    c = HG_CHUNK
    nsub = c // HG_SUB
    lb = lb_ref[...]
    gain = gain_ref[...]
    ones_kk = jnp.ones((HG_HEAD_DIM, HG_HEAD_DIM), BF16)
    row_c = lax.broadcasted_iota(jnp.int32, (c, HG_HEAD_DIM), 0)
    row_s = lax.broadcasted_iota(jnp.int32, (HG_SUB, HG_HEAD_DIM), 0)
    nt = (((1,), (1,)), ((), ()))
    tn = (((0,), (0,)), ((), ()))

    def chunk(ci, carry):
        r0 = pl.multiple_of(ci * c, c)
        q = in_ref[pl.ds(r0, c), 0:HG_HEAD_DIM]
        fp = in_ref[pl.ds(r0, c), HG_HEAD_DIM:2 * HG_HEAD_DIM]
        v = in_ref[pl.ds(r0, c), 2 * HG_HEAD_DIM:3 * HG_HEAD_DIM]
        g = in_ref[pl.ds(r0, c), 3 * HG_HEAD_DIM:4 * HG_HEAD_DIM]
        qf = jax.nn.silu(q)
        f = lb + (1.0 - lb) * jax.nn.sigmoid(fp)
        logf = jnp.log(jnp.maximum(f, TINY))
        kf = (1.0 - lb) * jax.nn.sigmoid(-fp)
        b = _cumsum_rows(logf)
        vb = v.astype(BF16)

        s_rows = [jnp.zeros((HG_SUB, c), F32)]
        for i in range(1, nsub):
            lo = i * HG_SUB
            ref_b = b[lo - 1:lo, :]
            qi = qf[lo:lo + HG_SUB, :] * jnp.exp(b[lo:lo + HG_SUB, :] - ref_b)
            ki = kf * jnp.exp(jnp.where(row_c < lo, ref_b - b, NEG_BIG))
            s_rows.append(lax.dot_general(qi.astype(BF16), ki.astype(BF16), nt, preferred_element_type=F32))
        scores = jnp.concatenate(s_rows, axis=0)
        o = jnp.dot(scores.astype(BF16), vb, preferred_element_type=F32)

        o_diag = []
        for j in range(nsub):
            lo = j * HG_SUB
            bs = b[lo:lo + HG_SUB, :]
            ks = kf[lo:lo + HG_SUB, :]
            qs = qf[lo:lo + HG_SUB, :]
            vs = v[lo:lo + HG_SUB, :]
            tiles = []
            for tt in range(HG_SUB):
                e = jnp.where(row_s <= tt, bs[tt:tt + 1, :] - bs, NEG_BIG)
                tiles.append(jnp.exp(e) * ks * qs[tt:tt + 1, :])
            w = jnp.concatenate(tiles, axis=0).astype(BF16)
            dfull = jnp.dot(w, ones_kk, preferred_element_type=F32)
            contrib = dfull.reshape(HG_SUB, HG_SUB, HG_HEAD_DIM) * vs[None, :, :]
            o_diag.append(jnp.sum(contrib, axis=1))
        o = o + jnp.concatenate(o_diag, axis=0)

        st = st_ref[...]
        qb = (qf * jnp.exp(b)).astype(BF16)
        o = o + lax.dot_general(qb, st.astype(BF16), nt, preferred_element_type=F32)
        b_last = b[c - 1:c, :]
        kn = (kf * jnp.exp(b_last - b)).astype(BF16)
        st_ref[...] = st * jnp.exp(b_last) + lax.dot_general(vb, kn, tn, preferred_element_type=F32)

        y = o * lax.rsqrt(jnp.mean(o * o, axis=-1, keepdims=True) + RMS_EPS) * gain
        o_ref[pl.ds(r0, c), :] = (y * jax.nn.silu(g)).astype(o_ref.dtype)
        return carry

    lax.fori_loop(0, in_ref.shape[0] // c, chunk, 0)


def _hgrn2(h, lb, gain, bsz, seq):
    t = bsz * seq
    rows = min(HG_ROWS, seq)
    nr = seq // rows
    hd_block0 = COL_HG // (4 * HG_HEAD_DIM)
    return pl.pallas_call(
        _hgrn_kernel,
        grid=(bsz, HG_HEADS, nr),
        in_specs=[pl.BlockSpec((rows, 4 * HG_HEAD_DIM), lambda b, hd, r: (b * nr + r, hd_block0 + hd)),
                  pl.BlockSpec((1, HG_HEAD_DIM), lambda b, hd, r: (0, hd)),
                  pl.BlockSpec((1, HG_HEAD_DIM), lambda b, hd, r: (0, hd))],
        out_specs=pl.BlockSpec((rows, HG_HEAD_DIM), lambda b, hd, r: (b * nr + r, hd)),
        out_shape=jax.ShapeDtypeStruct((t, HG_WIDTH), BF16),
        scratch_shapes=[pltpu.VMEM((HG_HEAD_DIM, HG_HEAD_DIM), F32)],
        compiler_params=_cparams(("arbitrary", "arbitrary", "arbitrary")),
        name="hgrn2",
    )(h, lb, gain)


def _lru_kernel(in_ref, cw_ref, cb_ref, wa_ref, ba_ref, wx_ref, bx_ref, lam_ref, gain_ref, o_ref,
                tail_ref, h_ref):
    r = pl.program_id(1)

    @pl.when(r == 0)
    def _():
        tail_ref[...] = jnp.zeros_like(tail_ref)
        h_ref[...] = jnp.zeros_like(h_ref)

    rows = in_ref.shape[0]
    x = in_ref[:, 0:LRU_WIDTH]
    gr = in_ref[:, LRU_WIDTH:2 * LRU_WIDTH]
    tail = tail_ref[...]
    row8 = lax.broadcasted_iota(jnp.int32, (8, LRU_WIDTH), 0)

    xc = x * cw_ref[CONV_WIDTH - 1:CONV_WIDTH, :] + cb_ref[...]
    for k in range(1, CONV_WIDTH):
        xs = pltpu.roll(x, k, axis=0)
        head = jnp.where(row8 < k, pltpu.roll(tail, k, axis=0), xs[0:8, :])
        xs = jnp.concatenate([head, xs[8:, :]], axis=0)
        xc = xc + xs * cw_ref[CONV_WIDTH - 1 - k:CONV_WIDTH - k, :]
    tail_ref[...] = x[rows - 8:rows, :]

    xcb = xc.astype(BF16)
    rg = jax.nn.sigmoid(jnp.dot(xcb, wa_ref[...], preferred_element_type=F32) + ba_ref[...])
    ig = jax.nn.sigmoid(jnp.dot(xcb, wx_ref[...], preferred_element_type=F32) + bx_ref[...])
    log_a = -LRU_C * rg * jax.nn.softplus(-lam_ref[...])
    a = jnp.exp(log_a)
    th = jnp.tanh(log_a)
    neg_expm1 = -2.0 * th / (1.0 - th)
    u = jnp.sqrt(jnp.maximum(neg_expm1, 0.0)) * (ig * xc)

    row = lax.broadcasted_iota(jnp.int32, (rows, LRU_WIDTH), 0)
    d = 1
    while d < rows:
        keep = row >= d
        a_s = jnp.where(keep, pltpu.roll(a, d, axis=0), 1.0)
        u_s = jnp.where(keep, pltpu.roll(u, d, axis=0), 0.0)
        u = a * u_s + u
        a = a * a_s
        d *= 2
    hcur = u + a * h_ref[0:1, :]
    h_ref[...] = jnp.broadcast_to(hcur[rows - 1:rows, :], h_ref.shape)

    y = hcur * lax.rsqrt(jnp.mean(hcur * hcur, axis=-1, keepdims=True) + RMS_EPS) * gain_ref[...]
    o_ref[...] = (y * jax.nn.gelu(gr)).astype(o_ref.dtype)


def _rglru(h, cw, cb, wa, ba, wx, bx, lam, gain, bsz, seq):
    t = bsz * seq
    rows = min(LRU_ROWS, seq)
    nr = seq // rows
    vec = pl.BlockSpec((1, LRU_WIDTH), lambda b, r: (0, 0))
    mat = pl.BlockSpec((LRU_WIDTH, LRU_WIDTH), lambda b, r: (0, 0))
    return pl.pallas_call(
        _lru_kernel,
        grid=(bsz, nr),
        in_specs=[pl.BlockSpec((rows, 2 * LRU_WIDTH), lambda b, r: (b * nr + r, COL_LRU // (2 * LRU_WIDTH))),
                  pl.BlockSpec((CONV_WIDTH, LRU_WIDTH), lambda b, r: (0, 0)),
                  vec, mat, vec, mat, vec, vec, vec],
        out_specs=pl.BlockSpec((rows, LRU_WIDTH), lambda b, r: (b * nr + r, 0)),
        out_shape=jax.ShapeDtypeStruct((t, LRU_WIDTH), BF16),
        scratch_shapes=[pltpu.VMEM((8, LRU_WIDTH), F32), pltpu.VMEM((8, LRU_WIDTH), F32)],
        compiler_params=_cparams(("arbitrary", "arbitrary")),
        name="rglru",
    )(h, cw, cb, wa, ba, wx, bx, lam, gain)


def _layer_norm(z, g, b):
    mu = jnp.mean(z, axis=-1, keepdims=True)
    zc = z - mu
    var = jnp.mean(zc * zc, axis=-1, keepdims=True)
    return zc * lax.rsqrt(var + LN_EPS) * g + b


def _outproj_kernel(ya_ref, yh_ref, yl_ref, x_ref, wa_ref, wh_ref, wl_ref, g_ref, b_ref, rw_ref, rb_ref,
                    x1_ref, idx_ref, rank_ref, gate_ref, counts_ref, carry_ref):
    i = pl.program_id(0)

    @pl.when(i == 0)
    def _():
        carry_ref[...] = jnp.zeros_like(carry_ref)

    mixed = jnp.dot(ya_ref[...], wa_ref[...], preferred_element_type=F32)
    mixed = mixed + jnp.dot(yh_ref[...], wh_ref[...], preferred_element_type=F32)
    mixed = mixed + jnp.dot(yl_ref[...], wl_ref[...], preferred_element_type=F32)
    x1 = _layer_norm(DEEPNORM_ALPHA * x_ref[...] + mixed, g_ref[...], b_ref[...])
    x1_ref[...] = x1

    tm = x1.shape[0]
    logits = lax.dot_general(rw_ref[...], x1, (((1,), (1,)), ((), ())),
                             precision=lax.Precision.HIGHEST, preferred_element_type=F32)
    scores = jax.nn.sigmoid(logits)
    sel = scores + rb_ref[...]
    shp = (N_GROUPS, EXPERTS_PER_GROUP, tm)
    s3 = sel.reshape(shp)
    sc3 = scores.reshape(shp)
    e_in = lax.broadcasted_iota(jnp.int32, shp, 1)
    g_id = lax.broadcasted_iota(jnp.int32, shp, 0)
    e_id = g_id * EXPERTS_PER_GROUP + e_in

    m1 = jnp.max(s3, axis=1, keepdims=True)
    i1 = jnp.min(jnp.where(s3 == m1, e_in, EXPERTS_PER_GROUP), axis=1, keepdims=True)
    m2 = jnp.max(jnp.where(e_in == i1, NEG_PICKED, s3), axis=1, keepdims=True)
    gs = m1 + m2
    g1 = lax.broadcasted_iota(jnp.int32, gs.shape, 0)
    gsel = jnp.zeros(gs.shape, jnp.int32)
    cur = gs
    for _ in range(TOPK_GROUPS):
        m = jnp.max(cur, axis=0, keepdims=True)
        pick = g1 == jnp.min(jnp.where(cur == m, g1, N_GROUPS), axis=0, keepdims=True)
        gsel = jnp.where(pick, 1, gsel)
        cur = jnp.where(pick, NEG_PICKED, cur)
    cur = jnp.where(gsel > 0, s3, NEG_BIG)

    def pick_sum(pick, vals):
        return jnp.sum(jnp.sum(jnp.where(pick, vals, 0.0), axis=1, keepdims=True), axis=0, keepdims=True).reshape(1, tm)

    idx_rows, w_rows, picks = [], [], []
    onehot = jnp.zeros(shp, F32)
    for _ in range(TOPK):
        m = jnp.max(jnp.max(cur, axis=1, keepdims=True), axis=0, keepdims=True)
        cand = jnp.where(cur == m, e_id, N_EXPERTS)
        ii = jnp.min(jnp.min(cand, axis=1, keepdims=True), axis=0, keepdims=True)
        pick = e_id == ii
        picks.append(pick)
        onehot = jnp.where(pick, 1.0, onehot)
        w_rows.append(pick_sum(pick, sc3))
        idx_rows.append(ii.reshape(1, tm))
        cur = jnp.where(pick, NEG_PICKED, cur)
    w = jnp.concatenate(w_rows, axis=0)
    idx = jnp.concatenate(idx_rows, axis=0)
    gates = w / jnp.sum(w, axis=0, keepdims=True) * ROUTED_SCALE

    oh = onehot.reshape(N_EXPERTS, tm).astype(BF16)
    r_i = lax.broadcasted_iota(jnp.int32, (tm, tm), 0)
    c_i = lax.broadcasted_iota(jnp.int32, (tm, tm), 1)
    before = jnp.where(r_i < c_i, 1.0, 0.0).astype(BF16)
    carry = carry_ref[...]
    prefix3 = (jnp.dot(oh, before, preferred_element_type=F32) + carry).reshape(shp)
    rank = jnp.concatenate([pick_sum(pk, prefix3) for pk in picks], axis=0).astype(jnp.int32)
    carry = carry + jnp.dot(oh, jnp.ones((tm, tm), BF16), preferred_element_type=F32)
    carry_ref[...] = carry
    counts_ref[...] = carry.astype(jnp.int32)

    for c in range(tm // IDX_TILE):
        sl = slice(c * IDX_TILE, (c + 1) * IDX_TILE)
        idx_ref[c] = idx[:, sl]
        rank_ref[c] = rank[:, sl]
        gate_ref[c] = gates[:, sl]


def _outproj_router(ya, yh, yl, x2d, w_out, ln_g, ln_b, rw_t, rb):
    t = x2d.shape[0]
    tm = min(OUT_TM, t)
    nt = tm // IDX_TILE
    rowb = lambda w: pl.BlockSpec((tm, w), lambda i: (i, 0))
    full = lambda a, b: pl.BlockSpec((a, b), lambda i: (0, 0))
    tiles = pl.BlockSpec((nt, TOPK, IDX_TILE), lambda i: (i, 0, 0))
    wa = pl.BlockSpec((ATTN_WIDTH, D_MODEL), lambda i: (0, 0))
    wh = pl.BlockSpec((HG_WIDTH, D_MODEL), lambda i: (ATTN_WIDTH // HG_WIDTH, 0))
    wl = pl.BlockSpec((LRU_WIDTH, D_MODEL), lambda i: ((ATTN_WIDTH + HG_WIDTH) // LRU_WIDTH, 0))
    tile_shape = lambda dt: jax.ShapeDtypeStruct((t // IDX_TILE, TOPK, IDX_TILE), dt)
    return pl.pallas_call(
        _outproj_kernel,
        grid=(t // tm,),
        in_specs=[rowb(ATTN_WIDTH), rowb(HG_WIDTH), rowb(LRU_WIDTH), rowb(D_MODEL), wa, wh, wl,
                  full(1, D_MODEL), full(1, D_MODEL), full(N_EXPERTS, D_MODEL), full(N_EXPERTS, 1)],
        out_specs=[rowb(D_MODEL), tiles, tiles, tiles, full(N_EXPERTS, tm)],
        out_shape=[jax.ShapeDtypeStruct((t, D_MODEL), F32),
                   tile_shape(jnp.int32), tile_shape(jnp.int32), tile_shape(F32),
                   jax.ShapeDtypeStruct((N_EXPERTS, tm), jnp.int32)],
        scratch_shapes=[pltpu.VMEM((N_EXPERTS, tm), F32)],
        compiler_params=_cparams(("arbitrary",)),
        name="outproj_router",
    )(ya, yh, yl, x2d, w_out, w_out, w_out, ln_g, ln_b, rw_t, rb)


def _dispatch_kernel(pend_ref, padded_ref, dest_hbm, x_ref, p_ref, sw1_ref, sw3_ref, sw2_ref, pw_ref,
                     gw_ref, gb_ref, xs_hbm, sp_ref, dest_smem, zeros_ref, rows_ref, sem_idx, sem_rows, sem_zero):
    i = pl.program_id(0)
    tm = x_ref.shape[0]
    bm = zeros_ref.shape[0]
    nt = tm // IDX_TILE

    @pl.when(i == 0)
    def _():
        zeros_ref[...] = jnp.zeros_like(zeros_ref)

        def fill(e, carry):
            @pl.when(padded_ref[e] > 0)
            def _():
                start = pl.multiple_of(pend_ref[e] - bm, bm)
                pltpu.make_async_copy(zeros_ref, xs_hbm.at[pl.ds(start, bm)], sem_zero).start()
            return carry

        def fill_wait(e, carry):
            @pl.when(padded_ref[e] > 0)
            def _():
                pltpu.make_async_copy(zeros_ref, xs_hbm.at[pl.ds(0, bm)], sem_zero).wait()
            return carry

        def fill_tail(blk, carry):
            start = pl.multiple_of(blk * bm, bm)
            pltpu.make_async_copy(zeros_ref, xs_hbm.at[pl.ds(start, bm)], sem_zero).start()
            return carry

        def fill_tail_wait(blk, carry):
            pltpu.make_async_copy(zeros_ref, xs_hbm.at[pl.ds(0, bm)], sem_zero).wait()
            return carry

        n_used = pend_ref[N_EXPERTS - 1] // bm
        n_blocks = xs_hbm.shape[0] // bm
        lax.fori_loop(0, N_EXPERTS, fill, 0)
        lax.fori_loop(n_used, n_blocks, fill_tail, 0)
        lax.fori_loop(0, N_EXPERTS, fill_wait, 0)
        lax.fori_loop(n_used, n_blocks, fill_tail_wait, 0)

    idx_cp = pltpu.make_async_copy(dest_hbm.at[pl.ds(i * nt, nt)], dest_smem, sem_idx)
    idx_cp.start()
    x = x_ref[...]
    _to_row_tiles(rows_ref, x)
    idx_cp.wait()

    for c in range(nt):
        def issue(r, carry, c=c):
            row = c * IDX_TILE + r
            for j in range(TOPK):
                d = dest_smem[c, j, r]
                pltpu.make_async_copy(rows_ref.at[pl.ds(row, 1)], xs_hbm.at[pl.ds(d, 1)], sem_rows).start()
            return carry

        lax.fori_loop(0, IDX_TILE, issue, 0)

    xb = x.astype(BF16)
    h1 = jnp.dot(xb, sw1_ref[...], preferred_element_type=F32)
    h3 = jnp.dot(xb, sw3_ref[...], preferred_element_type=F32)
    shared = jnp.dot((jax.nn.silu(h1) * h3).astype(BF16), sw2_ref[...], preferred_element_type=F32)
    gate = jax.nn.sigmoid(jnp.dot(xb, gw_ref[...], preferred_element_type=F32) + gb_ref[...])
    ple = gate * jnp.dot(p_ref[...].astype(BF16), pw_ref[...], preferred_element_type=F32)
    sp_ref[...] = shared + ple

    pltpu.make_async_copy(xs_hbm.at[pl.ds(0, TOPK * tm)], xs_hbm.at[pl.ds(0, TOPK * tm)], sem_rows).wait()


def _dispatch(layer, pend, padded, dest_tiles, x1, p3, sw1, sw3, sw2, pw, gw, gb, n_rows, bm):
    t = x1.shape[0]
    tm = min(DISP_TM, t)
    full = lambda a, b: pl.BlockSpec((a, b), lambda i, pe, pa: (0, 0))
    grid_spec = pltpu.PrefetchScalarGridSpec(
        num_scalar_prefetch=2,
        grid=(t // tm,),
        in_specs=[pl.BlockSpec(memory_space=pl.ANY),
                  pl.BlockSpec((tm, D_MODEL), lambda i, pe, pa: (i, 0)),
                  pl.BlockSpec((None, tm, PLE_DIM), lambda i, pe, pa: (layer, i, 0)),
                  full(D_MODEL, EXPERT_DIM), full(D_MODEL, EXPERT_DIM), full(EXPERT_DIM, D_MODEL),
                  full(PLE_DIM, D_MODEL), full(D_MODEL, D_MODEL), full(1, D_MODEL)],
        out_specs=[pl.BlockSpec(memory_space=pl.ANY),
                   pl.BlockSpec((tm, D_MODEL), lambda i, pe, pa: (i, 0))],
        scratch_shapes=[pltpu.SMEM((tm // IDX_TILE, TOPK, IDX_TILE), jnp.int32),
                        pltpu.VMEM((bm, ROW_TILES, LANES), F32),
                        pltpu.VMEM((tm, ROW_TILES, LANES), F32),
                        pltpu.SemaphoreType.DMA, pltpu.SemaphoreType.DMA, pltpu.SemaphoreType.DMA],
    )
    return pl.pallas_call(
        _dispatch_kernel,
        grid_spec=grid_spec,
        out_shape=[jax.ShapeDtypeStruct((n_rows, ROW_TILES, LANES), F32),
                   jax.ShapeDtypeStruct((t, D_MODEL), F32)],
        compiler_params=_cparams(("arbitrary",)),
        name="dispatch_shared_ple",
    )(pend, padded, dest_tiles, x1, p3, sw1, sw3, sw2, pw, gw, gb)


def _moe_kernel(be_ref, nused_ref, x_ref, w1_ref, w3_ref, w2_ref, y_ref, w1b, w3b, w2b):
    n = pl.program_id(0)

    @pl.when(n < nused_ref[0])
    def _():
        e = be_ref[n]
        e_prev = be_ref[jnp.maximum(n - 1, 0)]

        @pl.when((n == 0) | (e != e_prev))
        def _():
            w1b[...] = w1_ref[...].astype(BF16)
            w3b[...] = w3_ref[...].astype(BF16)
            w2b[...] = w2_ref[...].astype(BF16)

        xb = _from_row_tiles(x_ref).astype(BF16)
        h1 = jnp.dot(xb, w1b[...], preferred_element_type=F32)
        h3 = jnp.dot(xb, w3b[...], preferred_element_type=F32)
        act = (jax.nn.silu(h1) * h3).astype(BF16)
        _to_row_tiles(y_ref, jnp.dot(act, w2b[...], preferred_element_type=F32))

    @pl.when(n >= nused_ref[0])
    def _():
        y_ref[...] = jnp.zeros_like(y_ref)


def _moe(layer, block_expert, n_used, xs, w1, w3, w2, bm):
    n_blocks = xs.shape[0] // bm
    wspec = lambda a, b: pl.BlockSpec((None, None, a, b), lambda n, be, nu: (layer, be[n], 0, 0))
    grid_spec = pltpu.PrefetchScalarGridSpec(
        num_scalar_prefetch=2,
        grid=(n_blocks,),
        in_specs=[pl.BlockSpec((bm, ROW_TILES, LANES), lambda n, be, nu: (jnp.minimum(n, nu[0] - 1), 0, 0)),
                  wspec(D_MODEL, EXPERT_DIM), wspec(D_MODEL, EXPERT_DIM), wspec(EXPERT_DIM, D_MODEL)],
        out_specs=pl.BlockSpec((bm, ROW_TILES, LANES), lambda n, be, nu: (n, 0, 0)),
        scratch_shapes=[pltpu.VMEM((D_MODEL, EXPERT_DIM), BF16),
                        pltpu.VMEM((D_MODEL, EXPERT_DIM), BF16),
                        pltpu.VMEM((EXPERT_DIM, D_MODEL), BF16)],
    )
    return pl.pallas_call(
        _moe_kernel,
        grid_spec=grid_spec,
        out_shape=jax.ShapeDtypeStruct(xs.shape, F32),
        compiler_params=_cparams(("arbitrary",)),
        name="moe_experts",
    )(block_expert, n_used, xs, w1, w3, w2)


def _final_kernel(dest_hbm, gate_hbm, y_hbm, x_ref, sp_ref, g_ref, b_ref, o_ref,
                  dest_smem, gate_smem, ybuf, routed, sem_idx, sem_rows):
    i = pl.program_id(0)
    n = pl.num_programs(0)
    tm = x_ref.shape[0]
    slot = i % 2
    nslot = 1 - slot

    def idx_copies(tile, s):
        return (pltpu.make_async_copy(dest_hbm.at[tile], dest_smem.at[s], sem_idx.at[s]),
                pltpu.make_async_copy(gate_hbm.at[tile], gate_smem.at[s], sem_idx.at[s]))

    def start_idx(tile, s):
        for cp in idx_copies(tile, s):
            cp.start()

    def wait_idx(tile, s):
        for cp in idx_copies(tile, s):
            cp.wait()

    def issue_rows(s):
        def issue(r, carry):
            for j in range(TOPK):
                d = dest_smem[s, j, r]
                pltpu.make_async_copy(y_hbm.at[pl.ds(d, 1)], ybuf.at[s, j, pl.ds(r, 1)], sem_rows.at[s]).start()
            return carry

        lax.fori_loop(0, tm, issue, 0)

    @pl.when(i == 0)
    def _():
        start_idx(0, 0)
        wait_idx(0, 0)
        issue_rows(0)

        @pl.when(n > 1)
        def _():
            start_idx(1, 1)

    @pl.when(i + 1 < n)
    def _():
        wait_idx(i + 1, nslot)
        issue_rows(nslot)

    for j in range(TOPK):
        pltpu.make_async_copy(y_hbm.at[pl.ds(0, tm)], ybuf.at[slot, j], sem_rows.at[slot]).wait()

    def combine(r, carry):
        acc = gate_smem[slot, 0, r] * ybuf[slot, 0, r]
        for j in range(1, TOPK):
            acc = acc + gate_smem[slot, j, r] * ybuf[slot, j, r]
        routed[r] = acc
        return carry

    lax.fori_loop(0, tm, combine, 0)

    @pl.when(i + 2 < n)
    def _():
        start_idx(i + 2, slot)

    z = DEEPNORM_ALPHA * x_ref[...] + _from_row_tiles(routed) + sp_ref[...]
    o_ref[...] = _layer_norm(z, g_ref[...], b_ref[...])


def _final(dest_tiles, gate_tiles, y_sorted, x1, sp, ln_g, ln_b):
    t = x1.shape[0]
    tm = min(FIN_TM, t)
    full = lambda a, b: pl.BlockSpec((a, b), lambda i: (0, 0))
    rowb = pl.BlockSpec((tm, D_MODEL), lambda i: (i, 0))
    return pl.pallas_call(
        _final_kernel,
        grid=(t // tm,),
        in_specs=[pl.BlockSpec(memory_space=pl.ANY), pl.BlockSpec(memory_space=pl.ANY),
                  pl.BlockSpec(memory_space=pl.ANY), rowb, rowb, full(1, D_MODEL), full(1, D_MODEL)],
        out_specs=rowb,
        out_shape=jax.ShapeDtypeStruct((t, D_MODEL), F32),
        scratch_shapes=[pltpu.SMEM((2, TOPK, tm), jnp.int32),
                        pltpu.SMEM((2, TOPK, tm), F32),
                        pltpu.VMEM((2, TOPK, tm, ROW_TILES, LANES), F32),
                        pltpu.VMEM((tm, ROW_TILES, LANES), F32),
                        pltpu.SemaphoreType.DMA((2,)),
                        pltpu.SemaphoreType.DMA((2,))],
        compiler_params=_cparams(("arbitrary",)),
        name="combine_ln",
    )(dest_tiles, gate_tiles, y_sorted, x1, sp, ln_g, ln_b)


def _dispatch_plan(idx_tiles, rank_tiles, counts, bm, n_blocks):
    padded = ((counts + bm - 1) // bm) * bm
    pend = jnp.cumsum(padded).astype(jnp.int32)
    pstart = pend - padded
    experts = jnp.arange(N_EXPERTS, dtype=jnp.int32)
    start_of = jnp.sum(jnp.where(idx_tiles[..., None] == experts, pstart, 0), axis=-1)
    dest_tiles = (start_of + rank_tiles).astype(jnp.int32)
    block_row = jnp.arange(n_blocks, dtype=jnp.int32) * bm
    block_expert = jnp.minimum(jnp.sum((pend[None, :] <= block_row[:, None]).astype(jnp.int32), axis=1),
                               N_EXPERTS - 1)
    n_used = (pend[-1] // bm).reshape(1)
    return pend, padded.astype(jnp.int32), dest_tiles, block_expert, n_used


def _rotary_lane_tables(positions):
    inv_freq = ROPE_THETA ** (-jnp.arange(0, ROT_DIM, 2, dtype=F32) / ROT_DIM)
    ang = positions.astype(F32).reshape(-1)[:, None] * inv_freq
    cos, sin = jnp.cos(ang), jnp.sin(ang)
    half = ROT_DIM // 2
    t = ang.shape[0]
    one = jnp.ones((t, HEAD_DIM - ROT_DIM), F32)
    zero = jnp.zeros((t, HEAD_DIM - ROT_DIM), F32)
    zh = jnp.zeros((t, half), F32)
    cos64 = jnp.concatenate([cos, cos, one], axis=1)
    sa64 = jnp.concatenate([zh, sin, zero], axis=1)
    sb64 = jnp.concatenate([-sin, zh, zero], axis=1)
    tile2 = lambda m: jnp.concatenate([m, m], axis=1)
    return tile2(cos64), tile2(sa64), tile2(sb64)


def _permute_in_cols(w):
    off = [0]
    for s in IN_SIZES:
        off.append(off[-1] + s)
    aq, ak, av, hq, hf, hi, hg, lx, lg = [w[:, off[k]:off[k + 1]] for k in range(9)]
    parts = [aq]
    for hd in range(HG_HEADS):
        sl = slice(hd * HG_HEAD_DIM, (hd + 1) * HG_HEAD_DIM)
        parts += [hq[:, sl], hf[:, sl], hi[:, sl], hg[:, sl]]
    parts += [lx, lg, ak, av]
    return jnp.concatenate(parts, axis=1)


def _block_diag(w):
    hds, d, _ = w.shape
    eye = jnp.eye(hds, dtype=w.dtype)
    return (eye[:, None, :, None] * w[:, :, None, :]).reshape(hds * d, hds * d)


def kernel(x, p, positions, w_in, w_out, attn_sinks, attn_norm, hg_lb_logits, hg_norm, lru_conv_w, lru_conv_b, lru_wa, lru_ba, lru_wx, lru_bx, lru_lambda, lru_norm, ln1_g, ln1_b, router_w, router_b, exp_w1, exp_w3, exp_w2, sh_w1, sh_w3, sh_w2, ple_w, ple_gate_w, ple_gate_b, ln2_g, ln2_b):
    bsz, seq, _ = x.shape
    t = bsz * seq
    depth = w_in.shape[0]
    lb_sm = jax.nn.softmax(hg_lb_logits.astype(F32), axis=0)
    hg_lb = jnp.maximum(jnp.cumsum(lb_sm, axis=0) - lb_sm[0], 0.0)
    cos_t, sa_t, sb_t = _rotary_lane_tables(positions)
    p3 = p.reshape(depth, t, PLE_DIM)
    bm = min(MOE_BM, t)
    n_blocks = t * TOPK // bm + N_EXPERTS
    row = lambda v: v.reshape(1, -1)

    xc = x.reshape(t, D_MODEL)
    for i in range(depth):
        h = _inproj(xc, _permute_in_cols(w_in[i].astype(BF16)))
        ya = _attention(h, attn_sinks[i], cos_t, sa_t, sb_t, row(attn_norm[i]), bsz, seq)
        yh = _hgrn2(h, row(hg_lb[i]), row(hg_norm[i]), bsz, seq)
        yl = _rglru(h, lru_conv_w[i], row(lru_conv_b[i]), _block_diag(lru_wa[i]).astype(BF16), row(lru_ba[i]),
                    _block_diag(lru_wx[i]).astype(BF16), row(lru_bx[i]), row(lru_lambda[i]), row(lru_norm[i]),
                    bsz, seq)
        x1, idx_tiles, rank_tiles, gate_tiles, counts = _outproj_router(
            ya, yh, yl, xc, w_out[i].astype(BF16), row(ln1_g[i]), row(ln1_b[i]),
            router_w[i].T, router_b[i].reshape(N_EXPERTS, 1))
        pend, padded, dest_tiles, block_expert, n_used = _dispatch_plan(idx_tiles, rank_tiles, counts[:, 0], bm, n_blocks)
        xs, sp = _dispatch(i, pend, padded, dest_tiles, x1, p3, sh_w1[i].astype(BF16), sh_w3[i].astype(BF16),
                           sh_w2[i].astype(BF16), ple_w[i].astype(BF16), ple_gate_w[i].astype(BF16),
                           row(ple_gate_b[i]), n_blocks * bm, bm)
        y_sorted = _moe(i, block_expert, n_used, xs, exp_w1, exp_w3, exp_w2, bm)
        xc = _final(dest_tiles, gate_tiles, y_sorted, x1, sp, row(ln2_g[i]), row(ln2_b[i]))
    return xc.reshape(bsz, seq, D_MODEL)
```

```python
import jax
import jax.numpy as jnp
from jax import lax
from jax.experimental import pallas as pl
from jax.experimental.pallas import tpu as pltpu

F32 = jnp.float32
BF16 = jnp.bfloat16

D_MODEL = 2048
PLE_DIM = 256
N_HEADS = 16
KV_HEADS = 2
Q_PER_KV = N_HEADS // KV_HEADS
HEAD_DIM = 64
ATTN_WIDTH = N_HEADS * HEAD_DIM
KV_WIDTH = KV_HEADS * HEAD_DIM
ATTN_BLOCK = 128
ROT_DIM = HEAD_DIM // 4
ROPE_THETA = 500000.0
HG_HEADS = 4
HG_HEAD_DIM = 128
HG_WIDTH = HG_HEADS * HG_HEAD_DIM
HG_CHUNK = 64
HG_SUB = 8
LRU_HEADS = 8
LRU_HEAD_DIM = 64
LRU_WIDTH = LRU_HEADS * LRU_HEAD_DIM
CONV_WIDTH = 4
LRU_C = 8.0
IN_SIZES = (ATTN_WIDTH, KV_WIDTH, KV_WIDTH, HG_WIDTH, HG_WIDTH, HG_WIDTH, HG_WIDTH, LRU_WIDTH, LRU_WIDTH)
IN_COLS = sum(IN_SIZES)
MIX_WIDTH = ATTN_WIDTH + HG_WIDTH + LRU_WIDTH
N_EXPERTS = 64
TOPK = 8
N_GROUPS = 8
TOPK_GROUPS = 4
EXPERTS_PER_GROUP = N_EXPERTS // N_GROUPS
EXPERT_DIM = 512
ROUTED_SCALE = 2.5
DEPTH = 2
DEEPNORM_ALPHA = (2 * DEPTH) ** 0.25
RMS_EPS = 1e-6
LN_EPS = 1e-5
NEG_BIG = -1e30
NEG_PICKED = -3e38
TINY = 1e-30

LANES = 128
ROW_TILES = D_MODEL // LANES
PACK_TILES = ROW_TILES // 2
VMEM_LIMIT = 56 * 1024 * 1024

COL_Q = 0
COL_HG = ATTN_WIDTH
COL_LRU = COL_HG + 4 * HG_WIDTH
COL_KV = COL_LRU + 2 * LRU_WIDTH

INPROJ_TM = 512
INPROJ_TN = IN_COLS // 2
HG_ROWS = 512
LRU_ROWS = 256
OUT_TM = 512
MOE_BM = 256
MOE_SUB = 128
IDX_TILE = 128
DISP_TM = 256
FIN_TM = IDX_TILE
FIN_BUFS = 3


def _cparams(sem):
    return pltpu.CompilerParams(dimension_semantics=sem, vmem_limit_bytes=VMEM_LIMIT)


def _to_row_tiles(ref, val, base=0):
    rows, n = val.shape[0], val.shape[1] // LANES
    for s in range(n):
        ref[pl.ds(base + s, rows, stride=n), :] = val[:, s * LANES:(s + 1) * LANES]


def _from_row_tiles(ref, n, base=0, rows=None):
    rows = ref.shape[0] // n if rows is None else rows
    return jnp.concatenate([ref[pl.ds(base + s, rows, stride=n), :] for s in range(n)], axis=1)


def _pack_bf16_pairs(x):
    c = x.shape[1] // 2
    as_bits = lambda v: lax.bitcast_convert_type(v.astype(BF16).astype(F32), jnp.uint32)
    return (as_bits(x[:, c:]) & jnp.uint32(0xFFFF0000)) | (as_bits(x[:, :c]) >> 16)


def _unpack_bf16_pairs(w):
    lo = lax.bitcast_convert_type(w << 16, F32).astype(BF16)
    hi = lax.bitcast_convert_type(w & jnp.uint32(0xFFFF0000), F32).astype(BF16)
    return jnp.concatenate([lo, hi], axis=1)


def _inproj_kernel(x_ref, w_ref, o_ref):
    o_ref[...] = jnp.dot(x_ref[...].astype(BF16), w_ref[...], preferred_element_type=F32)


def _inproj(x2d, w):
    t = x2d.shape[0]
    tm = min(INPROJ_TM, t)
    return pl.pallas_call(
        _inproj_kernel,
        grid=(IN_COLS // INPROJ_TN, t // tm),
        in_specs=[pl.BlockSpec((tm, D_MODEL), lambda j, i: (i, 0)),
                  pl.BlockSpec((D_MODEL, INPROJ_TN), lambda j, i: (0, j))],
        out_specs=pl.BlockSpec((tm, INPROJ_TN), lambda j, i: (i, j)),
        out_shape=jax.ShapeDtypeStruct((t, IN_COLS), F32),
        compiler_params=_cparams(("arbitrary", "arbitrary")),
        name="inproj",
    )(x2d, w)


def _attn_kernel(sink_ref, q_ref, kv_ref, cos_ref, sa_ref, sb_ref, gain_ref, o_ref, kprev, vprev):
    n = pl.program_id(1)

    @pl.when(n == 0)
    def _():
        kprev[...] = jnp.zeros_like(kprev)
        vprev[...] = jnp.zeros_like(vprev)

    cos = cos_ref[...]
    sa = sa_ref[...]
    sb = sb_ref[...]

    def rot(t):
        return t * cos + pltpu.roll(t, 8, axis=1) * sa + pltpu.roll(t, LANES - 8, axis=1) * sb

    kc = rot(kv_ref[:, 0:KV_WIDTH])
    vc = kv_ref[:, KV_WIDTH:2 * KV_WIDTH]
    kb = jnp.concatenate([kprev[...], kc], axis=0).astype(BF16)
    vb = jnp.concatenate([vprev[...], vc], axis=0).astype(BF16)

    row = lax.broadcasted_iota(jnp.int32, (ATTN_BLOCK, 2 * ATTN_BLOCK), 0)
    col = lax.broadcasted_iota(jnp.int32, (ATTN_BLOCK, 2 * ATTN_BLOCK), 1)
    rel = row + ATTN_BLOCK - col
    allowed = (rel >= 0) & (rel < ATTN_BLOCK) & ((col >= ATTN_BLOCK) | (n > 0))

    outs = []
    for pair in range(N_HEADS // 2):
        qt = rot(q_ref[:, pair * LANES:(pair + 1) * LANES])
        for sub in range(2):
            h = pair * 2 + sub
            kvh = h // Q_PER_KV
            qh = qt[:, sub * HEAD_DIM:(sub + 1) * HEAD_DIM].astype(BF16)
            kh = kb[:, kvh * HEAD_DIM:(kvh + 1) * HEAD_DIM]
            vh = vb[:, kvh * HEAD_DIM:(kvh + 1) * HEAD_DIM]
            s = lax.dot_general(qh, kh, (((1,), (1,)), ((), ())), preferred_element_type=F32) * (HEAD_DIM ** -0.5)
            s = jnp.where(allowed, s, NEG_BIG)
            sink = sink_ref[h]
            m = jnp.maximum(jnp.max(s, axis=-1, keepdims=True), sink)
            pr = jnp.exp(s - m)
            den = jnp.sum(pr, axis=-1, keepdims=True) + jnp.exp(sink - m)
            probs = pr / den
            outs.append(jnp.dot(probs.astype(BF16), vh, preferred_element_type=F32))
    o = jnp.concatenate(outs, axis=1)
    y = o * lax.rsqrt(jnp.mean(o * o, axis=-1, keepdims=True) + RMS_EPS) * gain_ref[...]
    o_ref[...] = y.astype(o_ref.dtype)
    kprev[...] = kc
    vprev[...] = vc


def _attention(h, sinks, cos_t, sa_t, sb_t, gain, bsz, seq):
    t = bsz * seq
    nb = seq // ATTN_BLOCK
    rowmap = lambda b, n: (b * nb + n, 0)
    return pl.pallas_call(
        _attn_kernel,
        grid=(bsz, nb),
        in_specs=[pl.BlockSpec(memory_space=pltpu.SMEM),
                  pl.BlockSpec((ATTN_BLOCK, ATTN_WIDTH), lambda b, n: (b * nb + n, COL_Q // ATTN_WIDTH)),
                  pl.BlockSpec((ATTN_BLOCK, 2 * KV_WIDTH), lambda b, n: (b * nb + n, COL_KV // (2 * KV_WIDTH))),
                  pl.BlockSpec((ATTN_BLOCK, LANES), rowmap),
                  pl.BlockSpec((ATTN_BLOCK, LANES), rowmap),
                  pl.BlockSpec((ATTN_BLOCK, LANES), rowmap),
                  pl.BlockSpec((1, ATTN_WIDTH), lambda b, n: (0, 0))],
        out_specs=pl.BlockSpec((ATTN_BLOCK, ATTN_WIDTH), rowmap),
        out_shape=jax.ShapeDtypeStruct((t, ATTN_WIDTH), BF16),
        scratch_shapes=[pltpu.VMEM((ATTN_BLOCK, KV_WIDTH), F32), pltpu.VMEM((ATTN_BLOCK, KV_WIDTH), F32)],
        compiler_params=_cparams(("arbitrary", "arbitrary")),
        name="attn",
    )(sinks, h, h, cos_t, sa_t, sb_t, gain)


def _cumsum_rows(x):
    rows = x.shape[0]
    row = lax.broadcasted_iota(jnp.int32, x.shape, 0)
    d = 1
    while d < rows:
        x = x + jnp.where(row >= d, pltpu.roll(x, d, axis=0), 0.0)
        d *= 2
    return x


def _hgrn_kernel(in_ref, lb_ref, gain_ref, o_ref, st_ref):
    r = pl.program_id(2)

    @pl.when(r == 0)
    def _():
        st_ref[...] = jnp.zeros_like(st_ref)

    c = HG_CHUNK
    nsub = c // HG_SUB
    lb = lb_ref[...]
    gain = gain_ref[...]
    ones_kk = jnp.ones((HG_HEAD_DIM, HG_HEAD_DIM), BF16)
    row_c = lax.broadcasted_iota(jnp.int32, (c, HG_HEAD_DIM), 0)
    row_s = lax.broadcasted_iota(jnp.int32, (HG_SUB, HG_HEAD_DIM), 0)
    nt = (((1,), (1,)), ((), ()))
    tn = (((0,), (0,)), ((), ()))

    def chunk(ci, carry):
        r0 = pl.multiple_of(ci * c, c)
        q = in_ref[pl.ds(r0, c), 0:HG_HEAD_DIM]
        fp = in_ref[pl.ds(r0, c), HG_HEAD_DIM:2 * HG_HEAD_DIM]
        v = in_ref[pl.ds(r0, c), 2 * HG_HEAD_DIM:3 * HG_HEAD_DIM]
        g = in_ref[pl.ds(r0, c), 3 * HG_HEAD_DIM:4 * HG_HEAD_DIM]
        qf = jax.nn.silu(q)
        f = lb + (1.0 - lb) * jax.nn.sigmoid(fp)
        logf = jnp.log(jnp.maximum(f, TINY))
        kf = (1.0 - lb) * jax.nn.sigmoid(-fp)
        b = _cumsum_rows(logf)
        vb = v.astype(BF16)

        s_rows = [jnp.zeros((HG_SUB, c), F32)]
        for i in range(1, nsub):
            lo = i * HG_SUB
            ref_b = b[lo - 1:lo, :]
            qi = qf[lo:lo + HG_SUB, :] * jnp.exp(b[lo:lo + HG_SUB, :] - ref_b)
            ki = kf * jnp.exp(jnp.where(row_c < lo, ref_b - b, NEG_BIG))
            s_rows.append(lax.dot_general(qi.astype(BF16), ki.astype(BF16), nt, preferred_element_type=F32))
        scores = jnp.concatenate(s_rows, axis=0)
        o = jnp.dot(scores.astype(BF16), vb, preferred_element_type=F32)

        o_diag = []
        for j in range(nsub):
            lo = j * HG_SUB
            bs = b[lo:lo + HG_SUB, :]
            ks = kf[lo:lo + HG_SUB, :]
            qs = qf[lo:lo + HG_SUB, :]
            vs = v[lo:lo + HG_SUB, :]
            tiles = []
            for tt in range(HG_SUB):
                e = jnp.where(row_s <= tt, bs[tt:tt + 1, :] - bs, NEG_BIG)
                tiles.append(jnp.exp(e) * ks * qs[tt:tt + 1, :])
            w = jnp.concatenate(tiles, axis=0).astype(BF16)
            dfull = jnp.dot(w, ones_kk, preferred_element_type=F32)
            contrib = dfull.reshape(HG_SUB, HG_SUB, HG_HEAD_DIM) * vs[None, :, :]
            o_diag.append(jnp.sum(contrib, axis=1))
        o = o + jnp.concatenate(o_diag, axis=0)

        st = st_ref[...]
        qb = (qf * jnp.exp(b)).astype(BF16)
        o = o + lax.dot_general(qb, st.astype(BF16), nt, preferred_element_type=F32)
        b_last = b[c - 1:c, :]
        kn = (kf * jnp.exp(b_last - b)).astype(BF16)
        st_ref[...] = st * jnp.exp(b_last) + lax.dot_general(vb, kn, tn, preferred_element_type=F32)

        y = o * lax.rsqrt(jnp.mean(o * o, axis=-1, keepdims=True) + RMS_EPS) * gain
        o_ref[pl.ds(r0, c), :] = (y * jax.nn.silu(g)).astype(o_ref.dtype)
        return carry

    lax.fori_loop(0, in_ref.shape[0] // c, chunk, 0)


def _hgrn2(h, lb, gain, bsz, seq):
    t = bsz * seq
    rows = min(HG_ROWS, seq)
    nr = seq // rows
    hd_block0 = COL_HG // (4 * HG_HEAD_DIM)
    return pl.pallas_call(
        _hgrn_kernel,
        grid=(bsz, HG_HEADS, nr),
        in_specs=[pl.BlockSpec((rows, 4 * HG_HEAD_DIM), lambda b, hd, r: (b * nr + r, hd_block0 + hd)),
                  pl.BlockSpec((1, HG_HEAD_DIM), lambda b, hd, r: (0, hd)),
                  pl.BlockSpec((1, HG_HEAD_DIM), lambda b, hd, r: (0, hd))],
        out_specs=pl.BlockSpec((rows, HG_HEAD_DIM), lambda b, hd, r: (b * nr + r, hd)),
        out_shape=jax.ShapeDtypeStruct((t, HG_WIDTH), BF16),
        scratch_shapes=[pltpu.VMEM((HG_HEAD_DIM, HG_HEAD_DIM), F32)],
        compiler_params=_cparams(("arbitrary", "arbitrary", "arbitrary")),
        name="hgrn2",
    )(h, lb, gain)


def _lru_kernel(in_ref, cw_ref, cb_ref, wa_ref, ba_ref, wx_ref, bx_ref, lam_ref, gain_ref, o_ref,
                tail_ref, h_ref):
    r = pl.program_id(1)

    @pl.when(r == 0)
    def _():
        tail_ref[...] = jnp.zeros_like(tail_ref)
        h_ref[...] = jnp.zeros_like(h_ref)

    rows = in_ref.shape[0]
    x = in_ref[:, 0:LRU_WIDTH]
    gr = in_ref[:, LRU_WIDTH:2 * LRU_WIDTH]
    tail = tail_ref[...]
    row8 = lax.broadcasted_iota(jnp.int32, (8, LRU_WIDTH), 0)

    xc = x * cw_ref[CONV_WIDTH - 1:CONV_WIDTH, :] + cb_ref[...]
    for k in range(1, CONV_WIDTH):
        xs = pltpu.roll(x, k, axis=0)
        head = jnp.where(row8 < k, pltpu.roll(tail, k, axis=0), xs[0:8, :])
        xs = jnp.concatenate([head, xs[8:, :]], axis=0)
        xc = xc + xs * cw_ref[CONV_WIDTH - 1 - k:CONV_WIDTH - k, :]
    tail_ref[...] = x[rows - 8:rows, :]

    xcb = xc.astype(BF16)
    rg = jax.nn.sigmoid(jnp.dot(xcb, wa_ref[...], preferred_element_type=F32) + ba_ref[...])
    ig = jax.nn.sigmoid(jnp.dot(xcb, wx_ref[...], preferred_element_type=F32) + bx_ref[...])
    log_a = -LRU_C * rg * jax.nn.softplus(-lam_ref[...])
    a = jnp.exp(log_a)
    th = jnp.tanh(log_a)
    neg_expm1 = -2.0 * th / (1.0 - th)
    u = jnp.sqrt(jnp.maximum(neg_expm1, 0.0)) * (ig * xc)

    row = lax.broadcasted_iota(jnp.int32, (rows, LRU_WIDTH), 0)
    d = 1
    while d < rows:
        keep = row >= d
        a_s = jnp.where(keep, pltpu.roll(a, d, axis=0), 1.0)
        u_s = jnp.where(keep, pltpu.roll(u, d, axis=0), 0.0)
        u = a * u_s + u
        a = a * a_s
        d *= 2
    hcur = u + a * h_ref[0:1, :]
    h_ref[...] = jnp.broadcast_to(hcur[rows - 1:rows, :], h_ref.shape)

    y = hcur * lax.rsqrt(jnp.mean(hcur * hcur, axis=-1, keepdims=True) + RMS_EPS) * gain_ref[...]
    o_ref[...] = (y * jax.nn.gelu(gr)).astype(o_ref.dtype)


def _rglru(h, cw, cb, wa, ba, wx, bx, lam, gain, bsz, seq):
    t = bsz * seq
    rows = min(LRU_ROWS, seq)
    nr = seq // rows
    vec = pl.BlockSpec((1, LRU_WIDTH), lambda b, r: (0, 0))
    mat = pl.BlockSpec((LRU_WIDTH, LRU_WIDTH), lambda b, r: (0, 0))
    return pl.pallas_call(
        _lru_kernel,
        grid=(bsz, nr),
        in_specs=[pl.BlockSpec((rows, 2 * LRU_WIDTH), lambda b, r: (b * nr + r, COL_LRU // (2 * LRU_WIDTH))),
                  pl.BlockSpec((CONV_WIDTH, LRU_WIDTH), lambda b, r: (0, 0)),
                  vec, mat, vec, mat, vec, vec, vec],
        out_specs=pl.BlockSpec((rows, LRU_WIDTH), lambda b, r: (b * nr + r, 0)),
        out_shape=jax.ShapeDtypeStruct((t, LRU_WIDTH), BF16),
        scratch_shapes=[pltpu.VMEM((8, LRU_WIDTH), F32), pltpu.VMEM((8, LRU_WIDTH), F32)],
        compiler_params=_cparams(("arbitrary", "arbitrary")),
        name="rglru",
    )(h, cw, cb, wa, ba, wx, bx, lam, gain)


def _layer_norm(z, g, b):
    mu = jnp.mean(z, axis=-1, keepdims=True)
    zc = z - mu
    var = jnp.mean(zc * zc, axis=-1, keepdims=True)
    return zc * lax.rsqrt(var + LN_EPS) * g + b


def _outproj_kernel(ya_ref, yh_ref, yl_ref, x_ref, wa_ref, wh_ref, wl_ref, g_ref, b_ref, rw_ref, rb_ref,
                    x1_ref, idx_ref, rank_ref, gate_ref, counts_ref, carry_ref):
    i = pl.program_id(0)

    @pl.when(i == 0)
    def _():
        carry_ref[...] = jnp.zeros_like(carry_ref)

    mixed = jnp.dot(ya_ref[...], wa_ref[...], preferred_element_type=F32)
    mixed = mixed + jnp.dot(yh_ref[...], wh_ref[...], preferred_element_type=F32)
    mixed = mixed + jnp.dot(yl_ref[...], wl_ref[...], preferred_element_type=F32)
    x1 = _layer_norm(DEEPNORM_ALPHA * x_ref[...] + mixed, g_ref[...], b_ref[...])
    x1_ref[...] = x1

    tm = x1.shape[0]
    logits = lax.dot_general(rw_ref[...], x1, (((1,), (1,)), ((), ())),
                             precision=lax.Precision.HIGHEST, preferred_element_type=F32)
    scores = jax.nn.sigmoid(logits)
    sel = scores + rb_ref[...]
    shp = (N_GROUPS, EXPERTS_PER_GROUP, tm)
    s3 = sel.reshape(shp)
    sc3 = scores.reshape(shp)
    e_in = lax.broadcasted_iota(jnp.int32, shp, 1)
    g_id = lax.broadcasted_iota(jnp.int32, shp, 0)
    e_id = g_id * EXPERTS_PER_GROUP + e_in

    m1 = jnp.max(s3, axis=1, keepdims=True)
    i1 = jnp.min(jnp.where(s3 == m1, e_in, EXPERTS_PER_GROUP), axis=1, keepdims=True)
    m2 = jnp.max(jnp.where(e_in == i1, NEG_PICKED, s3), axis=1, keepdims=True)
    gs = m1 + m2
    g1 = lax.broadcasted_iota(jnp.int32, gs.shape, 0)
    gsel = jnp.zeros(gs.shape, jnp.int32)
    cur = gs
    for _ in range(TOPK_GROUPS):
        m = jnp.max(cur, axis=0, keepdims=True)
        pick = g1 == jnp.min(jnp.where(cur == m, g1, N_GROUPS), axis=0, keepdims=True)
        gsel = jnp.where(pick, 1, gsel)
        cur = jnp.where(pick, NEG_PICKED, cur)
    cur = jnp.where(gsel > 0, s3, NEG_BIG)

    def pick_sum(pick, vals):
        return jnp.sum(jnp.sum(jnp.where(pick, vals, 0.0), axis=1, keepdims=True), axis=0, keepdims=True).reshape(1, tm)

    idx_rows, w_rows, picks = [], [], []
    onehot = jnp.zeros(shp, F32)
    for _ in range(TOPK):
        m = jnp.max(jnp.max(cur, axis=1, keepdims=True), axis=0, keepdims=True)
        cand = jnp.where(cur == m, e_id, N_EXPERTS)
        ii = jnp.min(jnp.min(cand, axis=1, keepdims=True), axis=0, keepdims=True)
        pick = e_id == ii
        picks.append(pick)
        onehot = jnp.where(pick, 1.0, onehot)
        w_rows.append(pick_sum(pick, sc3))
        idx_rows.append(ii.reshape(1, tm))
        cur = jnp.where(pick, NEG_PICKED, cur)
    w = jnp.concatenate(w_rows, axis=0)
    idx = jnp.concatenate(idx_rows, axis=0)
    gates = w / jnp.sum(w, axis=0, keepdims=True) * ROUTED_SCALE

    oh = onehot.reshape(N_EXPERTS, tm).astype(BF16)
    r_i = lax.broadcasted_iota(jnp.int32, (tm, tm), 0)
    c_i = lax.broadcasted_iota(jnp.int32, (tm, tm), 1)
    before = jnp.where(r_i < c_i, 1.0, 0.0).astype(BF16)
    carry = carry_ref[...]
    prefix3 = (jnp.dot(oh, before, preferred_element_type=F32) + carry).reshape(shp)
    rank = jnp.concatenate([pick_sum(pk, prefix3) for pk in picks], axis=0).astype(jnp.int32)
    carry = carry + jnp.dot(oh, jnp.ones((tm, tm), BF16), preferred_element_type=F32)
    carry_ref[...] = carry
    counts_ref[...] = carry.astype(jnp.int32)

    for c in range(tm // IDX_TILE):
        sl = slice(c * IDX_TILE, (c + 1) * IDX_TILE)
        idx_ref[c] = idx[:, sl]
        rank_ref[c] = rank[:, sl]
        gate_ref[c] = gates[:, sl]


def _outproj_router(ya, yh, yl, x2d, w_out, ln_g, ln_b, rw_t, rb):
    t = x2d.shape[0]
    tm = min(OUT_TM, t)
    nt = tm // IDX_TILE
    rowb = lambda w: pl.BlockSpec((tm, w), lambda i: (i, 0))
    full = lambda a, b: pl.BlockSpec((a, b), lambda i: (0, 0))
    tiles = pl.BlockSpec((nt, TOPK, IDX_TILE), lambda i: (i, 0, 0))
    wa = pl.BlockSpec((ATTN_WIDTH, D_MODEL), lambda i: (0, 0))
    wh = pl.BlockSpec((HG_WIDTH, D_MODEL), lambda i: (ATTN_WIDTH // HG_WIDTH, 0))
    wl = pl.BlockSpec((LRU_WIDTH, D_MODEL), lambda i: ((ATTN_WIDTH + HG_WIDTH) // LRU_WIDTH, 0))
    tile_shape = lambda dt: jax.ShapeDtypeStruct((t // IDX_TILE, TOPK, IDX_TILE), dt)
    return pl.pallas_call(
        _outproj_kernel,
        grid=(t // tm,),
        in_specs=[rowb(ATTN_WIDTH), rowb(HG_WIDTH), rowb(LRU_WIDTH), rowb(D_MODEL), wa, wh, wl,
                  full(1, D_MODEL), full(1, D_MODEL), full(N_EXPERTS, D_MODEL), full(N_EXPERTS, 1)],
        out_specs=[rowb(D_MODEL), tiles, tiles, tiles, full(N_EXPERTS, tm)],
        out_shape=[jax.ShapeDtypeStruct((t, D_MODEL), F32),
                   tile_shape(jnp.int32), tile_shape(jnp.int32), tile_shape(F32),
                   jax.ShapeDtypeStruct((N_EXPERTS, tm), jnp.int32)],
        scratch_shapes=[pltpu.VMEM((N_EXPERTS, tm), F32)],
        compiler_params=_cparams(("arbitrary",)),
        name="outproj_router",
    )(ya, yh, yl, x2d, w_out, w_out, w_out, ln_g, ln_b, rw_t, rb)


def _dispatch_kernel(pend_ref, padded_ref, dest_hbm, x_ref, p_ref, sw1_ref, sw3_ref, sw2_ref, pw_ref,
                     gw_ref, gb_ref, xs_hbm, sp_ref, dest_smem, zeros_ref, rows_ref, sem_idx, sem_rows, sem_zero):
    i = pl.program_id(0)
    tm = x_ref.shape[0]
    bm = zeros_ref.shape[0] // PACK_TILES
    nt = tm // IDX_TILE

    @pl.when(i == 0)
    def _():
        zeros_ref[...] = jnp.zeros_like(zeros_ref)

        def fill(e, carry):
            @pl.when(padded_ref[e] > 0)
            def _():
                start = pl.multiple_of(pend_ref[e] - bm, bm)
                pltpu.make_async_copy(zeros_ref, xs_hbm.at[pl.ds(start * PACK_TILES, bm * PACK_TILES)], sem_zero).start()
            return carry

        def fill_wait(e, carry):
            @pl.when(padded_ref[e] > 0)
            def _():
                pltpu.make_async_copy(zeros_ref, xs_hbm.at[pl.ds(0, bm * PACK_TILES)], sem_zero).wait()
            return carry

        def fill_tail(blk, carry):
            start = pl.multiple_of(blk * bm, bm)
            pltpu.make_async_copy(zeros_ref, xs_hbm.at[pl.ds(start * PACK_TILES, bm * PACK_TILES)], sem_zero).start()
            return carry

        def fill_tail_wait(blk, carry):
            pltpu.make_async_copy(zeros_ref, xs_hbm.at[pl.ds(0, bm * PACK_TILES)], sem_zero).wait()
            return carry

        n_used = pend_ref[N_EXPERTS - 1] // bm
        n_blocks = xs_hbm.shape[0] // (bm * PACK_TILES)
        lax.fori_loop(0, N_EXPERTS, fill, 0)
        lax.fori_loop(n_used, n_blocks, fill_tail, 0)
        lax.fori_loop(0, N_EXPERTS, fill_wait, 0)
        lax.fori_loop(n_used, n_blocks, fill_tail_wait, 0)

    idx_cp = pltpu.make_async_copy(dest_hbm.at[pl.ds(i * nt, nt)], dest_smem, sem_idx)
    idx_cp.start()
    x = x_ref[...]
    _to_row_tiles(rows_ref, _pack_bf16_pairs(x))
    idx_cp.wait()

    for c in range(nt):
        for r in range(IDX_TILE):
            row = c * IDX_TILE + r
            for j in range(TOPK):
                d = dest_smem[c, j, r]
                pltpu.make_async_copy(rows_ref.at[pl.ds(row * PACK_TILES, PACK_TILES)],
                                      xs_hbm.at[pl.ds(d * PACK_TILES, PACK_TILES)], sem_rows).start()

    xb = x.astype(BF16)
    h1 = jnp.dot(xb, sw1_ref[...], preferred_element_type=F32)
    h3 = jnp.dot(xb, sw3_ref[...], preferred_element_type=F32)
    shared = jnp.dot((jax.nn.silu(h1) * h3).astype(BF16), sw2_ref[...], preferred_element_type=F32)
    gate = jax.nn.sigmoid(jnp.dot(xb, gw_ref[...], preferred_element_type=F32) + gb_ref[...])
    ple = gate * jnp.dot(p_ref[...].astype(BF16), pw_ref[...], preferred_element_type=F32)
    sp_ref[...] = shared + ple

    n_copied = TOPK * tm * PACK_TILES
    pltpu.make_async_copy(xs_hbm.at[pl.ds(0, n_copied)], xs_hbm.at[pl.ds(0, n_copied)], sem_rows).wait()


def _dispatch(layer, pend, padded, dest_tiles, x1, p3, sw1, sw3, sw2, pw, gw, gb, n_rows, bm):
    t = x1.shape[0]
    tm = min(DISP_TM, t)
    full = lambda a, b: pl.BlockSpec((a, b), lambda i, pe, pa: (0, 0))
    grid_spec = pltpu.PrefetchScalarGridSpec(
        num_scalar_prefetch=2,
        grid=(t // tm,),
        in_specs=[pl.BlockSpec(memory_space=pl.ANY),
                  pl.BlockSpec((tm, D_MODEL), lambda i, pe, pa: (i, 0)),
                  pl.BlockSpec((None, tm, PLE_DIM), lambda i, pe, pa: (layer, i, 0)),
                  full(D_MODEL, EXPERT_DIM), full(D_MODEL, EXPERT_DIM), full(EXPERT_DIM, D_MODEL),
                  full(PLE_DIM, D_MODEL), full(D_MODEL, D_MODEL), full(1, D_MODEL)],
        out_specs=[pl.BlockSpec(memory_space=pl.ANY),
                   pl.BlockSpec((tm, D_MODEL), lambda i, pe, pa: (i, 0))],
        scratch_shapes=[pltpu.SMEM((tm // IDX_TILE, TOPK, IDX_TILE), jnp.int32),
                        pltpu.VMEM((bm * PACK_TILES, LANES), jnp.uint32),
                        pltpu.VMEM((tm * PACK_TILES, LANES), jnp.uint32),
                        pltpu.SemaphoreType.DMA, pltpu.SemaphoreType.DMA, pltpu.SemaphoreType.DMA],
    )
    return pl.pallas_call(
        _dispatch_kernel,
        grid_spec=grid_spec,
        out_shape=[jax.ShapeDtypeStruct((n_rows * PACK_TILES, LANES), jnp.uint32),
                   jax.ShapeDtypeStruct((t, D_MODEL), F32)],
        compiler_params=_cparams(("arbitrary",)),
        name="dispatch_shared_ple",
    )(pend, padded, dest_tiles, x1, p3, sw1, sw3, sw2, pw, gw, gb)


def _moe_kernel(be_ref, nused_ref, x_ref, w1_ref, w3_ref, w2_ref, y_ref, w1b, w3b, w2b):
    n = pl.program_id(0)

    @pl.when(n < nused_ref[0])
    def _():
        e = be_ref[n]
        e_prev = be_ref[jnp.maximum(n - 1, 0)]

        @pl.when((n == 0) | (e != e_prev))
        def _():
            w1b[...] = w1_ref[...].astype(BF16)
            w3b[...] = w3_ref[...].astype(BF16)
            w2b[...] = w2_ref[...].astype(BF16)

        sub = min(MOE_SUB, x_ref.shape[0] // PACK_TILES)
        for c in range(x_ref.shape[0] // (PACK_TILES * sub)):
            xb = _unpack_bf16_pairs(_from_row_tiles(x_ref, PACK_TILES, base=c * sub * PACK_TILES, rows=sub))
            h1 = jnp.dot(xb, w1b[...], preferred_element_type=F32)
            h3 = jnp.dot(xb, w3b[...], preferred_element_type=F32)
            act = (jax.nn.silu(h1) * h3).astype(BF16)
            _to_row_tiles(y_ref, jnp.dot(act, w2b[...], preferred_element_type=F32), base=c * sub * ROW_TILES)

    @pl.when(n >= nused_ref[0])
    def _():
        y_ref[...] = jnp.zeros_like(y_ref)


def _moe(layer, block_expert, n_used, xs, w1, w3, w2, bm):
    n_blocks = xs.shape[0] // (bm * PACK_TILES)
    wspec = lambda a, b: pl.BlockSpec((None, None, a, b), lambda n, be, nu: (layer, be[n], 0, 0))
    grid_spec = pltpu.PrefetchScalarGridSpec(
        num_scalar_prefetch=2,
        grid=(n_blocks,),
        in_specs=[pl.BlockSpec((bm * PACK_TILES, LANES), lambda n, be, nu: (jnp.minimum(n, nu[0] - 1), 0)),
                  wspec(D_MODEL, EXPERT_DIM), wspec(D_MODEL, EXPERT_DIM), wspec(EXPERT_DIM, D_MODEL)],
        out_specs=pl.BlockSpec((bm * ROW_TILES, LANES), lambda n, be, nu: (n, 0)),
        scratch_shapes=[pltpu.VMEM((D_MODEL, EXPERT_DIM), BF16),
                        pltpu.VMEM((D_MODEL, EXPERT_DIM), BF16),
                        pltpu.VMEM((EXPERT_DIM, D_MODEL), BF16)],
    )
    return pl.pallas_call(
        _moe_kernel,
        grid_spec=grid_spec,
        out_shape=jax.ShapeDtypeStruct((n_blocks * bm * ROW_TILES, LANES), F32),
        compiler_params=_cparams(("arbitrary",)),
        name="moe_experts",
    )(block_expert, n_used, xs, w1, w3, w2)


def _final_kernel(dest_hbm, gate_hbm, y_hbm, x_ref, sp_ref, g_ref, b_ref, o_ref,
                  dest_smem, gate_smem, ybuf, routed, sem_idx, sem_rows):
    i = pl.program_id(0)
    n = pl.num_programs(0)
    tm = x_ref.shape[0]
    nbuf = ybuf.shape[0]
    slot = i % nbuf
    ahead = nbuf - 1

    def idx_copies(tile, s):
        return (pltpu.make_async_copy(dest_hbm.at[tile], dest_smem.at[s], sem_idx.at[s]),
                pltpu.make_async_copy(gate_hbm.at[tile], gate_smem.at[s], sem_idx.at[s]))

    def start_idx(tile, s):
        for cp in idx_copies(tile, s):
            cp.start()

    def wait_idx(tile, s):
        for cp in idx_copies(tile, s):
            cp.wait()

    def issue_row(s, r):
        for j in range(TOPK):
            d = dest_smem[s, j, r]
            pltpu.make_async_copy(y_hbm.at[pl.ds(d * ROW_TILES, ROW_TILES)],
                                  ybuf.at[s, j, pl.ds(r * ROW_TILES, ROW_TILES)], sem_rows.at[s]).start()

    def issue_rows_loop(s):
        def issue(r, carry):
            issue_row(s, r)
            return carry

        lax.fori_loop(0, tm, issue, 0)

    @pl.when(i == 0)
    def _():
        for k in range(ahead):
            @pl.when(k < n)
            def _(k=k):
                start_idx(k, k)
                wait_idx(k, k)
                issue_rows_loop(k)

        @pl.when(ahead < n)
        def _():
            start_idx(ahead, ahead)

    for j in range(TOPK):
        pltpu.make_async_copy(y_hbm.at[pl.ds(0, tm * ROW_TILES)], ybuf.at[slot, j], sem_rows.at[slot]).wait()

    def combine(r, carry):
        rows = pl.ds(pl.multiple_of(r * ROW_TILES, ROW_TILES), ROW_TILES)
        acc = gate_smem[slot, 0, r] * ybuf[slot, 0, rows, :]
        for j in range(1, TOPK):
            acc = acc + gate_smem[slot, j, r] * ybuf[slot, j, rows, :]
        routed[rows, :] = acc
        return carry

    lax.fori_loop(0, tm, combine, 0)

    @pl.when(i + nbuf < n)
    def _():
        start_idx(i + nbuf, slot)

    def finish():
        z = DEEPNORM_ALPHA * x_ref[...] + _from_row_tiles(routed, ROW_TILES) + sp_ref[...]
        o_ref[...] = _layer_norm(z, g_ref[...], b_ref[...])

    @pl.when(i + ahead < n)
    def _():
        s_new = (i + ahead) % nbuf
        wait_idx(i + ahead, s_new)
        for r in range(tm):
            issue_row(s_new, r)
        finish()

    @pl.when(i + ahead >= n)
    def _():
        finish()


def _final(dest_tiles, gate_tiles, y_sorted, x1, sp, ln_g, ln_b):
    t = x1.shape[0]
    tm = min(FIN_TM, t)
    full = lambda a, b: pl.BlockSpec((a, b), lambda i: (0, 0))
    rowb = pl.BlockSpec((tm, D_MODEL), lambda i: (i, 0))
    return pl.pallas_call(
        _final_kernel,
        grid=(t // tm,),
        in_specs=[pl.BlockSpec(memory_space=pl.ANY), pl.BlockSpec(memory_space=pl.ANY),
                  pl.BlockSpec(memory_space=pl.ANY), rowb, rowb, full(1, D_MODEL), full(1, D_MODEL)],
        out_specs=rowb,
        out_shape=jax.ShapeDtypeStruct((t, D_MODEL), F32),
        scratch_shapes=[pltpu.SMEM((FIN_BUFS, TOPK, tm), jnp.int32),
                        pltpu.SMEM((FIN_BUFS, TOPK, tm), F32),
                        pltpu.VMEM((FIN_BUFS, TOPK, tm * ROW_TILES, LANES), F32),
                        pltpu.VMEM((tm * ROW_TILES, LANES), F32),
                        pltpu.SemaphoreType.DMA((FIN_BUFS,)),
                        pltpu.SemaphoreType.DMA((FIN_BUFS,))],
        compiler_params=_cparams(("arbitrary",)),
        name="combine_ln",
    )(dest_tiles, gate_tiles, y_sorted, x1, sp, ln_g, ln_b)


def _dispatch_plan(idx_tiles, rank_tiles, counts, bm, n_blocks):
    padded = ((counts + bm - 1) // bm) * bm
    pend = jnp.cumsum(padded).astype(jnp.int32)
    pstart = pend - padded
    experts = jnp.arange(N_EXPERTS, dtype=jnp.int32)
    start_of = jnp.sum(jnp.where(idx_tiles[..., None] == experts, pstart, 0), axis=-1)
    dest_tiles = (start_of + rank_tiles).astype(jnp.int32)
    block_row = jnp.arange(n_blocks, dtype=jnp.int32) * bm
    block_expert = jnp.minimum(jnp.sum((pend[None, :] <= block_row[:, None]).astype(jnp.int32), axis=1),
                               N_EXPERTS - 1)
    n_used = (pend[-1] // bm).reshape(1)
    return pend, padded.astype(jnp.int32), dest_tiles, block_expert, n_used


def _rotary_lane_tables(positions):
    inv_freq = ROPE_THETA ** (-jnp.arange(0, ROT_DIM, 2, dtype=F32) / ROT_DIM)
    ang = positions.astype(F32).reshape(-1)[:, None] * inv_freq
    cos, sin = jnp.cos(ang), jnp.sin(ang)
    half = ROT_DIM // 2
    t = ang.shape[0]
    one = jnp.ones((t, HEAD_DIM - ROT_DIM), F32)
    zero = jnp.zeros((t, HEAD_DIM - ROT_DIM), F32)
    zh = jnp.zeros((t, half), F32)
    cos64 = jnp.concatenate([cos, cos, one], axis=1)
    sa64 = jnp.concatenate([zh, sin, zero], axis=1)
    sb64 = jnp.concatenate([-sin, zh, zero], axis=1)
    tile2 = lambda m: jnp.concatenate([m, m], axis=1)
    return tile2(cos64), tile2(sa64), tile2(sb64)


def _permute_in_cols(w):
    off = [0]
    for s in IN_SIZES:
        off.append(off[-1] + s)
    aq, ak, av, hq, hf, hi, hg, lx, lg = [w[:, off[k]:off[k + 1]] for k in range(9)]
    parts = [aq]
    for hd in range(HG_HEADS):
        sl = slice(hd * HG_HEAD_DIM, (hd + 1) * HG_HEAD_DIM)
        parts += [hq[:, sl], hf[:, sl], hi[:, sl], hg[:, sl]]
    parts += [lx, lg, ak, av]
    return jnp.concatenate(parts, axis=1)


def _block_diag(w):
    hds, d, _ = w.shape
    eye = jnp.eye(hds, dtype=w.dtype)
    return (eye[:, None, :, None] * w[:, :, None, :]).reshape(hds * d, hds * d)


def kernel(x, p, positions, w_in, w_out, attn_sinks, attn_norm, hg_lb_logits, hg_norm, lru_conv_w, lru_conv_b, lru_wa, lru_ba, lru_wx, lru_bx, lru_lambda, lru_norm, ln1_g, ln1_b, router_w, router_b, exp_w1, exp_w3, exp_w2, sh_w1, sh_w3, sh_w2, ple_w, ple_gate_w, ple_gate_b, ln2_g, ln2_b):
    bsz, seq, _ = x.shape
    t = bsz * seq
    depth = w_in.shape[0]
    lb_sm = jax.nn.softmax(hg_lb_logits.astype(F32), axis=0)
    hg_lb = jnp.maximum(jnp.cumsum(lb_sm, axis=0) - lb_sm[0], 0.0)
    cos_t, sa_t, sb_t = _rotary_lane_tables(positions)
    p3 = p.reshape(depth, t, PLE_DIM)
    bm = min(MOE_BM, t)
    n_blocks = t * TOPK // bm + N_EXPERTS
    row = lambda v: v.reshape(1, -1)

    xc = x.reshape(t, D_MODEL)
    for i in range(depth):
        h = _inproj(xc, _permute_in_cols(w_in[i].astype(BF16)))
        ya = _attention(h, attn_sinks[i], cos_t, sa_t, sb_t, row(attn_norm[i]), bsz, seq)
        yh = _hgrn2(h, row(hg_lb[i]), row(hg_norm[i]), bsz, seq)
        yl = _rglru(h, lru_conv_w[i], row(lru_conv_b[i]), _block_diag(lru_wa[i]).astype(BF16), row(lru_ba[i]),
                    _block_diag(lru_wx[i]).astype(BF16), row(lru_bx[i]), row(lru_lambda[i]), row(lru_norm[i]),
                    bsz, seq)
        x1, idx_tiles, rank_tiles, gate_tiles, counts = _outproj_router(
            ya, yh, yl, xc, w_out[i].astype(BF16), row(ln1_g[i]), row(ln1_b[i]),
            router_w[i].T, router_b[i].reshape(N_EXPERTS, 1))
        pend, padded, dest_tiles, block_expert, n_used = _dispatch_plan(idx_tiles, rank_tiles, counts[:, 0], bm, n_blocks)
        xs, sp = _dispatch(i, pend, padded, dest_tiles, x1, p3, sh_w1[i].astype(BF16), sh_w3[i].astype(BF16),
                           sh_w2[i].astype(BF16), ple_w[i].astype(BF16), ple_gate_w[i].astype(BF16),
                           row(ple_gate_b[i]), n_blocks * bm, bm)
        y_sorted = _moe(i, block_expert, n_used, xs, exp_w1, exp_w3, exp_w2, bm)
        xc = _final(dest_tiles, gate_tiles, y_sorted, x1, sp, row(ln2_g[i]), row(ln2_b[i]))
    return xc.reshape(bsz, seq, D_MODEL)
```

```python
import jax
import jax.numpy as jnp
from jax import lax
from jax.experimental import pallas as pl
from jax.experimental.pallas import tpu as pltpu

F32 = jnp.float32
BF16 = jnp.bfloat16

D_MODEL = 2048
PLE_DIM = 256
N_HEADS = 16
KV_HEADS = 2
Q_PER_KV = N_HEADS // KV_HEADS
HEAD_DIM = 64
ATTN_WIDTH = N_HEADS * HEAD_DIM
KV_WIDTH = KV_HEADS * HEAD_DIM
ATTN_BLOCK = 128
ROT_DIM = HEAD_DIM // 4
ROPE_THETA = 500000.0
HG_HEADS = 4
HG_HEAD_DIM = 128
HG_WIDTH = HG_HEADS * HG_HEAD_DIM
HG_CHUNK = 64
HG_SUB = 8
LRU_HEADS = 8
LRU_HEAD_DIM = 64
LRU_WIDTH = LRU_HEADS * LRU_HEAD_DIM
CONV_WIDTH = 4
LRU_C = 8.0
IN_SIZES = (ATTN_WIDTH, KV_WIDTH, KV_WIDTH, HG_WIDTH, HG_WIDTH, HG_WIDTH, HG_WIDTH, LRU_WIDTH, LRU_WIDTH)
IN_COLS = sum(IN_SIZES)
MIX_WIDTH = ATTN_WIDTH + HG_WIDTH + LRU_WIDTH
N_EXPERTS = 64
TOPK = 8
N_GROUPS = 8
TOPK_GROUPS = 4
EXPERTS_PER_GROUP = N_EXPERTS // N_GROUPS
EXPERT_DIM = 512
ROUTED_SCALE = 2.5
DEPTH = 2
DEEPNORM_ALPHA = (2 * DEPTH) ** 0.25
RMS_EPS = 1e-6
LN_EPS = 1e-5
NEG_BIG = -1e30
NEG_PICKED = -3e38
TINY = 1e-30

LANES = 128
ROW_TILES = D_MODEL // LANES
PACK_TILES = ROW_TILES // 2
VMEM_LIMIT = 56 * 1024 * 1024

COL_Q = 0
COL_HG = ATTN_WIDTH
COL_LRU = COL_HG + 4 * HG_WIDTH
COL_KV = COL_LRU + 2 * LRU_WIDTH

INPROJ_TM = 512
INPROJ_TN = IN_COLS // 2
HG_ROWS = 512
LRU_ROWS = 256
OUT_TM = 512
MOE_BM = 512
MOE_SUB = 128
IDX_TILE = 128
DISP_TM = 256
FIN_TM = IDX_TILE
FIN_BUFS = 3


def _cparams(sem):
    return pltpu.CompilerParams(dimension_semantics=sem, vmem_limit_bytes=VMEM_LIMIT)


def _to_row_tiles(ref, val, base=0):
    rows, n = val.shape[0], val.shape[1] // LANES
    for s in range(n):
        ref[pl.ds(base + s, rows, stride=n), :] = val[:, s * LANES:(s + 1) * LANES]


def _from_row_tiles(ref, n, base=0, rows=None):
    rows = ref.shape[0] // n if rows is None else rows
    return jnp.concatenate([ref[pl.ds(base + s, rows, stride=n), :] for s in range(n)], axis=1)


def _pack_bf16_pairs(x):
    c = x.shape[1] // 2
    as_bits = lambda v: lax.bitcast_convert_type(v.astype(BF16).astype(F32), jnp.uint32)
    return (as_bits(x[:, c:]) & jnp.uint32(0xFFFF0000)) | (as_bits(x[:, :c]) >> 16)


def _unpack_bf16_pairs(w):
    lo = lax.bitcast_convert_type(w << 16, F32).astype(BF16)
    hi = lax.bitcast_convert_type(w & jnp.uint32(0xFFFF0000), F32).astype(BF16)
    return jnp.concatenate([lo, hi], axis=1)


def _inproj_kernel(x_ref, w_ref, o_ref):
    o_ref[...] = jnp.dot(x_ref[...].astype(BF16), w_ref[...], preferred_element_type=F32)


def _inproj(x2d, w):
    t = x2d.shape[0]
    tm = min(INPROJ_TM, t)
    return pl.pallas_call(
        _inproj_kernel,
        grid=(IN_COLS // INPROJ_TN, t // tm),
        in_specs=[pl.BlockSpec((tm, D_MODEL), lambda j, i: (i, 0)),
                  pl.BlockSpec((D_MODEL, INPROJ_TN), lambda j, i: (0, j))],
        out_specs=pl.BlockSpec((tm, INPROJ_TN), lambda j, i: (i, j)),
        out_shape=jax.ShapeDtypeStruct((t, IN_COLS), F32),
        compiler_params=_cparams(("arbitrary", "arbitrary")),
        name="inproj",
    )(x2d, w)


def _attn_kernel(sink_ref, q_ref, kv_ref, cos_ref, sa_ref, sb_ref, gain_ref, o_ref, kprev, vprev):
    n = pl.program_id(1)

    @pl.when(n == 0)
    def _():
        kprev[...] = jnp.zeros_like(kprev)
        vprev[...] = jnp.zeros_like(vprev)

    cos = cos_ref[...]
    sa = sa_ref[...]
    sb = sb_ref[...]

    def rot(t):
        return t * cos + pltpu.roll(t, 8, axis=1) * sa + pltpu.roll(t, LANES - 8, axis=1) * sb

    kc = rot(kv_ref[:, 0:KV_WIDTH])
    vc = kv_ref[:, KV_WIDTH:2 * KV_WIDTH]
    kb = jnp.concatenate([kprev[...], kc], axis=0).astype(BF16)
    vb = jnp.concatenate([vprev[...], vc], axis=0).astype(BF16)

    row = lax.broadcasted_iota(jnp.int32, (ATTN_BLOCK, 2 * ATTN_BLOCK), 0)
    col = lax.broadcasted_iota(jnp.int32, (ATTN_BLOCK, 2 * ATTN_BLOCK), 1)
    rel = row + ATTN_BLOCK - col
    allowed = (rel >= 0) & (rel < ATTN_BLOCK) & ((col >= ATTN_BLOCK) | (n > 0))

    outs = []
    for pair in range(N_HEADS // 2):
        qt = rot(q_ref[:, pair * LANES:(pair + 1) * LANES])
        for sub in range(2):
            h = pair * 2 + sub
            kvh = h // Q_PER_KV
            qh = qt[:, sub * HEAD_DIM:(sub + 1) * HEAD_DIM].astype(BF16)
            kh = kb[:, kvh * HEAD_DIM:(kvh + 1) * HEAD_DIM]
            vh = vb[:, kvh * HEAD_DIM:(kvh + 1) * HEAD_DIM]
            s = lax.dot_general(qh, kh, (((1,), (1,)), ((), ())), preferred_element_type=F32) * (HEAD_DIM ** -0.5)
            s = jnp.where(allowed, s, NEG_BIG)
            sink = sink_ref[h]
            m = jnp.maximum(jnp.max(s, axis=-1, keepdims=True), sink)
            pr = jnp.exp(s - m)
            den = jnp.sum(pr, axis=-1, keepdims=True) + jnp.exp(sink - m)
            probs = pr / den
            outs.append(jnp.dot(probs.astype(BF16), vh, preferred_element_type=F32))
    o = jnp.concatenate(outs, axis=1)
    y = o * lax.rsqrt(jnp.mean(o * o, axis=-1, keepdims=True) + RMS_EPS) * gain_ref[...]
    o_ref[...] = y.astype(o_ref.dtype)
    kprev[...] = kc
    vprev[...] = vc


def _attention(h, sinks, cos_t, sa_t, sb_t, gain, bsz, seq):
    t = bsz * seq
    nb = seq // ATTN_BLOCK
    rowmap = lambda b, n: (b * nb + n, 0)
    return pl.pallas_call(
        _attn_kernel,
        grid=(bsz, nb),
        in_specs=[pl.BlockSpec(memory_space=pltpu.SMEM),
                  pl.BlockSpec((ATTN_BLOCK, ATTN_WIDTH), lambda b, n: (b * nb + n, COL_Q // ATTN_WIDTH)),
                  pl.BlockSpec((ATTN_BLOCK, 2 * KV_WIDTH), lambda b, n: (b * nb + n, COL_KV // (2 * KV_WIDTH))),
                  pl.BlockSpec((ATTN_BLOCK, LANES), rowmap),
                  pl.BlockSpec((ATTN_BLOCK, LANES), rowmap),
                  pl.BlockSpec((ATTN_BLOCK, LANES), rowmap),
                  pl.BlockSpec((1, ATTN_WIDTH), lambda b, n: (0, 0))],
        out_specs=pl.BlockSpec((ATTN_BLOCK, ATTN_WIDTH), rowmap),
        out_shape=jax.ShapeDtypeStruct((t, ATTN_WIDTH), BF16),
        scratch_shapes=[pltpu.VMEM((ATTN_BLOCK, KV_WIDTH), F32), pltpu.VMEM((ATTN_BLOCK, KV_WIDTH), F32)],
        compiler_params=_cparams(("arbitrary", "arbitrary")),
        name="attn",
    )(sinks, h, h, cos_t, sa_t, sb_t, gain)


def _cumsum_rows(x):
    rows = x.shape[0]
    row = lax.broadcasted_iota(jnp.int32, x.shape, 0)
    d = 1
    while d < rows:
        x = x + jnp.where(row >= d, pltpu.roll(x, d, axis=0), 0.0)
        d *= 2
    return x


def _hgrn_kernel(in_ref, lb_ref, gain_ref, o_ref, st_ref):
    r = pl.program_id(2)

    @pl.when(r == 0)
    def _():
        st_ref[...] = jnp.zeros_like(st_ref)

    c = HG_CHUNK
    nsub = c // HG_SUB
    lb = lb_ref[...]
    gain = gain_ref[...]
    ones_kk = jnp.ones((HG_HEAD_DIM, HG_HEAD_DIM), BF16)
    row_c = lax.broadcasted_iota(jnp.int32, (c, HG_HEAD_DIM), 0)
    row_s = lax.broadcasted_iota(jnp.int32, (HG_SUB, HG_HEAD_DIM), 0)
    nt = (((1,), (1,)), ((), ()))
    tn = (((0,), (0,)), ((), ()))

    def chunk(ci, carry):
        r0 = pl.multiple_of(ci * c, c)
        q = in_ref[pl.ds(r0, c), 0:HG_HEAD_DIM]
        fp = in_ref[pl.ds(r0, c), HG_HEAD_DIM:2 * HG_HEAD_DIM]
        v = in_ref[pl.ds(r0, c), 2 * HG_HEAD_DIM:3 * HG_HEAD_DIM]
        g = in_ref[pl.ds(r0, c), 3 * HG_HEAD_DIM:4 * HG_HEAD_DIM]
        qf = jax.nn.silu(q)
        f = lb + (1.0 - lb) * jax.nn.sigmoid(fp)
        logf = jnp.log(jnp.maximum(f, TINY))
        kf = (1.0 - lb) * jax.nn.sigmoid(-fp)
        b = _cumsum_rows(logf)
        vb = v.astype(BF16)

        s_rows = [jnp.zeros((HG_SUB, c), F32)]
        for i in range(1, nsub):
            lo = i * HG_SUB
            ref_b = b[lo - 1:lo, :]
            qi = qf[lo:lo + HG_SUB, :] * jnp.exp(b[lo:lo + HG_SUB, :] - ref_b)
            ki = kf * jnp.exp(jnp.where(row_c < lo, ref_b - b, NEG_BIG))
            s_rows.append(lax.dot_general(qi.astype(BF16), ki.astype(BF16), nt, preferred_element_type=F32))
        scores = jnp.concatenate(s_rows, axis=0)
        o = jnp.dot(scores.astype(BF16), vb, preferred_element_type=F32)

        o_diag = []
        for j in range(nsub):
            lo = j * HG_SUB
            bs = b[lo:lo + HG_SUB, :]
            ks = kf[lo:lo + HG_SUB, :]
            qs = qf[lo:lo + HG_SUB, :]
            vs = v[lo:lo + HG_SUB, :]
            tiles = []
            for tt in range(HG_SUB):
                e = jnp.where(row_s <= tt, bs[tt:tt + 1, :] - bs, NEG_BIG)
                tiles.append(jnp.exp(e) * ks * qs[tt:tt + 1, :])
            w = jnp.concatenate(tiles, axis=0).astype(BF16)
            dfull = jnp.dot(w, ones_kk, preferred_element_type=F32)
            contrib = dfull.reshape(HG_SUB, HG_SUB, HG_HEAD_DIM) * vs[None, :, :]
            o_diag.append(jnp.sum(contrib, axis=1))
        o = o + jnp.concatenate(o_diag, axis=0)

        st = st_ref[...]
        qb = (qf * jnp.exp(b)).astype(BF16)
        o = o + lax.dot_general(qb, st.astype(BF16), nt, preferred_element_type=F32)
        b_last = b[c - 1:c, :]
        kn = (kf * jnp.exp(b_last - b)).astype(BF16)
        st_ref[...] = st * jnp.exp(b_last) + lax.dot_general(vb, kn, tn, preferred_element_type=F32)

        y = o * lax.rsqrt(jnp.mean(o * o, axis=-1, keepdims=True) + RMS_EPS) * gain
        o_ref[pl.ds(r0, c), :] = (y * jax.nn.silu(g)).astype(o_ref.dtype)
        return carry

    lax.fori_loop(0, in_ref.shape[0] // c, chunk, 0)


def _hgrn2(h, lb, gain, bsz, seq):
    t = bsz * seq
    rows = min(HG_ROWS, seq)
    nr = seq // rows
    hd_block0 = COL_HG // (4 * HG_HEAD_DIM)
    return pl.pallas_call(
        _hgrn_kernel,
        grid=(bsz, HG_HEADS, nr),
        in_specs=[pl.BlockSpec((rows, 4 * HG_HEAD_DIM), lambda b, hd, r: (b * nr + r, hd_block0 + hd)),
                  pl.BlockSpec((1, HG_HEAD_DIM), lambda b, hd, r: (0, hd)),
                  pl.BlockSpec((1, HG_HEAD_DIM), lambda b, hd, r: (0, hd))],
        out_specs=pl.BlockSpec((rows, HG_HEAD_DIM), lambda b, hd, r: (b * nr + r, hd)),
        out_shape=jax.ShapeDtypeStruct((t, HG_WIDTH), BF16),
        scratch_shapes=[pltpu.VMEM((HG_HEAD_DIM, HG_HEAD_DIM), F32)],
        compiler_params=_cparams(("arbitrary", "arbitrary", "arbitrary")),
        name="hgrn2",
    )(h, lb, gain)


def _lru_kernel(in_ref, cw_ref, cb_ref, wa_ref, ba_ref, wx_ref, bx_ref, lam_ref, gain_ref, o_ref,
                tail_ref, h_ref):
    r = pl.program_id(1)

    @pl.when(r == 0)
    def _():
        tail_ref[...] = jnp.zeros_like(tail_ref)
        h_ref[...] = jnp.zeros_like(h_ref)

    rows = in_ref.shape[0]
    x = in_ref[:, 0:LRU_WIDTH]
    gr = in_ref[:, LRU_WIDTH:2 * LRU_WIDTH]
    tail = tail_ref[...]
    row8 = lax.broadcasted_iota(jnp.int32, (8, LRU_WIDTH), 0)

    xc = x * cw_ref[CONV_WIDTH - 1:CONV_WIDTH, :] + cb_ref[...]
    for k in range(1, CONV_WIDTH):
        xs = pltpu.roll(x, k, axis=0)
        head = jnp.where(row8 < k, pltpu.roll(tail, k, axis=0), xs[0:8, :])
        xs = jnp.concatenate([head, xs[8:, :]], axis=0)
        xc = xc + xs * cw_ref[CONV_WIDTH - 1 - k:CONV_WIDTH - k, :]
    tail_ref[...] = x[rows - 8:rows, :]

    xcb = xc.astype(BF16)
    rg = jax.nn.sigmoid(jnp.dot(xcb, wa_ref[...], preferred_element_type=F32) + ba_ref[...])
    ig = jax.nn.sigmoid(jnp.dot(xcb, wx_ref[...], preferred_element_type=F32) + bx_ref[...])
    log_a = -LRU_C * rg * jax.nn.softplus(-lam_ref[...])
    a = jnp.exp(log_a)
    th = jnp.tanh(log_a)
    neg_expm1 = -2.0 * th / (1.0 - th)
    u = jnp.sqrt(jnp.maximum(neg_expm1, 0.0)) * (ig * xc)

    row = lax.broadcasted_iota(jnp.int32, (rows, LRU_WIDTH), 0)
    d = 1
    while d < rows:
        keep = row >= d
        a_s = jnp.where(keep, pltpu.roll(a, d, axis=0), 1.0)
        u_s = jnp.where(keep, pltpu.roll(u, d, axis=0), 0.0)
        u = a * u_s + u
        a = a * a_s
        d *= 2
    hcur = u + a * h_ref[0:1, :]
    h_ref[...] = jnp.broadcast_to(hcur[rows - 1:rows, :], h_ref.shape)

    y = hcur * lax.rsqrt(jnp.mean(hcur * hcur, axis=-1, keepdims=True) + RMS_EPS) * gain_ref[...]
    o_ref[...] = (y * jax.nn.gelu(gr)).astype(o_ref.dtype)


def _rglru(h, cw, cb, wa, ba, wx, bx, lam, gain, bsz, seq):
    t = bsz * seq
    rows = min(LRU_ROWS, seq)
    nr = seq // rows
    vec = pl.BlockSpec((1, LRU_WIDTH), lambda b, r: (0, 0))
    mat = pl.BlockSpec((LRU_WIDTH, LRU_WIDTH), lambda b, r: (0, 0))
    return pl.pallas_call(
        _lru_kernel,
        grid=(bsz, nr),
        in_specs=[pl.BlockSpec((rows, 2 * LRU_WIDTH), lambda b, r: (b * nr + r, COL_LRU // (2 * LRU_WIDTH))),
                  pl.BlockSpec((CONV_WIDTH, LRU_WIDTH), lambda b, r: (0, 0)),
                  vec, mat, vec, mat, vec, vec, vec],
        out_specs=pl.BlockSpec((rows, LRU_WIDTH), lambda b, r: (b * nr + r, 0)),
        out_shape=jax.ShapeDtypeStruct((t, LRU_WIDTH), BF16),
        scratch_shapes=[pltpu.VMEM((8, LRU_WIDTH), F32), pltpu.VMEM((8, LRU_WIDTH), F32)],
        compiler_params=_cparams(("arbitrary", "arbitrary")),
        name="rglru",
    )(h, cw, cb, wa, ba, wx, bx, lam, gain)


def _layer_norm(z, g, b):
    mu = jnp.mean(z, axis=-1, keepdims=True)
    zc = z - mu
    var = jnp.mean(zc * zc, axis=-1, keepdims=True)
    return zc * lax.rsqrt(var + LN_EPS) * g + b


def _outproj_kernel(ya_ref, yh_ref, yl_ref, x_ref, wa_ref, wh_ref, wl_ref, g_ref, b_ref, rw_ref, rb_ref,
                    x1_ref, idx_ref, rank_ref, gate_ref, counts_ref, carry_ref):
    i = pl.program_id(0)

    @pl.when(i == 0)
    def _():
        carry_ref[...] = jnp.zeros_like(carry_ref)

    mixed = jnp.dot(ya_ref[...], wa_ref[...], preferred_element_type=F32)
    mixed = mixed + jnp.dot(yh_ref[...], wh_ref[...], preferred_element_type=F32)
    mixed = mixed + jnp.dot(yl_ref[...], wl_ref[...], preferred_element_type=F32)
    x1 = _layer_norm(DEEPNORM_ALPHA * x_ref[...] + mixed, g_ref[...], b_ref[...])
    x1_ref[...] = x1

    tm = x1.shape[0]
    logits = lax.dot_general(rw_ref[...], x1, (((1,), (1,)), ((), ())),
                             precision=lax.Precision.HIGHEST, preferred_element_type=F32)
    scores = jax.nn.sigmoid(logits)
    sel = scores + rb_ref[...]
    shp = (N_GROUPS, EXPERTS_PER_GROUP, tm)
    s3 = sel.reshape(shp)
    sc3 = scores.reshape(shp)
    e_in = lax.broadcasted_iota(jnp.int32, shp, 1)
    g_id = lax.broadcasted_iota(jnp.int32, shp, 0)
    e_id = g_id * EXPERTS_PER_GROUP + e_in

    m1 = jnp.max(s3, axis=1, keepdims=True)
    i1 = jnp.min(jnp.where(s3 == m1, e_in, EXPERTS_PER_GROUP), axis=1, keepdims=True)
    m2 = jnp.max(jnp.where(e_in == i1, NEG_PICKED, s3), axis=1, keepdims=True)
    gs = m1 + m2
    g1 = lax.broadcasted_iota(jnp.int32, gs.shape, 0)
    gsel = jnp.zeros(gs.shape, jnp.int32)
    cur = gs
    for _ in range(TOPK_GROUPS):
        m = jnp.max(cur, axis=0, keepdims=True)
        pick = g1 == jnp.min(jnp.where(cur == m, g1, N_GROUPS), axis=0, keepdims=True)
        gsel = jnp.where(pick, 1, gsel)
        cur = jnp.where(pick, NEG_PICKED, cur)
    cur = jnp.where(gsel > 0, s3, NEG_BIG)

    def pick_sum(pick, vals):
        return jnp.sum(jnp.sum(jnp.where(pick, vals, 0.0), axis=1, keepdims=True), axis=0, keepdims=True).reshape(1, tm)

    idx_rows, w_rows, picks = [], [], []
    onehot = jnp.zeros(shp, F32)
    for _ in range(TOPK):
        m = jnp.max(jnp.max(cur, axis=1, keepdims=True), axis=0, keepdims=True)
        cand = jnp.where(cur == m, e_id, N_EXPERTS)
        ii = jnp.min(jnp.min(cand, axis=1, keepdims=True), axis=0, keepdims=True)
        pick = e_id == ii
        picks.append(pick)
        onehot = jnp.where(pick, 1.0, onehot)
        w_rows.append(pick_sum(pick, sc3))
        idx_rows.append(ii.reshape(1, tm))
        cur = jnp.where(pick, NEG_PICKED, cur)
    w = jnp.concatenate(w_rows, axis=0)
    idx = jnp.concatenate(idx_rows, axis=0)
    gates = w / jnp.sum(w, axis=0, keepdims=True) * ROUTED_SCALE

    oh = onehot.reshape(N_EXPERTS, tm).astype(BF16)
    r_i = lax.broadcasted_iota(jnp.int32, (tm, tm), 0)
    c_i = lax.broadcasted_iota(jnp.int32, (tm, tm), 1)
    before = jnp.where(r_i < c_i, 1.0, 0.0).astype(BF16)
    carry = carry_ref[...]
    prefix3 = (jnp.dot(oh, before, preferred_element_type=F32) + carry).reshape(shp)
    rank = jnp.concatenate([pick_sum(pk, prefix3) for pk in picks], axis=0).astype(jnp.int32)
    carry = carry + jnp.dot(oh, jnp.ones((tm, tm), BF16), preferred_element_type=F32)
    carry_ref[...] = carry
    counts_ref[...] = carry.astype(jnp.int32)

    for c in range(tm // IDX_TILE):
        sl = slice(c * IDX_TILE, (c + 1) * IDX_TILE)
        idx_ref[c] = idx[:, sl]
        rank_ref[c] = rank[:, sl]
        gate_ref[c] = gates[:, sl]


def _outproj_router(ya, yh, yl, x2d, w_out, ln_g, ln_b, rw_t, rb):
    t = x2d.shape[0]
    tm = min(OUT_TM, t)
    nt = tm // IDX_TILE
    rowb = lambda w: pl.BlockSpec((tm, w), lambda i: (i, 0))
    full = lambda a, b: pl.BlockSpec((a, b), lambda i: (0, 0))
    tiles = pl.BlockSpec((nt, TOPK, IDX_TILE), lambda i: (i, 0, 0))
    wa = pl.BlockSpec((ATTN_WIDTH, D_MODEL), lambda i: (0, 0))
    wh = pl.BlockSpec((HG_WIDTH, D_MODEL), lambda i: (ATTN_WIDTH // HG_WIDTH, 0))
    wl = pl.BlockSpec((LRU_WIDTH, D_MODEL), lambda i: ((ATTN_WIDTH + HG_WIDTH) // LRU_WIDTH, 0))
    tile_shape = lambda dt: jax.ShapeDtypeStruct((t // IDX_TILE, TOPK, IDX_TILE), dt)
    return pl.pallas_call(
        _outproj_kernel,
        grid=(t // tm,),
        in_specs=[rowb(ATTN_WIDTH), rowb(HG_WIDTH), rowb(LRU_WIDTH), rowb(D_MODEL), wa, wh, wl,
                  full(1, D_MODEL), full(1, D_MODEL), full(N_EXPERTS, D_MODEL), full(N_EXPERTS, 1)],
        out_specs=[rowb(D_MODEL), tiles, tiles, tiles, full(N_EXPERTS, tm)],
        out_shape=[jax.ShapeDtypeStruct((t, D_MODEL), F32),
                   tile_shape(jnp.int32), tile_shape(jnp.int32), tile_shape(F32),
                   jax.ShapeDtypeStruct((N_EXPERTS, tm), jnp.int32)],
        scratch_shapes=[pltpu.VMEM((N_EXPERTS, tm), F32)],
        compiler_params=_cparams(("arbitrary",)),
        name="outproj_router",
    )(ya, yh, yl, x2d, w_out, w_out, w_out, ln_g, ln_b, rw_t, rb)


def _dispatch_kernel(pend_ref, padded_ref, dest_hbm, x_ref, p_ref, sw1_ref, sw3_ref, sw2_ref, pw_ref,
                     gw_ref, gb_ref, xs_hbm, sp_ref, dest_smem, zeros_ref, rows_ref, sem_idx, sem_rows, sem_zero):
    i = pl.program_id(0)
    tm = x_ref.shape[0]
    bm = zeros_ref.shape[0] // PACK_TILES
    nt = tm // IDX_TILE

    @pl.when(i == 0)
    def _():
        zeros_ref[...] = jnp.zeros_like(zeros_ref)

        def fill(e, carry):
            @pl.when(padded_ref[e] > 0)
            def _():
                start = pl.multiple_of(pend_ref[e] - bm, bm)
                pltpu.make_async_copy(zeros_ref, xs_hbm.at[pl.ds(start * PACK_TILES, bm * PACK_TILES)], sem_zero).start()
            return carry

        def fill_wait(e, carry):
            @pl.when(padded_ref[e] > 0)
            def _():
                pltpu.make_async_copy(zeros_ref, xs_hbm.at[pl.ds(0, bm * PACK_TILES)], sem_zero).wait()
            return carry

        def fill_tail(blk, carry):
            start = pl.multiple_of(blk * bm, bm)
            pltpu.make_async_copy(zeros_ref, xs_hbm.at[pl.ds(start * PACK_TILES, bm * PACK_TILES)], sem_zero).start()
            return carry

        def fill_tail_wait(blk, carry):
            pltpu.make_async_copy(zeros_ref, xs_hbm.at[pl.ds(0, bm * PACK_TILES)], sem_zero).wait()
            return carry

        n_used = pend_ref[N_EXPERTS - 1] // bm
        n_blocks = xs_hbm.shape[0] // (bm * PACK_TILES)
        lax.fori_loop(0, N_EXPERTS, fill, 0)
        lax.fori_loop(n_used, n_blocks, fill_tail, 0)
        lax.fori_loop(0, N_EXPERTS, fill_wait, 0)
        lax.fori_loop(n_used, n_blocks, fill_tail_wait, 0)

    idx_cp = pltpu.make_async_copy(dest_hbm.at[pl.ds(i * nt, nt)], dest_smem, sem_idx)
    idx_cp.start()
    x = x_ref[...]
    _to_row_tiles(rows_ref, _pack_bf16_pairs(x))
    idx_cp.wait()

    for c in range(nt):
        for r in range(IDX_TILE):
            row = c * IDX_TILE + r
            for j in range(TOPK):
                d = dest_smem[c, j, r]
                pltpu.make_async_copy(rows_ref.at[pl.ds(row * PACK_TILES, PACK_TILES)],
                                      xs_hbm.at[pl.ds(d * PACK_TILES, PACK_TILES)], sem_rows).start(priority=j % 2)

    xb = x.astype(BF16)
    h1 = jnp.dot(xb, sw1_ref[...], preferred_element_type=F32)
    h3 = jnp.dot(xb, sw3_ref[...], preferred_element_type=F32)
    shared = jnp.dot((jax.nn.silu(h1) * h3).astype(BF16), sw2_ref[...], preferred_element_type=F32)
    gate = jax.nn.sigmoid(jnp.dot(xb, gw_ref[...], preferred_element_type=F32) + gb_ref[...])
    ple = gate * jnp.dot(p_ref[...].astype(BF16), pw_ref[...], preferred_element_type=F32)
    sp_ref[...] = shared + ple

    n_copied = TOPK * tm * PACK_TILES
    pltpu.make_async_copy(xs_hbm.at[pl.ds(0, n_copied)], xs_hbm.at[pl.ds(0, n_copied)], sem_rows).wait()


def _dispatch(layer, pend, padded, dest_tiles, x1, p3, sw1, sw3, sw2, pw, gw, gb, n_rows, bm):
    t = x1.shape[0]
    tm = min(DISP_TM, t)
    full = lambda a, b: pl.BlockSpec((a, b), lambda i, pe, pa: (0, 0))
    grid_spec = pltpu.PrefetchScalarGridSpec(
        num_scalar_prefetch=2,
        grid=(t // tm,),
        in_specs=[pl.BlockSpec(memory_space=pl.ANY),
                  pl.BlockSpec((tm, D_MODEL), lambda i, pe, pa: (i, 0)),
                  pl.BlockSpec((None, tm, PLE_DIM), lambda i, pe, pa: (layer, i, 0)),
                  full(D_MODEL, EXPERT_DIM), full(D_MODEL, EXPERT_DIM), full(EXPERT_DIM, D_MODEL),
                  full(PLE_DIM, D_MODEL), full(D_MODEL, D_MODEL), full(1, D_MODEL)],
        out_specs=[pl.BlockSpec(memory_space=pl.ANY),
                   pl.BlockSpec((tm, D_MODEL), lambda i, pe, pa: (i, 0))],
        scratch_shapes=[pltpu.SMEM((tm // IDX_TILE, TOPK, IDX_TILE), jnp.int32),
                        pltpu.VMEM((bm * PACK_TILES, LANES), jnp.uint32),
                        pltpu.VMEM((tm * PACK_TILES, LANES), jnp.uint32),
                        pltpu.SemaphoreType.DMA, pltpu.SemaphoreType.DMA, pltpu.SemaphoreType.DMA],
    )
    return pl.pallas_call(
        _dispatch_kernel,
        grid_spec=grid_spec,
        out_shape=[jax.ShapeDtypeStruct((n_rows * PACK_TILES, LANES), jnp.uint32),
                   jax.ShapeDtypeStruct((t, D_MODEL), F32)],
        compiler_params=_cparams(("arbitrary",)),
        name="dispatch_shared_ple",
    )(pend, padded, dest_tiles, x1, p3, sw1, sw3, sw2, pw, gw, gb)


def _moe_kernel(be_ref, nused_ref, x_ref, w1_ref, w3_ref, w2_ref, y_ref, w1b, w3b, w2b):
    n = pl.program_id(0)

    @pl.when(n < nused_ref[0])
    def _():
        e = be_ref[n]
        e_prev = be_ref[jnp.maximum(n - 1, 0)]

        @pl.when((n == 0) | (e != e_prev))
        def _():
            w1b[...] = w1_ref[...].astype(BF16)
            w3b[...] = w3_ref[...].astype(BF16)
            w2b[...] = w2_ref[...].astype(BF16)

        sub = min(MOE_SUB, x_ref.shape[0] // PACK_TILES)
        for c in range(x_ref.shape[0] // (PACK_TILES * sub)):
            xb = _unpack_bf16_pairs(_from_row_tiles(x_ref, PACK_TILES, base=c * sub * PACK_TILES, rows=sub))
            h1 = jnp.dot(xb, w1b[...], preferred_element_type=F32)
            h3 = jnp.dot(xb, w3b[...], preferred_element_type=F32)
            act = (jax.nn.silu(h1) * h3).astype(BF16)
            _to_row_tiles(y_ref, jnp.dot(act, w2b[...], preferred_element_type=F32), base=c * sub * ROW_TILES)

    @pl.when(n >= nused_ref[0])
    def _():
        y_ref[...] = jnp.zeros_like(y_ref)


def _moe(layer, block_expert, n_used, xs, w1, w3, w2, bm):
    n_blocks = xs.shape[0] // (bm * PACK_TILES)
    wspec = lambda a, b: pl.BlockSpec((None, None, a, b), lambda n, be, nu: (layer, be[n], 0, 0))
    grid_spec = pltpu.PrefetchScalarGridSpec(
        num_scalar_prefetch=2,
        grid=(n_blocks,),
        in_specs=[pl.BlockSpec((bm * PACK_TILES, LANES), lambda n, be, nu: (jnp.minimum(n, nu[0] - 1), 0)),
                  wspec(D_MODEL, EXPERT_DIM), wspec(D_MODEL, EXPERT_DIM), wspec(EXPERT_DIM, D_MODEL)],
        out_specs=pl.BlockSpec((bm * ROW_TILES, LANES), lambda n, be, nu: (n, 0)),
        scratch_shapes=[pltpu.VMEM((D_MODEL, EXPERT_DIM), BF16),
                        pltpu.VMEM((D_MODEL, EXPERT_DIM), BF16),
                        pltpu.VMEM((EXPERT_DIM, D_MODEL), BF16)],
    )
    return pl.pallas_call(
        _moe_kernel,
        grid_spec=grid_spec,
        out_shape=jax.ShapeDtypeStruct((n_blocks * bm * ROW_TILES, LANES), F32),
        compiler_params=_cparams(("arbitrary",)),
        name="moe_experts",
    )(block_expert, n_used, xs, w1, w3, w2)


def _final_kernel(dest_hbm, gate_hbm, y_hbm, x_ref, sp_ref, g_ref, b_ref, o_ref,
                  dest_smem, gate_smem, ybuf, routed, sem_idx, sem_rows):
    i = pl.program_id(0)
    n = pl.num_programs(0)
    tm = x_ref.shape[0]
    nbuf = ybuf.shape[0]
    slot = i % nbuf
    ahead = nbuf - 1

    def idx_copies(tile, s):
        return (pltpu.make_async_copy(dest_hbm.at[tile], dest_smem.at[s], sem_idx.at[s]),
                pltpu.make_async_copy(gate_hbm.at[tile], gate_smem.at[s], sem_idx.at[s]))

    def start_idx(tile, s):
        for cp in idx_copies(tile, s):
            cp.start()

    def wait_idx(tile, s):
        for cp in idx_copies(tile, s):
            cp.wait()

    def issue_row(s, r):
        for j in range(TOPK):
            d = dest_smem[s, j, r]
            pltpu.make_async_copy(y_hbm.at[pl.ds(d * ROW_TILES, ROW_TILES)],
                                  ybuf.at[s, j, pl.ds(r * ROW_TILES, ROW_TILES)], sem_rows.at[s]).start(priority=j % 2)

    def issue_rows_loop(s):
        def issue(r, carry):
            issue_row(s, r)
            return carry

        lax.fori_loop(0, tm, issue, 0)

    @pl.when(i == 0)
    def _():
        for k in range(ahead):
            @pl.when(k < n)
            def _(k=k):
                start_idx(k, k)
                wait_idx(k, k)
                issue_rows_loop(k)

        @pl.when(ahead < n)
        def _():
            start_idx(ahead, ahead)

    for j in range(TOPK):
        pltpu.make_async_copy(y_hbm.at[pl.ds(0, tm * ROW_TILES)], ybuf.at[slot, j], sem_rows.at[slot]).wait()

    def combine(r, carry):
        rows = pl.ds(pl.multiple_of(r * ROW_TILES, ROW_TILES), ROW_TILES)
        acc = gate_smem[slot, 0, r] * ybuf[slot, 0, rows, :]
        for j in range(1, TOPK):
            acc = acc + gate_smem[slot, j, r] * ybuf[slot, j, rows, :]
        routed[rows, :] = acc
        return carry

    lax.fori_loop(0, tm, combine, 0)

    @pl.when(i + nbuf < n)
    def _():
        start_idx(i + nbuf, slot)

    def finish():
        z = DEEPNORM_ALPHA * x_ref[...] + _from_row_tiles(routed, ROW_TILES) + sp_ref[...]
        o_ref[...] = _layer_norm(z, g_ref[...], b_ref[...])

    @pl.when(i + ahead < n)
    def _():
        s_new = (i + ahead) % nbuf
        wait_idx(i + ahead, s_new)
        for r in range(tm):
            issue_row(s_new, r)
        finish()

    @pl.when(i + ahead >= n)
    def _():
        finish()


def _final(dest_tiles, gate_tiles, y_sorted, x1, sp, ln_g, ln_b):
    t = x1.shape[0]
    tm = min(FIN_TM, t)
    full = lambda a, b: pl.BlockSpec((a, b), lambda i: (0, 0))
    rowb = pl.BlockSpec((tm, D_MODEL), lambda i: (i, 0))
    return pl.pallas_call(
        _final_kernel,
        grid=(t // tm,),
        in_specs=[pl.BlockSpec(memory_space=pl.ANY), pl.BlockSpec(memory_space=pl.ANY),
                  pl.BlockSpec(memory_space=pl.ANY), rowb, rowb, full(1, D_MODEL), full(1, D_MODEL)],
        out_specs=rowb,
        out_shape=jax.ShapeDtypeStruct((t, D_MODEL), F32),
        scratch_shapes=[pltpu.SMEM((FIN_BUFS, TOPK, tm), jnp.int32),
                        pltpu.SMEM((FIN_BUFS, TOPK, tm), F32),
                        pltpu.VMEM((FIN_BUFS, TOPK, tm * ROW_TILES, LANES), F32),
                        pltpu.VMEM((tm * ROW_TILES, LANES), F32),
                        pltpu.SemaphoreType.DMA((FIN_BUFS,)),
                        pltpu.SemaphoreType.DMA((FIN_BUFS,))],
        compiler_params=_cparams(("arbitrary",)),
        name="combine_ln",
    )(dest_tiles, gate_tiles, y_sorted, x1, sp, ln_g, ln_b)


def _dispatch_plan(idx_tiles, rank_tiles, counts, bm, n_blocks):
    padded = ((counts + bm - 1) // bm) * bm
    pend = jnp.cumsum(padded).astype(jnp.int32)
    pstart = pend - padded
    experts = jnp.arange(N_EXPERTS, dtype=jnp.int32)
    start_of = jnp.sum(jnp.where(idx_tiles[..., None] == experts, pstart, 0), axis=-1)
    dest_tiles = (start_of + rank_tiles).astype(jnp.int32)
    block_row = jnp.arange(n_blocks, dtype=jnp.int32) * bm
    block_expert = jnp.minimum(jnp.sum((pend[None, :] <= block_row[:, None]).astype(jnp.int32), axis=1),
                               N_EXPERTS - 1)
    n_used = (pend[-1] // bm).reshape(1)
    return pend, padded.astype(jnp.int32), dest_tiles, block_expert, n_used


def _rotary_lane_tables(positions):
    inv_freq = ROPE_THETA ** (-jnp.arange(0, ROT_DIM, 2, dtype=F32) / ROT_DIM)
    ang = positions.astype(F32).reshape(-1)[:, None] * inv_freq
    cos, sin = jnp.cos(ang), jnp.sin(ang)
    half = ROT_DIM // 2
    t = ang.shape[0]
    one = jnp.ones((t, HEAD_DIM - ROT_DIM), F32)
    zero = jnp.zeros((t, HEAD_DIM - ROT_DIM), F32)
    zh = jnp.zeros((t, half), F32)
    cos64 = jnp.concatenate([cos, cos, one], axis=1)
    sa64 = jnp.concatenate([zh, sin, zero], axis=1)
    sb64 = jnp.concatenate([-sin, zh, zero], axis=1)
    tile2 = lambda m: jnp.concatenate([m, m], axis=1)
    return tile2(cos64), tile2(sa64), tile2(sb64)


def _permute_in_cols(w):
    off = [0]
    for s in IN_SIZES:
        off.append(off[-1] + s)
    aq, ak, av, hq, hf, hi, hg, lx, lg = [w[:, off[k]:off[k + 1]] for k in range(9)]
    parts = [aq]
    for hd in range(HG_HEADS):
        sl = slice(hd * HG_HEAD_DIM, (hd + 1) * HG_HEAD_DIM)
        parts += [hq[:, sl], hf[:, sl], hi[:, sl], hg[:, sl]]
    parts += [lx, lg, ak, av]
    return jnp.concatenate(parts, axis=1)


def _block_diag(w):
    hds, d, _ = w.shape
    eye = jnp.eye(hds, dtype=w.dtype)
    return (eye[:, None, :, None] * w[:, :, None, :]).reshape(hds * d, hds * d)


def kernel(x, p, positions, w_in, w_out, attn_sinks, attn_norm, hg_lb_logits, hg_norm, lru_conv_w, lru_conv_b, lru_wa, lru_ba, lru_wx, lru_bx, lru_lambda, lru_norm, ln1_g, ln1_b, router_w, router_b, exp_w1, exp_w3, exp_w2, sh_w1, sh_w3, sh_w2, ple_w, ple_gate_w, ple_gate_b, ln2_g, ln2_b):
    bsz, seq, _ = x.shape
    t = bsz * seq
    depth = w_in.shape[0]
    lb_sm = jax.nn.softmax(hg_lb_logits.astype(F32), axis=0)
    hg_lb = jnp.maximum(jnp.cumsum(lb_sm, axis=0) - lb_sm[0], 0.0)
    cos_t, sa_t, sb_t = _rotary_lane_tables(positions)
    p3 = p.reshape(depth, t, PLE_DIM)
    bm = min(MOE_BM, t)
    n_blocks = t * TOPK // bm + N_EXPERTS
    row = lambda v: v.reshape(1, -1)

    xc = x.reshape(t, D_MODEL)
    for i in range(depth):
        h = _inproj(xc, _permute_in_cols(w_in[i].astype(BF16)))
        ya = _attention(h, attn_sinks[i], cos_t, sa_t, sb_t, row(attn_norm[i]), bsz, seq)
        yh = _hgrn2(h, row(hg_lb[i]), row(hg_norm[i]), bsz, seq)
        yl = _rglru(h, lru_conv_w[i], row(lru_conv_b[i]), _block_diag(lru_wa[i]).astype(BF16), row(lru_ba[i]),
                    _block_diag(lru_wx[i]).astype(BF16), row(lru_bx[i]), row(lru_lambda[i]), row(lru_norm[i]),
                    bsz, seq)
        x1, idx_tiles, rank_tiles, gate_tiles, counts = _outproj_router(
            ya, yh, yl, xc, w_out[i].astype(BF16), row(ln1_g[i]), row(ln1_b[i]),
            router_w[i].T, router_b[i].reshape(N_EXPERTS, 1))
        pend, padded, dest_tiles, block_expert, n_used = _dispatch_plan(idx_tiles, rank_tiles, counts[:, 0], bm, n_blocks)
        xs, sp = _dispatch(i, pend, padded, dest_tiles, x1, p3, sh_w1[i].astype(BF16), sh_w3[i].astype(BF16),
                           sh_w2[i].astype(BF16), ple_w[i].astype(BF16), ple_gate_w[i].astype(BF16),
                           row(ple_gate_b[i]), n_blocks * bm, bm)
        y_sorted = _moe(i, block_expert, n_used, xs, exp_w1, exp_w3, exp_w2, bm)
        xc = _final(dest_tiles, gate_tiles, y_sorted, x1, sp, row(ln2_g[i]), row(ln2_b[i]))
    return xc.reshape(bsz, seq, D_MODEL)
```

```python
import jax
import jax.numpy as jnp
from jax import lax
from jax.experimental import pallas as pl
from jax.experimental.pallas import tpu as pltpu

F32 = jnp.float32
BF16 = jnp.bfloat16

D_MODEL = 2048
PLE_DIM = 256
N_HEADS = 16
KV_HEADS = 2
Q_PER_KV = N_HEADS // KV_HEADS
HEAD_DIM = 64
ATTN_WIDTH = N_HEADS * HEAD_DIM
KV_WIDTH = KV_HEADS * HEAD_DIM
ATTN_BLOCK = 128
ROT_DIM = HEAD_DIM // 4
ROPE_THETA = 500000.0
HG_HEADS = 4
HG_HEAD_DIM = 128
HG_WIDTH = HG_HEADS * HG_HEAD_DIM
HG_CHUNK = 64
HG_SUB = 8
LRU_HEADS = 8
LRU_HEAD_DIM = 64
LRU_WIDTH = LRU_HEADS * LRU_HEAD_DIM
CONV_WIDTH = 4
LRU_C = 8.0
IN_SIZES = (ATTN_WIDTH, KV_WIDTH, KV_WIDTH, HG_WIDTH, HG_WIDTH, HG_WIDTH, HG_WIDTH, LRU_WIDTH, LRU_WIDTH)
IN_COLS = sum(IN_SIZES)
MIX_WIDTH = ATTN_WIDTH + HG_WIDTH + LRU_WIDTH
N_EXPERTS = 64
TOPK = 8
N_GROUPS = 8
TOPK_GROUPS = 4
EXPERTS_PER_GROUP = N_EXPERTS // N_GROUPS
EXPERT_DIM = 512
ROUTED_SCALE = 2.5
DEPTH = 2
DEEPNORM_ALPHA = (2 * DEPTH) ** 0.25
RMS_EPS = 1e-6
LN_EPS = 1e-5
NEG_BIG = -1e30
NEG_PICKED = -3e38
TINY = 1e-30

LANES = 128
ROW_TILES = D_MODEL // LANES
PACK_TILES = ROW_TILES // 2
VMEM_LIMIT = 56 * 1024 * 1024

COL_HG = 0
COL_Q = COL_HG + 4 * HG_WIDTH
COL_LRU = COL_Q + ATTN_WIDTH
COL_KV = COL_LRU + 2 * LRU_WIDTH

INPROJ_TM = 512
INPROJ_TN = IN_COLS // 2
HG_ROWS = 512
HG_GROUP = 4
LRU_ROWS = 256
OUT_TM = 512
MOE_BM = 512
MOE_SUB = 128
IDX_TILE = 128
DISP_TM = 256
FIN_TM = IDX_TILE
FIN_BUFS = 3


def _cparams(sem):
    return pltpu.CompilerParams(dimension_semantics=sem, vmem_limit_bytes=VMEM_LIMIT)


def _to_row_tiles(ref, val, base=0):
    rows, n = val.shape[0], val.shape[1] // LANES
    for s in range(n):
        ref[pl.ds(base + s, rows, stride=n), :] = val[:, s * LANES:(s + 1) * LANES]


def _from_row_tiles(ref, n, base=0, rows=None):
    rows = ref.shape[0] // n if rows is None else rows
    return jnp.concatenate([ref[pl.ds(base + s, rows, stride=n), :] for s in range(n)], axis=1)


def _pack_bf16_pairs(x):
    c = x.shape[1] // 2
    as_bits = lambda v: lax.bitcast_convert_type(v.astype(BF16).astype(F32), jnp.uint32)
    return (as_bits(x[:, c:]) & jnp.uint32(0xFFFF0000)) | (as_bits(x[:, :c]) >> 16)


def _unpack_bf16_pairs(w):
    lo = lax.bitcast_convert_type(w << 16, F32).astype(BF16)
    hi = lax.bitcast_convert_type(w & jnp.uint32(0xFFFF0000), F32).astype(BF16)
    return jnp.concatenate([lo, hi], axis=1)


def _inproj_kernel(x_ref, w_ref, o_ref):
    o_ref[...] = jnp.dot(x_ref[...].astype(BF16), w_ref[...], preferred_element_type=F32)


def _inproj(x2d, w):
    t = x2d.shape[0]
    tm = min(INPROJ_TM, t)
    return pl.pallas_call(
        _inproj_kernel,
        grid=(IN_COLS // INPROJ_TN, t // tm),
        in_specs=[pl.BlockSpec((tm, D_MODEL), lambda j, i: (i, 0)),
                  pl.BlockSpec((D_MODEL, INPROJ_TN), lambda j, i: (0, j))],
        out_specs=pl.BlockSpec((tm, INPROJ_TN), lambda j, i: (i, j)),
        out_shape=jax.ShapeDtypeStruct((t, IN_COLS), F32),
        compiler_params=_cparams(("arbitrary", "arbitrary")),
        name="inproj",
    )(x2d, w)


def _attn_kernel(sink_ref, q_ref, kv_ref, cos_ref, sa_ref, sb_ref, gain_ref, o_ref, kprev, vprev):
    n = pl.program_id(1)

    @pl.when(n == 0)
    def _():
        kprev[...] = jnp.zeros_like(kprev)
        vprev[...] = jnp.zeros_like(vprev)

    cos = cos_ref[...]
    sa = sa_ref[...]
    sb = sb_ref[...]

    def rot(t):
        return t * cos + pltpu.roll(t, 8, axis=1) * sa + pltpu.roll(t, LANES - 8, axis=1) * sb

    kc = rot(kv_ref[:, 0:KV_WIDTH])
    vc = kv_ref[:, KV_WIDTH:2 * KV_WIDTH]
    kb = jnp.concatenate([kprev[...], kc], axis=0).astype(BF16)
    vb = jnp.concatenate([vprev[...], vc], axis=0).astype(BF16)

    row = lax.broadcasted_iota(jnp.int32, (ATTN_BLOCK, 2 * ATTN_BLOCK), 0)
    col = lax.broadcasted_iota(jnp.int32, (ATTN_BLOCK, 2 * ATTN_BLOCK), 1)
    rel = row + ATTN_BLOCK - col
    allowed = (rel >= 0) & (rel < ATTN_BLOCK) & ((col >= ATTN_BLOCK) | (n > 0))

    outs = []
    for pair in range(N_HEADS // 2):
        qt = rot(q_ref[:, pair * LANES:(pair + 1) * LANES])
        for sub in range(2):
            h = pair * 2 + sub
            kvh = h // Q_PER_KV
            qh = qt[:, sub * HEAD_DIM:(sub + 1) * HEAD_DIM].astype(BF16)
            kh = kb[:, kvh * HEAD_DIM:(kvh + 1) * HEAD_DIM]
            vh = vb[:, kvh * HEAD_DIM:(kvh + 1) * HEAD_DIM]
            s = lax.dot_general(qh, kh, (((1,), (1,)), ((), ())), preferred_element_type=F32) * (HEAD_DIM ** -0.5)
            s = jnp.where(allowed, s, NEG_BIG)
            sink = sink_ref[h]
            m = jnp.maximum(jnp.max(s, axis=-1, keepdims=True), sink)
            pr = jnp.exp(s - m)
            den = jnp.sum(pr, axis=-1, keepdims=True) + jnp.exp(sink - m)
            probs = pr / den
            outs.append(jnp.dot(probs.astype(BF16), vh, preferred_element_type=F32))
    o = jnp.concatenate(outs, axis=1)
    y = o * lax.rsqrt(jnp.mean(o * o, axis=-1, keepdims=True) + RMS_EPS) * gain_ref[...]
    o_ref[...] = y.astype(o_ref.dtype)
    kprev[...] = kc
    vprev[...] = vc


def _attention(h, sinks, cos_t, sa_t, sb_t, gain, bsz, seq):
    t = bsz * seq
    nb = seq // ATTN_BLOCK
    rowmap = lambda b, n: (b * nb + n, 0)
    return pl.pallas_call(
        _attn_kernel,
        grid=(bsz, nb),
        in_specs=[pl.BlockSpec(memory_space=pltpu.SMEM),
                  pl.BlockSpec((ATTN_BLOCK, ATTN_WIDTH), lambda b, n: (b * nb + n, COL_Q // ATTN_WIDTH)),
                  pl.BlockSpec((ATTN_BLOCK, 2 * KV_WIDTH), lambda b, n: (b * nb + n, COL_KV // (2 * KV_WIDTH))),
                  pl.BlockSpec((ATTN_BLOCK, LANES), rowmap),
                  pl.BlockSpec((ATTN_BLOCK, LANES), rowmap),
                  pl.BlockSpec((ATTN_BLOCK, LANES), rowmap),
                  pl.BlockSpec((1, ATTN_WIDTH), lambda b, n: (0, 0))],
        out_specs=pl.BlockSpec((ATTN_BLOCK, ATTN_WIDTH), rowmap),
        out_shape=jax.ShapeDtypeStruct((t, ATTN_WIDTH), BF16),
        scratch_shapes=[pltpu.VMEM((ATTN_BLOCK, KV_WIDTH), F32), pltpu.VMEM((ATTN_BLOCK, KV_WIDTH), F32)],
        compiler_params=_cparams(("arbitrary", "arbitrary")),
        name="attn",
    )(sinks, h, h, cos_t, sa_t, sb_t, gain)


def _cumsum_rows(x):
    rows = x.shape[0]
    row = lax.broadcasted_iota(jnp.int32, x.shape, 0)
    d = 1
    while d < rows:
        x = x + jnp.where(row >= d, pltpu.roll(x, d, axis=0), 0.0)
        d *= 2
    return x


def _hgrn_kernel(in_ref, lb_ref, gain_ref, o_ref, st_ref):
    r = pl.program_id(2)

    @pl.when(r == 0)
    def _():
        st_ref[...] = jnp.zeros_like(st_ref)

    c = HG_CHUNK
    nsub = c // HG_SUB
    ones_kk = jnp.ones((HG_HEAD_DIM, HG_HEAD_DIM), BF16)
    row_c = lax.broadcasted_iota(jnp.int32, (c, HG_HEAD_DIM), 0)
    row_s = lax.broadcasted_iota(jnp.int32, (HG_SUB, HG_HEAD_DIM), 0)
    nt = (((1,), (1,)), ((), ()))
    tn = (((0,), (0,)), ((), ()))

    def one_head(r0, hh):
        col = hh * 4 * HG_HEAD_DIM
        hcol = slice(hh * HG_HEAD_DIM, (hh + 1) * HG_HEAD_DIM)
        lb = lb_ref[:, hcol]
        gain = gain_ref[:, hcol]
        q = in_ref[pl.ds(r0, c), col:col + HG_HEAD_DIM]
        fp = in_ref[pl.ds(r0, c), col + HG_HEAD_DIM:col + 2 * HG_HEAD_DIM]
        v = in_ref[pl.ds(r0, c), col + 2 * HG_HEAD_DIM:col + 3 * HG_HEAD_DIM]
        g = in_ref[pl.ds(r0, c), col + 3 * HG_HEAD_DIM:col + 4 * HG_HEAD_DIM]
        qf = jax.nn.silu(q)
        f = lb + (1.0 - lb) * jax.nn.sigmoid(fp)
        logf = jnp.log(jnp.maximum(f, TINY))
        kf = (1.0 - lb) * jax.nn.sigmoid(-fp)
        b = _cumsum_rows(logf)
        vb = v.astype(BF16)

        s_rows = [jnp.zeros((HG_SUB, c), F32)]
        for i in range(1, nsub):
            lo = i * HG_SUB
            ref_b = b[lo - 1:lo, :]
            qi = qf[lo:lo + HG_SUB, :] * jnp.exp(b[lo:lo + HG_SUB, :] - ref_b)
            ki = kf * jnp.exp(jnp.where(row_c < lo, ref_b - b, NEG_BIG))
            s_rows.append(lax.dot_general(qi.astype(BF16), ki.astype(BF16), nt, preferred_element_type=F32))
        scores = jnp.concatenate(s_rows, axis=0)
        o = jnp.dot(scores.astype(BF16), vb, preferred_element_type=F32)

        o_diag = []
        for j in range(nsub):
            lo = j * HG_SUB
            bs = b[lo:lo + HG_SUB, :]
            ks = kf[lo:lo + HG_SUB, :]
            qs = qf[lo:lo + HG_SUB, :]
            vs = v[lo:lo + HG_SUB, :]
            tiles = []
            for tt in range(HG_SUB):
                e = jnp.where(row_s <= tt, bs[tt:tt + 1, :] - bs, NEG_BIG)
                tiles.append(jnp.exp(e) * ks * qs[tt:tt + 1, :])
            w = jnp.concatenate(tiles, axis=0).astype(BF16)
            dfull = jnp.dot(w, ones_kk, preferred_element_type=F32)
            contrib = dfull.reshape(HG_SUB, HG_SUB, HG_HEAD_DIM) * vs[None, :, :]
            o_diag.append(jnp.sum(contrib, axis=1))
        o = o + jnp.concatenate(o_diag, axis=0)

        st = st_ref[hh]
        qb = (qf * jnp.exp(b)).astype(BF16)
        o = o + lax.dot_general(qb, st.astype(BF16), nt, preferred_element_type=F32)
        b_last = b[c - 1:c, :]
        kn = (kf * jnp.exp(b_last - b)).astype(BF16)
        st_ref[hh] = st * jnp.exp(b_last) + lax.dot_general(vb, kn, tn, preferred_element_type=F32)

        y = o * lax.rsqrt(jnp.mean(o * o, axis=-1, keepdims=True) + RMS_EPS) * gain
        o_ref[pl.ds(r0, c), hcol] = (y * jax.nn.silu(g)).astype(o_ref.dtype)

    def chunk(ci, carry):
        r0 = pl.multiple_of(ci * c, c)
        for hh in range(st_ref.shape[0]):
            one_head(r0, hh)
        return carry

    lax.fori_loop(0, in_ref.shape[0] // c, chunk, 0)


def _hgrn2(h, lb, gain, bsz, seq):
    t = bsz * seq
    rows = min(HG_ROWS, seq)
    nr = seq // rows
    hd_block0 = COL_HG // (4 * HG_HEAD_DIM)
    grp = HG_GROUP
    return pl.pallas_call(
        _hgrn_kernel,
        grid=(bsz, HG_HEADS // grp, nr),
        in_specs=[pl.BlockSpec((rows, grp * 4 * HG_HEAD_DIM), lambda b, hd, r: (b * nr + r, hd_block0 // grp + hd)),
                  pl.BlockSpec((1, grp * HG_HEAD_DIM), lambda b, hd, r: (0, hd)),
                  pl.BlockSpec((1, grp * HG_HEAD_DIM), lambda b, hd, r: (0, hd))],
        out_specs=pl.BlockSpec((rows, grp * HG_HEAD_DIM), lambda b, hd, r: (b * nr + r, hd)),
        out_shape=jax.ShapeDtypeStruct((t, HG_WIDTH), BF16),
        scratch_shapes=[pltpu.VMEM((grp, HG_HEAD_DIM, HG_HEAD_DIM), F32)],
        compiler_params=_cparams(("arbitrary", "arbitrary", "arbitrary")),
        name="hgrn2",
    )(h, lb, gain)


def _lru_kernel(in_ref, cw_ref, cb_ref, wa_ref, ba_ref, wx_ref, bx_ref, lam_ref, gain_ref, o_ref,
                tail_ref, h_ref):
    r = pl.program_id(1)

    @pl.when(r == 0)
    def _():
        tail_ref[...] = jnp.zeros_like(tail_ref)
        h_ref[...] = jnp.zeros_like(h_ref)

    rows = in_ref.shape[0]
    x = in_ref[:, 0:LRU_WIDTH]
    gr = in_ref[:, LRU_WIDTH:2 * LRU_WIDTH]
    tail = tail_ref[...]
    row8 = lax.broadcasted_iota(jnp.int32, (8, LRU_WIDTH), 0)

    xc = x * cw_ref[CONV_WIDTH - 1:CONV_WIDTH, :] + cb_ref[...]
    for k in range(1, CONV_WIDTH):
        xs = pltpu.roll(x, k, axis=0)
        head = jnp.where(row8 < k, pltpu.roll(tail, k, axis=0), xs[0:8, :])
        xs = jnp.concatenate([head, xs[8:, :]], axis=0)
        xc = xc + xs * cw_ref[CONV_WIDTH - 1 - k:CONV_WIDTH - k, :]
    tail_ref[...] = x[rows - 8:rows, :]

    xcb = xc.astype(BF16)
    rg = jax.nn.sigmoid(jnp.dot(xcb, wa_ref[...], preferred_element_type=F32) + ba_ref[...])
    ig = jax.nn.sigmoid(jnp.dot(xcb, wx_ref[...], preferred_element_type=F32) + bx_ref[...])
    log_a = -LRU_C * rg * jax.nn.softplus(-lam_ref[...])
    a = jnp.exp(log_a)
    th = jnp.tanh(log_a)
    neg_expm1 = -2.0 * th / (1.0 - th)
    u = jnp.sqrt(jnp.maximum(neg_expm1, 0.0)) * (ig * xc)

    row = lax.broadcasted_iota(jnp.int32, (rows, LRU_WIDTH), 0)
    d = 1
    while d < rows:
        keep = row >= d
        a_s = jnp.where(keep, pltpu.roll(a, d, axis=0), 1.0)
        u_s = jnp.where(keep, pltpu.roll(u, d, axis=0), 0.0)
        u = a * u_s + u
        a = a * a_s
        d *= 2
    hcur = u + a * h_ref[0:1, :]
    h_ref[...] = jnp.broadcast_to(hcur[rows - 1:rows, :], h_ref.shape)

    y = hcur * lax.rsqrt(jnp.mean(hcur * hcur, axis=-1, keepdims=True) + RMS_EPS) * gain_ref[...]
    o_ref[...] = (y * jax.nn.gelu(gr)).astype(o_ref.dtype)


def _rglru(h, cw, cb, wa, ba, wx, bx, lam, gain, bsz, seq):
    t = bsz * seq
    rows = min(LRU_ROWS, seq)
    nr = seq // rows
    vec = pl.BlockSpec((1, LRU_WIDTH), lambda b, r: (0, 0))
    mat = pl.BlockSpec((LRU_WIDTH, LRU_WIDTH), lambda b, r: (0, 0))
    return pl.pallas_call(
        _lru_kernel,
        grid=(bsz, nr),
        in_specs=[pl.BlockSpec((rows, 2 * LRU_WIDTH), lambda b, r: (b * nr + r, COL_LRU // (2 * LRU_WIDTH))),
                  pl.BlockSpec((CONV_WIDTH, LRU_WIDTH), lambda b, r: (0, 0)),
                  vec, mat, vec, mat, vec, vec, vec],
        out_specs=pl.BlockSpec((rows, LRU_WIDTH), lambda b, r: (b * nr + r, 0)),
        out_shape=jax.ShapeDtypeStruct((t, LRU_WIDTH), BF16),
        scratch_shapes=[pltpu.VMEM((8, LRU_WIDTH), F32), pltpu.VMEM((8, LRU_WIDTH), F32)],
        compiler_params=_cparams(("arbitrary", "arbitrary")),
        name="rglru",
    )(h, cw, cb, wa, ba, wx, bx, lam, gain)


def _layer_norm(z, g, b):
    mu = jnp.mean(z, axis=-1, keepdims=True)
    zc = z - mu
    var = jnp.mean(zc * zc, axis=-1, keepdims=True)
    return zc * lax.rsqrt(var + LN_EPS) * g + b


def _outproj_kernel(ya_ref, yh_ref, yl_ref, x_ref, wa_ref, wh_ref, wl_ref, g_ref, b_ref, rw_ref, rb_ref,
                    x1_ref, idx_ref, rank_ref, gate_ref, counts_ref, carry_ref):
    i = pl.program_id(0)

    @pl.when(i == 0)
    def _():
        carry_ref[...] = jnp.zeros_like(carry_ref)

    mixed = jnp.dot(ya_ref[...], wa_ref[...], preferred_element_type=F32)
    mixed = mixed + jnp.dot(yh_ref[...], wh_ref[...], preferred_element_type=F32)
    mixed = mixed + jnp.dot(yl_ref[...], wl_ref[...], preferred_element_type=F32)
    x1 = _layer_norm(DEEPNORM_ALPHA * x_ref[...] + mixed, g_ref[...], b_ref[...])
    x1_ref[...] = x1

    tm = x1.shape[0]
    logits = lax.dot_general(rw_ref[...], x1, (((1,), (1,)), ((), ())),
                             precision=lax.Precision.HIGHEST, preferred_element_type=F32)
    scores = jax.nn.sigmoid(logits)
    sel = scores + rb_ref[...]
    shp = (N_GROUPS, EXPERTS_PER_GROUP, tm)
    s3 = sel.reshape(shp)
    sc3 = scores.reshape(shp)
    e_in = lax.broadcasted_iota(jnp.int32, shp, 1)
    g_id = lax.broadcasted_iota(jnp.int32, shp, 0)
    e_id = g_id * EXPERTS_PER_GROUP + e_in

    m1 = jnp.max(s3, axis=1, keepdims=True)
    i1 = jnp.min(jnp.where(s3 == m1, e_in, EXPERTS_PER_GROUP), axis=1, keepdims=True)
    m2 = jnp.max(jnp.where(e_in == i1, NEG_PICKED, s3), axis=1, keepdims=True)
    gs = m1 + m2
    g1 = lax.broadcasted_iota(jnp.int32, gs.shape, 0)
    gsel = jnp.zeros(gs.shape, jnp.int32)
    cur = gs
    for _ in range(TOPK_GROUPS):
        m = jnp.max(cur, axis=0, keepdims=True)
        pick = g1 == jnp.min(jnp.where(cur == m, g1, N_GROUPS), axis=0, keepdims=True)
        gsel = jnp.where(pick, 1, gsel)
        cur = jnp.where(pick, NEG_PICKED, cur)
    cur = jnp.where(gsel > 0, s3, NEG_BIG)

    def pick_sum(pick, vals):
        return jnp.sum(jnp.sum(jnp.where(pick, vals, 0.0), axis=1, keepdims=True), axis=0, keepdims=True).reshape(1, tm)

    idx_rows, w_rows, picks = [], [], []
    onehot = jnp.zeros(shp, F32)
    for _ in range(TOPK):
        m = jnp.max(jnp.max(cur, axis=1, keepdims=True), axis=0, keepdims=True)
        cand = jnp.where(cur == m, e_id, N_EXPERTS)
        ii = jnp.min(jnp.min(cand, axis=1, keepdims=True), axis=0, keepdims=True)
        pick = e_id == ii
        picks.append(pick)
        onehot = jnp.where(pick, 1.0, onehot)
        w_rows.append(pick_sum(pick, sc3))
        idx_rows.append(ii.reshape(1, tm))
        cur = jnp.where(pick, NEG_PICKED, cur)
    w = jnp.concatenate(w_rows, axis=0)
    idx = jnp.concatenate(idx_rows, axis=0)
    gates = w / jnp.sum(w, axis=0, keepdims=True) * ROUTED_SCALE

    oh = onehot.reshape(N_EXPERTS, tm).astype(BF16)
    r_i = lax.broadcasted_iota(jnp.int32, (tm, tm), 0)
    c_i = lax.broadcasted_iota(jnp.int32, (tm, tm), 1)
    before = jnp.where(r_i < c_i, 1.0, 0.0).astype(BF16)
    carry = carry_ref[...]
    prefix3 = (jnp.dot(oh, before, preferred_element_type=F32) + carry).reshape(shp)
    rank = jnp.concatenate([pick_sum(pk, prefix3) for pk in picks], axis=0).astype(jnp.int32)
    carry = carry + jnp.dot(oh, jnp.ones((tm, tm), BF16), preferred_element_type=F32)
    carry_ref[...] = carry
    counts_ref[...] = carry.astype(jnp.int32)

    for c in range(tm // IDX_TILE):
        sl = slice(c * IDX_TILE, (c + 1) * IDX_TILE)
        idx_ref[c] = idx[:, sl]
        rank_ref[c] = rank[:, sl]
        gate_ref[c] = gates[:, sl]


def _outproj_router(ya, yh, yl, x2d, w_out, ln_g, ln_b, rw_t, rb):
    t = x2d.shape[0]
    tm = min(OUT_TM, t)
    nt = tm // IDX_TILE
    rowb = lambda w: pl.BlockSpec((tm, w), lambda i: (i, 0))
    full = lambda a, b: pl.BlockSpec((a, b), lambda i: (0, 0))
    tiles = pl.BlockSpec((nt, TOPK, IDX_TILE), lambda i: (i, 0, 0))
    wa = pl.BlockSpec((ATTN_WIDTH, D_MODEL), lambda i: (0, 0))
    wh = pl.BlockSpec((HG_WIDTH, D_MODEL), lambda i: (ATTN_WIDTH // HG_WIDTH, 0))
    wl = pl.BlockSpec((LRU_WIDTH, D_MODEL), lambda i: ((ATTN_WIDTH + HG_WIDTH) // LRU_WIDTH, 0))
    tile_shape = lambda dt: jax.ShapeDtypeStruct((t // IDX_TILE, TOPK, IDX_TILE), dt)
    return pl.pallas_call(
        _outproj_kernel,
        grid=(t // tm,),
        in_specs=[rowb(ATTN_WIDTH), rowb(HG_WIDTH), rowb(LRU_WIDTH), rowb(D_MODEL), wa, wh, wl,
                  full(1, D_MODEL), full(1, D_MODEL), full(N_EXPERTS, D_MODEL), full(N_EXPERTS, 1)],
        out_specs=[rowb(D_MODEL), tiles, tiles, tiles, full(N_EXPERTS, tm)],
        out_shape=[jax.ShapeDtypeStruct((t, D_MODEL), F32),
                   tile_shape(jnp.int32), tile_shape(jnp.int32), tile_shape(F32),
                   jax.ShapeDtypeStruct((N_EXPERTS, tm), jnp.int32)],
        scratch_shapes=[pltpu.VMEM((N_EXPERTS, tm), F32)],
        compiler_params=_cparams(("arbitrary",)),
        name="outproj_router",
    )(ya, yh, yl, x2d, w_out, w_out, w_out, ln_g, ln_b, rw_t, rb)


def _dispatch_kernel(pend_ref, padded_ref, dest_hbm, x_ref, p_ref, sw1_ref, sw3_ref, sw2_ref, pw_ref,
                     gw_ref, gb_ref, xs_hbm, sp_ref, dest_smem, zeros_ref, rows_ref, sem_idx, sem_rows, sem_zero):
    i = pl.program_id(0)
    tm = x_ref.shape[0]
    bm = zeros_ref.shape[0] // PACK_TILES
    nt = tm // IDX_TILE

    @pl.when(i == 0)
    def _():
        zeros_ref[...] = jnp.zeros_like(zeros_ref)

        def fill(e, carry):
            @pl.when(padded_ref[e] > 0)
            def _():
                start = pl.multiple_of(pend_ref[e] - bm, bm)
                pltpu.make_async_copy(zeros_ref, xs_hbm.at[pl.ds(start * PACK_TILES, bm * PACK_TILES)], sem_zero).start()
            return carry

        def fill_wait(e, carry):
            @pl.when(padded_ref[e] > 0)
            def _():
                pltpu.make_async_copy(zeros_ref, xs_hbm.at[pl.ds(0, bm * PACK_TILES)], sem_zero).wait()
            return carry

        def fill_tail(blk, carry):
            start = pl.multiple_of(blk * bm, bm)
            pltpu.make_async_copy(zeros_ref, xs_hbm.at[pl.ds(start * PACK_TILES, bm * PACK_TILES)], sem_zero).start()
            return carry

        def fill_tail_wait(blk, carry):
            pltpu.make_async_copy(zeros_ref, xs_hbm.at[pl.ds(0, bm * PACK_TILES)], sem_zero).wait()
            return carry

        n_used = pend_ref[N_EXPERTS - 1] // bm
        n_blocks = xs_hbm.shape[0] // (bm * PACK_TILES)
        lax.fori_loop(0, N_EXPERTS, fill, 0)
        lax.fori_loop(n_used, n_blocks, fill_tail, 0)
        lax.fori_loop(0, N_EXPERTS, fill_wait, 0)
        lax.fori_loop(n_used, n_blocks, fill_tail_wait, 0)

    idx_cp = pltpu.make_async_copy(dest_hbm.at[pl.ds(i * nt, nt)], dest_smem, sem_idx)
    idx_cp.start()
    x = x_ref[...]
    _to_row_tiles(rows_ref, _pack_bf16_pairs(x))
    idx_cp.wait()

    for c in range(nt):
        for r in range(IDX_TILE):
            row = c * IDX_TILE + r
            for j in range(TOPK):
                d = dest_smem[c, j, r]
                pltpu.make_async_copy(rows_ref.at[pl.ds(row * PACK_TILES, PACK_TILES)],
                                      xs_hbm.at[pl.ds(d * PACK_TILES, PACK_TILES)], sem_rows).start(priority=j % 2)

    xb = x.astype(BF16)
    h1 = jnp.dot(xb, sw1_ref[...], preferred_element_type=F32)
    h3 = jnp.dot(xb, sw3_ref[...], preferred_element_type=F32)
    shared = jnp.dot((jax.nn.silu(h1) * h3).astype(BF16), sw2_ref[...], preferred_element_type=F32)
    gate = jax.nn.sigmoid(jnp.dot(xb, gw_ref[...], preferred_element_type=F32) + gb_ref[...])
    ple = gate * jnp.dot(p_ref[...].astype(BF16), pw_ref[...], preferred_element_type=F32)
    sp_ref[...] = shared + ple

    n_copied = TOPK * tm * PACK_TILES
    pltpu.make_async_copy(xs_hbm.at[pl.ds(0, n_copied)], xs_hbm.at[pl.ds(0, n_copied)], sem_rows).wait()


def _dispatch(layer, pend, padded, dest_tiles, x1, p3, sw1, sw3, sw2, pw, gw, gb, n_rows, bm):
    t = x1.shape[0]
    tm = min(DISP_TM, t)
    full = lambda a, b: pl.BlockSpec((a, b), lambda i, pe, pa: (0, 0))
    grid_spec = pltpu.PrefetchScalarGridSpec(
        num_scalar_prefetch=2,
        grid=(t // tm,),
        in_specs=[pl.BlockSpec(memory_space=pl.ANY),
                  pl.BlockSpec((tm, D_MODEL), lambda i, pe, pa: (i, 0)),
                  pl.BlockSpec((None, tm, PLE_DIM), lambda i, pe, pa: (layer, i, 0)),
                  full(D_MODEL, EXPERT_DIM), full(D_MODEL, EXPERT_DIM), full(EXPERT_DIM, D_MODEL),
                  full(PLE_DIM, D_MODEL), full(D_MODEL, D_MODEL), full(1, D_MODEL)],
        out_specs=[pl.BlockSpec(memory_space=pl.ANY),
                   pl.BlockSpec((tm, D_MODEL), lambda i, pe, pa: (i, 0))],
        scratch_shapes=[pltpu.SMEM((tm // IDX_TILE, TOPK, IDX_TILE), jnp.int32),
                        pltpu.VMEM((bm * PACK_TILES, LANES), jnp.uint32),
                        pltpu.VMEM((tm * PACK_TILES, LANES), jnp.uint32),
                        pltpu.SemaphoreType.DMA, pltpu.SemaphoreType.DMA, pltpu.SemaphoreType.DMA],
    )
    return pl.pallas_call(
        _dispatch_kernel,
        grid_spec=grid_spec,
        out_shape=[jax.ShapeDtypeStruct((n_rows * PACK_TILES, LANES), jnp.uint32),
                   jax.ShapeDtypeStruct((t, D_MODEL), F32)],
        compiler_params=_cparams(("arbitrary",)),
        name="dispatch_shared_ple",
    )(pend, padded, dest_tiles, x1, p3, sw1, sw3, sw2, pw, gw, gb)


def _moe_kernel(be_ref, nused_ref, x_ref, w1_ref, w3_ref, w2_ref, y_ref, w1b, w3b, w2b):
    n = pl.program_id(0)

    @pl.when(n < nused_ref[0])
    def _():
        e = be_ref[n]
        e_prev = be_ref[jnp.maximum(n - 1, 0)]

        @pl.when((n == 0) | (e != e_prev))
        def _():
            w1b[...] = w1_ref[...].astype(BF16)
            w3b[...] = w3_ref[...].astype(BF16)
            w2b[...] = w2_ref[...].astype(BF16)

        sub = min(MOE_SUB, x_ref.shape[0] // PACK_TILES)
        for c in range(x_ref.shape[0] // (PACK_TILES * sub)):
            xb = _unpack_bf16_pairs(_from_row_tiles(x_ref, PACK_TILES, base=c * sub * PACK_TILES, rows=sub))
            h1 = jnp.dot(xb, w1b[...], preferred_element_type=F32)
            h3 = jnp.dot(xb, w3b[...], preferred_element_type=F32)
            act = (jax.nn.silu(h1) * h3).astype(BF16)
            y = jnp.dot(act, w2b[...], preferred_element_type=F32)
            _to_row_tiles(y_ref, _pack_bf16_pairs(y), base=c * sub * PACK_TILES)

    @pl.when(n >= nused_ref[0])
    def _():
        y_ref[...] = jnp.zeros_like(y_ref)


def _moe(layer, block_expert, n_used, xs, w1, w3, w2, bm):
    n_blocks = xs.shape[0] // (bm * PACK_TILES)
    wspec = lambda a, b: pl.BlockSpec((None, None, a, b), lambda n, be, nu: (layer, be[n], 0, 0))
    grid_spec = pltpu.PrefetchScalarGridSpec(
        num_scalar_prefetch=2,
        grid=(n_blocks,),
        in_specs=[pl.BlockSpec((bm * PACK_TILES, LANES), lambda n, be, nu: (jnp.minimum(n, nu[0] - 1), 0)),
                  wspec(D_MODEL, EXPERT_DIM), wspec(D_MODEL, EXPERT_DIM), wspec(EXPERT_DIM, D_MODEL)],
        out_specs=pl.BlockSpec((bm * PACK_TILES, LANES), lambda n, be, nu: (n, 0)),
        scratch_shapes=[pltpu.VMEM((D_MODEL, EXPERT_DIM), BF16),
                        pltpu.VMEM((D_MODEL, EXPERT_DIM), BF16),
                        pltpu.VMEM((EXPERT_DIM, D_MODEL), BF16)],
    )
    return pl.pallas_call(
        _moe_kernel,
        grid_spec=grid_spec,
        out_shape=jax.ShapeDtypeStruct((n_blocks * bm * PACK_TILES, LANES), jnp.uint32),
        compiler_params=_cparams(("arbitrary",)),
        name="moe_experts",
    )(block_expert, n_used, xs, w1, w3, w2)


def _final_kernel(dest_hbm, gate_hbm, y_hbm, x_ref, sp_ref, g_ref, b_ref, o_ref,
                  dest_smem, gate_smem, ybuf, routed, sem_idx, sem_rows):
    i = pl.program_id(0)
    n = pl.num_programs(0)
    tm = x_ref.shape[0]
    nbuf = ybuf.shape[0]
    slot = i % nbuf
    ahead = nbuf - 1

    def idx_copies(tile, s):
        return (pltpu.make_async_copy(dest_hbm.at[tile], dest_smem.at[s], sem_idx.at[s]),
                pltpu.make_async_copy(gate_hbm.at[tile], gate_smem.at[s], sem_idx.at[s]))

    def start_idx(tile, s):
        for cp in idx_copies(tile, s):
            cp.start()

    def wait_idx(tile, s):
        for cp in idx_copies(tile, s):
            cp.wait()

    def issue_row(s, r):
        for j in range(TOPK):
            d = dest_smem[s, j, r]
            pltpu.make_async_copy(y_hbm.at[pl.ds(d * PACK_TILES, PACK_TILES)],
                                  ybuf.at[s, j, pl.ds(r * PACK_TILES, PACK_TILES)], sem_rows.at[s]).start(priority=j % 2)

    def issue_rows_loop(s):
        def issue(r, carry):
            issue_row(s, r)
            return carry

        lax.fori_loop(0, tm, issue, 0)

    @pl.when(i == 0)
    def _():
        for k in range(ahead):
            @pl.when(k < n)
            def _(k=k):
                start_idx(k, k)
                wait_idx(k, k)
                issue_rows_loop(k)

        @pl.when(ahead < n)
        def _():
            start_idx(ahead, ahead)

    for j in range(TOPK):
        pltpu.make_async_copy(y_hbm.at[pl.ds(0, tm * PACK_TILES)], ybuf.at[slot, j], sem_rows.at[slot]).wait()

    def combine(r, carry):
        rows = pl.ds(pl.multiple_of(r * PACK_TILES, PACK_TILES), PACK_TILES)
        acc_lo = acc_hi = None
        for j in range(TOPK):
            w = ybuf[slot, j, rows, :]
            g = gate_smem[slot, j, r]
            lo = g * lax.bitcast_convert_type(w << 16, F32)
            hi = g * lax.bitcast_convert_type(w & jnp.uint32(0xFFFF0000), F32)
            acc_lo = lo if acc_lo is None else acc_lo + lo
            acc_hi = hi if acc_hi is None else acc_hi + hi
        routed[0, rows, :] = acc_lo
        routed[1, rows, :] = acc_hi
        return carry

    lax.fori_loop(0, tm, combine, 0)

    @pl.when(i + nbuf < n)
    def _():
        start_idx(i + nbuf, slot)

    def finish():
        routed2d = jnp.concatenate([_from_row_tiles(routed.at[0], PACK_TILES),
                                    _from_row_tiles(routed.at[1], PACK_TILES)], axis=1)
        z = DEEPNORM_ALPHA * x_ref[...] + routed2d + sp_ref[...]
        o_ref[...] = _layer_norm(z, g_ref[...], b_ref[...])

    @pl.when(i + ahead < n)
    def _():
        s_new = (i + ahead) % nbuf
        wait_idx(i + ahead, s_new)
        for r in range(tm):
            issue_row(s_new, r)
        finish()

    @pl.when(i + ahead >= n)
    def _():
        finish()


def _final(dest_tiles, gate_tiles, y_sorted, x1, sp, ln_g, ln_b):
    t = x1.shape[0]
    tm = min(FIN_TM, t)
    full = lambda a, b: pl.BlockSpec((a, b), lambda i: (0, 0))
    rowb = pl.BlockSpec((tm, D_MODEL), lambda i: (i, 0))
    return pl.pallas_call(
        _final_kernel,
        grid=(t // tm,),
        in_specs=[pl.BlockSpec(memory_space=pl.ANY), pl.BlockSpec(memory_space=pl.ANY),
                  pl.BlockSpec(memory_space=pl.ANY), rowb, rowb, full(1, D_MODEL), full(1, D_MODEL)],
        out_specs=rowb,
        out_shape=jax.ShapeDtypeStruct((t, D_MODEL), F32),
        scratch_shapes=[pltpu.SMEM((FIN_BUFS, TOPK, tm), jnp.int32),
                        pltpu.SMEM((FIN_BUFS, TOPK, tm), F32),
                        pltpu.VMEM((FIN_BUFS, TOPK, tm * PACK_TILES, LANES), jnp.uint32),
                        pltpu.VMEM((2, tm * PACK_TILES, LANES), F32),
                        pltpu.SemaphoreType.DMA((FIN_BUFS,)),
                        pltpu.SemaphoreType.DMA((FIN_BUFS,))],
        compiler_params=_cparams(("arbitrary",)),
        name="combine_ln",
    )(dest_tiles, gate_tiles, y_sorted, x1, sp, ln_g, ln_b)


def _dispatch_plan(idx_tiles, rank_tiles, counts, bm, n_blocks):
    padded = ((counts + bm - 1) // bm) * bm
    pend = jnp.cumsum(padded).astype(jnp.int32)
    pstart = pend - padded
    experts = jnp.arange(N_EXPERTS, dtype=jnp.int32)
    start_of = jnp.sum(jnp.where(idx_tiles[..., None] == experts, pstart, 0), axis=-1)
    dest_tiles = (start_of + rank_tiles).astype(jnp.int32)
    block_row = jnp.arange(n_blocks, dtype=jnp.int32) * bm
    block_expert = jnp.minimum(jnp.sum((pend[None, :] <= block_row[:, None]).astype(jnp.int32), axis=1),
                               N_EXPERTS - 1)
    n_used = (pend[-1] // bm).reshape(1)
    return pend, padded.astype(jnp.int32), dest_tiles, block_expert, n_used


def _rotary_lane_tables(positions):
    inv_freq = ROPE_THETA ** (-jnp.arange(0, ROT_DIM, 2, dtype=F32) / ROT_DIM)
    ang = positions.astype(F32).reshape(-1)[:, None] * inv_freq
    cos, sin = jnp.cos(ang), jnp.sin(ang)
    half = ROT_DIM // 2
    t = ang.shape[0]
    one = jnp.ones((t, HEAD_DIM - ROT_DIM), F32)
    zero = jnp.zeros((t, HEAD_DIM - ROT_DIM), F32)
    zh = jnp.zeros((t, half), F32)
    cos64 = jnp.concatenate([cos, cos, one], axis=1)
    sa64 = jnp.concatenate([zh, sin, zero], axis=1)
    sb64 = jnp.concatenate([-sin, zh, zero], axis=1)
    tile2 = lambda m: jnp.concatenate([m, m], axis=1)
    return tile2(cos64), tile2(sa64), tile2(sb64)


def _permute_in_cols(w):
    off = [0]
    for s in IN_SIZES:
        off.append(off[-1] + s)
    aq, ak, av, hq, hf, hi, hg, lx, lg = [w[:, off[k]:off[k + 1]] for k in range(9)]
    parts = []
    for hd in range(HG_HEADS):
        sl = slice(hd * HG_HEAD_DIM, (hd + 1) * HG_HEAD_DIM)
        parts += [hq[:, sl], hf[:, sl], hi[:, sl], hg[:, sl]]
    parts += [aq, lx, lg, ak, av]
    return jnp.concatenate(parts, axis=1)


def _block_diag(w):
    hds, d, _ = w.shape
    eye = jnp.eye(hds, dtype=w.dtype)
    return (eye[:, None, :, None] * w[:, :, None, :]).reshape(hds * d, hds * d)


def kernel(x, p, positions, w_in, w_out, attn_sinks, attn_norm, hg_lb_logits, hg_norm, lru_conv_w, lru_conv_b, lru_wa, lru_ba, lru_wx, lru_bx, lru_lambda, lru_norm, ln1_g, ln1_b, router_w, router_b, exp_w1, exp_w3, exp_w2, sh_w1, sh_w3, sh_w2, ple_w, ple_gate_w, ple_gate_b, ln2_g, ln2_b):
    bsz, seq, _ = x.shape
    t = bsz * seq
    depth = w_in.shape[0]
    lb_sm = jax.nn.softmax(hg_lb_logits.astype(F32), axis=0)
    hg_lb = jnp.maximum(jnp.cumsum(lb_sm, axis=0) - lb_sm[0], 0.0)
    cos_t, sa_t, sb_t = _rotary_lane_tables(positions)
    p3 = p.reshape(depth, t, PLE_DIM)
    bm = min(MOE_BM, t)
    n_blocks = t * TOPK // bm + N_EXPERTS
    row = lambda v: v.reshape(1, -1)

    xc = x.reshape(t, D_MODEL)
    for i in range(depth):
        h = _inproj(xc, _permute_in_cols(w_in[i].astype(BF16)))
        ya = _attention(h, attn_sinks[i], cos_t, sa_t, sb_t, row(attn_norm[i]), bsz, seq)
        yh = _hgrn2(h, row(hg_lb[i]), row(hg_norm[i]), bsz, seq)
        yl = _rglru(h, lru_conv_w[i], row(lru_conv_b[i]), _block_diag(lru_wa[i]).astype(BF16), row(lru_ba[i]),
                    _block_diag(lru_wx[i]).astype(BF16), row(lru_bx[i]), row(lru_lambda[i]), row(lru_norm[i]),
                    bsz, seq)
        x1, idx_tiles, rank_tiles, gate_tiles, counts = _outproj_router(
            ya, yh, yl, xc, w_out[i].astype(BF16), row(ln1_g[i]), row(ln1_b[i]),
            router_w[i].T, router_b[i].reshape(N_EXPERTS, 1))
        pend, padded, dest_tiles, block_expert, n_used = _dispatch_plan(idx_tiles, rank_tiles, counts[:, 0], bm, n_blocks)
        xs, sp = _dispatch(i, pend, padded, dest_tiles, x1, p3, sh_w1[i].astype(BF16), sh_w3[i].astype(BF16),
                           sh_w2[i].astype(BF16), ple_w[i].astype(BF16), ple_gate_w[i].astype(BF16),
                           row(ple_gate_b[i]), n_blocks * bm, bm)
        y_sorted = _moe(i, block_expert, n_used, xs, exp_w1, exp_w3, exp_w2, bm)
        xc = _final(dest_tiles, gate_tiles, y_sorted, x1, sp, row(ln2_g[i]), row(ln2_b[i]))
    return xc.reshape(bsz, seq, D_MODEL)
```

```python
import jax
import jax.numpy as jnp
from jax import lax
from jax.experimental import pallas as pl
from jax.experimental.pallas import tpu as pltpu

F32 = jnp.float32
BF16 = jnp.bfloat16

D_MODEL = 2048
PLE_DIM = 256
N_HEADS = 16
KV_HEADS = 2
Q_PER_KV = N_HEADS // KV_HEADS
HEAD_DIM = 64
ATTN_WIDTH = N_HEADS * HEAD_DIM
KV_WIDTH = KV_HEADS * HEAD_DIM
ATTN_BLOCK = 128
ROT_DIM = HEAD_DIM // 4
ROPE_THETA = 500000.0
HG_HEADS = 4
HG_HEAD_DIM = 128
HG_WIDTH = HG_HEADS * HG_HEAD_DIM
HG_CHUNK = 64
HG_SUB = 8
LRU_HEADS = 8
LRU_HEAD_DIM = 64
LRU_WIDTH = LRU_HEADS * LRU_HEAD_DIM
CONV_WIDTH = 4
LRU_C = 8.0
IN_SIZES = (ATTN_WIDTH, KV_WIDTH, KV_WIDTH, HG_WIDTH, HG_WIDTH, HG_WIDTH, HG_WIDTH, LRU_WIDTH, LRU_WIDTH)
IN_COLS = sum(IN_SIZES)
MIX_WIDTH = ATTN_WIDTH + HG_WIDTH + LRU_WIDTH
N_EXPERTS = 64
TOPK = 8
N_GROUPS = 8
TOPK_GROUPS = 4
EXPERTS_PER_GROUP = N_EXPERTS // N_GROUPS
EXPERT_DIM = 512
ROUTED_SCALE = 2.5
DEPTH = 2
DEEPNORM_ALPHA = (2 * DEPTH) ** 0.25
RMS_EPS = 1e-6
LN_EPS = 1e-5
NEG_BIG = -1e30
NEG_PICKED = -3e38
TINY = 1e-30

LANES = 128
ROW_TILES = D_MODEL // LANES
PACK_TILES = ROW_TILES // 2
VMEM_LIMIT = 56 * 1024 * 1024

COL_HG = 0
COL_Q = COL_HG + 4 * HG_WIDTH
COL_LRU = COL_Q + ATTN_WIDTH
COL_KV = COL_LRU + 2 * LRU_WIDTH

INPROJ_TM = 512
INPROJ_TN = IN_COLS // 2
HG_ROWS = 512
HG_GROUP = 4
LRU_ROWS = 256
OUT_TM = 512
MOE_BM = 512
MOE_SUB = 256
IDX_TILE = 128
DISP_TM = 256
FIN_TM = IDX_TILE
FIN_BUFS = 3


def _cparams(sem):
    return pltpu.CompilerParams(dimension_semantics=sem, vmem_limit_bytes=VMEM_LIMIT)


def _to_row_tiles(ref, val, base=0):
    rows, n = val.shape[0], val.shape[1] // LANES
    for s in range(n):
        ref[pl.ds(base + s, rows, stride=n), :] = val[:, s * LANES:(s + 1) * LANES]


def _from_row_tiles(ref, n, base=0, rows=None):
    rows = ref.shape[0] // n if rows is None else rows
    return jnp.concatenate([ref[pl.ds(base + s, rows, stride=n), :] for s in range(n)], axis=1)


def _pack_bf16_pairs(x):
    c = x.shape[1] // 2
    as_bits = lambda v: lax.bitcast_convert_type(v.astype(BF16).astype(F32), jnp.uint32)
    return (as_bits(x[:, c:]) & jnp.uint32(0xFFFF0000)) | (as_bits(x[:, :c]) >> 16)


def _unpack_bf16_pairs(w):
    lo = lax.bitcast_convert_type(w << 16, F32).astype(BF16)
    hi = lax.bitcast_convert_type(w & jnp.uint32(0xFFFF0000), F32).astype(BF16)
    return jnp.concatenate([lo, hi], axis=1)


def _inproj_kernel(x_ref, w_ref, o_ref):
    o_ref[...] = jnp.dot(x_ref[...].astype(BF16), w_ref[...], preferred_element_type=F32)


def _inproj(x2d, w):
    t = x2d.shape[0]
    tm = min(INPROJ_TM, t)
    return pl.pallas_call(
        _inproj_kernel,
        grid=(IN_COLS // INPROJ_TN, t // tm),
        in_specs=[pl.BlockSpec((tm, D_MODEL), lambda j, i: (i, 0)),
                  pl.BlockSpec((D_MODEL, INPROJ_TN), lambda j, i: (0, j))],
        out_specs=pl.BlockSpec((tm, INPROJ_TN), lambda j, i: (i, j)),
        out_shape=jax.ShapeDtypeStruct((t, IN_COLS), F32),
        compiler_params=_cparams(("arbitrary", "arbitrary")),
        name="inproj",
    )(x2d, w)


def _attn_kernel(sink_ref, q_ref, kv_ref, cos_ref, sa_ref, sb_ref, gain_ref, o_ref, kprev, vprev):
    n = pl.program_id(1)

    @pl.when(n == 0)
    def _():
        kprev[...] = jnp.zeros_like(kprev)
        vprev[...] = jnp.zeros_like(vprev)

    cos = cos_ref[...]
    sa = sa_ref[...]
    sb = sb_ref[...]

    def rot(t):
        return t * cos + pltpu.roll(t, 8, axis=1) * sa + pltpu.roll(t, LANES - 8, axis=1) * sb

    kc = rot(kv_ref[:, 0:KV_WIDTH])
    vc = kv_ref[:, KV_WIDTH:2 * KV_WIDTH]
    kb = jnp.concatenate([kprev[...], kc], axis=0)
    vb = jnp.concatenate([vprev[...], vc], axis=0)
    kb_sw = pltpu.roll(kb, HEAD_DIM, axis=1)
    vb_sw = pltpu.roll(vb, HEAD_DIM, axis=1)
    nkeys = 2 * ATTN_BLOCK
    low = lax.broadcasted_iota(jnp.int32, (nkeys, LANES), 1) < HEAD_DIM
    low_q = lax.broadcasted_iota(jnp.int32, (ATTN_BLOCK, LANES), 1) < HEAD_DIM

    k2, v2 = [], []
    for kvh in range(KV_HEADS):
        k_lo, k_hi = (kb, kb_sw) if kvh == 0 else (kb_sw, kb)
        v_lo, v_hi = (vb, vb_sw) if kvh == 0 else (vb_sw, vb)
        k2.append(jnp.concatenate([jnp.where(low, k_lo, 0.0), jnp.where(low, 0.0, k_hi)], axis=0).astype(BF16))
        top = jnp.concatenate([jnp.where(low, v_lo, 0.0), jnp.where(low, 1.0, 0.0)], axis=1)
        bot = jnp.concatenate([jnp.where(low, 0.0, v_hi), jnp.where(low, 0.0, 1.0)], axis=1)
        v2.append(jnp.concatenate([top, bot], axis=0).astype(BF16))

    row = lax.broadcasted_iota(jnp.int32, (ATTN_BLOCK, 2 * nkeys), 0)
    col = lax.broadcasted_iota(jnp.int32, (ATTN_BLOCK, 2 * nkeys), 1) % nkeys
    rel = row + ATTN_BLOCK - col
    allowed = (rel >= 0) & (rel < ATTN_BLOCK) & ((col >= ATTN_BLOCK) | (n > 0))

    outs = []
    for pair in range(N_HEADS // 2):
        kvh = (2 * pair) // Q_PER_KV
        qt = (rot(q_ref[:, pair * LANES:(pair + 1) * LANES]) * (HEAD_DIM ** -0.5)).astype(BF16)
        s = lax.dot_general(qt, k2[kvh], (((1,), (1,)), ((), ())), preferred_element_type=F32)
        s = jnp.where(allowed, s, NEG_BIG)
        sink_a = sink_ref[2 * pair]
        sink_b = sink_ref[2 * pair + 1]
        m_a = jnp.maximum(jnp.max(s[:, :nkeys], axis=-1, keepdims=True), sink_a)
        m_b = jnp.maximum(jnp.max(s[:, nkeys:], axis=-1, keepdims=True), sink_b)
        pr = jnp.concatenate([jnp.exp(s[:, :nkeys] - m_a), jnp.exp(s[:, nkeys:] - m_b)], axis=1)
        acc = jnp.dot(pr.astype(BF16), v2[kvh], preferred_element_type=F32)
        sink_term = jnp.where(low_q, jnp.exp(sink_a - m_a), jnp.exp(sink_b - m_b))
        outs.append(acc[:, :LANES] / (acc[:, LANES:] + sink_term))
    o = jnp.concatenate(outs, axis=1)
    y = o * lax.rsqrt(jnp.mean(o * o, axis=-1, keepdims=True) + RMS_EPS) * gain_ref[...]
    o_ref[...] = y.astype(o_ref.dtype)
    kprev[...] = kc
    vprev[...] = vc


def _attention(h, sinks, cos_t, sa_t, sb_t, gain, bsz, seq):
    t = bsz * seq
    nb = seq // ATTN_BLOCK
    rowmap = lambda b, n: (b * nb + n, 0)
    return pl.pallas_call(
        _attn_kernel,
        grid=(bsz, nb),
        in_specs=[pl.BlockSpec(memory_space=pltpu.SMEM),
                  pl.BlockSpec((ATTN_BLOCK, ATTN_WIDTH), lambda b, n: (b * nb + n, COL_Q // ATTN_WIDTH)),
                  pl.BlockSpec((ATTN_BLOCK, 2 * KV_WIDTH), lambda b, n: (b * nb + n, COL_KV // (2 * KV_WIDTH))),
                  pl.BlockSpec((ATTN_BLOCK, LANES), rowmap),
                  pl.BlockSpec((ATTN_BLOCK, LANES), rowmap),
                  pl.BlockSpec((ATTN_BLOCK, LANES), rowmap),
                  pl.BlockSpec((1, ATTN_WIDTH), lambda b, n: (0, 0))],
        out_specs=pl.BlockSpec((ATTN_BLOCK, ATTN_WIDTH), rowmap),
        out_shape=jax.ShapeDtypeStruct((t, ATTN_WIDTH), BF16),
        scratch_shapes=[pltpu.VMEM((ATTN_BLOCK, KV_WIDTH), F32), pltpu.VMEM((ATTN_BLOCK, KV_WIDTH), F32)],
        compiler_params=_cparams(("arbitrary", "arbitrary")),
        name="attn",
    )(sinks, h, h, cos_t, sa_t, sb_t, gain)


def _cumsum_rows(x):
    rows = x.shape[0]
    row = lax.broadcasted_iota(jnp.int32, x.shape, 0)
    d = 1
    while d < rows:
        x = x + jnp.where(row >= d, pltpu.roll(x, d, axis=0), 0.0)
        d *= 2
    return x


def _hgrn_kernel(in_ref, lb_ref, gain_ref, o_ref, st_ref):
    r = pl.program_id(2)

    @pl.when(r == 0)
    def _():
        st_ref[...] = jnp.zeros_like(st_ref)

    c = HG_CHUNK
    nsub = c // HG_SUB
    ones_kk = jnp.ones((HG_HEAD_DIM, HG_HEAD_DIM), BF16)
    row_c = lax.broadcasted_iota(jnp.int32, (c, HG_HEAD_DIM), 0)
    row_s = lax.broadcasted_iota(jnp.int32, (HG_SUB, HG_HEAD_DIM), 0)
    nt = (((1,), (1,)), ((), ()))
    tn = (((0,), (0,)), ((), ()))

    def one_head(r0, hh):
        col = hh * 4 * HG_HEAD_DIM
        hcol = slice(hh * HG_HEAD_DIM, (hh + 1) * HG_HEAD_DIM)
        lb = lb_ref[:, hcol]
        gain = gain_ref[:, hcol]
        q = in_ref[pl.ds(r0, c), col:col + HG_HEAD_DIM]
        fp = in_ref[pl.ds(r0, c), col + HG_HEAD_DIM:col + 2 * HG_HEAD_DIM]
        v = in_ref[pl.ds(r0, c), col + 2 * HG_HEAD_DIM:col + 3 * HG_HEAD_DIM]
        g = in_ref[pl.ds(r0, c), col + 3 * HG_HEAD_DIM:col + 4 * HG_HEAD_DIM]
        qf = jax.nn.silu(q)
        f = lb + (1.0 - lb) * jax.nn.sigmoid(fp)
        logf = jnp.log(jnp.maximum(f, TINY))
        kf = (1.0 - lb) * jax.nn.sigmoid(-fp)
        b = _cumsum_rows(logf)
        vb = v.astype(BF16)

        s_rows = [jnp.zeros((HG_SUB, c), F32)]
        for i in range(1, nsub):
            lo = i * HG_SUB
            ref_b = b[lo - 1:lo, :]
            qi = qf[lo:lo + HG_SUB, :] * jnp.exp(b[lo:lo + HG_SUB, :] - ref_b)
            ki = kf * jnp.exp(jnp.where(row_c < lo, ref_b - b, NEG_BIG))
            s_rows.append(lax.dot_general(qi.astype(BF16), ki.astype(BF16), nt, preferred_element_type=F32))
        scores = jnp.concatenate(s_rows, axis=0)
        o = jnp.dot(scores.astype(BF16), vb, preferred_element_type=F32)

        o_diag = []
        for j in range(nsub):
            lo = j * HG_SUB
            bs = b[lo:lo + HG_SUB, :]
            ks = kf[lo:lo + HG_SUB, :]
            qs = qf[lo:lo + HG_SUB, :]
            vs = v[lo:lo + HG_SUB, :]
            tiles = []
            for tt in range(HG_SUB):
                e = jnp.where(row_s <= tt, bs[tt:tt + 1, :] - bs, NEG_BIG)
                tiles.append(jnp.exp(e) * ks * qs[tt:tt + 1, :])
            w = jnp.concatenate(tiles, axis=0).astype(BF16)
            dfull = jnp.dot(w, ones_kk, preferred_element_type=F32)
            contrib = dfull.reshape(HG_SUB, HG_SUB, HG_HEAD_DIM) * vs[None, :, :]
            o_diag.append(jnp.sum(contrib, axis=1))
        o = o + jnp.concatenate(o_diag, axis=0)

        st = st_ref[hh]
        qb = (qf * jnp.exp(b)).astype(BF16)
        o = o + lax.dot_general(qb, st.astype(BF16), nt, preferred_element_type=F32)
        b_last = b[c - 1:c, :]
        kn = (kf * jnp.exp(b_last - b)).astype(BF16)
        st_ref[hh] = st * jnp.exp(b_last) + lax.dot_general(vb, kn, tn, preferred_element_type=F32)

        y = o * lax.rsqrt(jnp.mean(o * o, axis=-1, keepdims=True) + RMS_EPS) * gain
        o_ref[pl.ds(r0, c), hcol] = (y * jax.nn.silu(g)).astype(o_ref.dtype)

    def chunk(ci, carry):
        r0 = pl.multiple_of(ci * c, c)
        for hh in range(st_ref.shape[0]):
            one_head(r0, hh)
        return carry

    lax.fori_loop(0, in_ref.shape[0] // c, chunk, 0)


def _hgrn2(h, lb, gain, bsz, seq):
    t = bsz * seq
    rows = min(HG_ROWS, seq)
    nr = seq // rows
    hd_block0 = COL_HG // (4 * HG_HEAD_DIM)
    grp = HG_GROUP
    return pl.pallas_call(
        _hgrn_kernel,
        grid=(bsz, HG_HEADS // grp, nr),
        in_specs=[pl.BlockSpec((rows, grp * 4 * HG_HEAD_DIM), lambda b, hd, r: (b * nr + r, hd_block0 // grp + hd)),
                  pl.BlockSpec((1, grp * HG_HEAD_DIM), lambda b, hd, r: (0, hd)),
                  pl.BlockSpec((1, grp * HG_HEAD_DIM), lambda b, hd, r: (0, hd))],
        out_specs=pl.BlockSpec((rows, grp * HG_HEAD_DIM), lambda b, hd, r: (b * nr + r, hd)),
        out_shape=jax.ShapeDtypeStruct((t, HG_WIDTH), BF16),
        scratch_shapes=[pltpu.VMEM((grp, HG_HEAD_DIM, HG_HEAD_DIM), F32)],
        compiler_params=_cparams(("arbitrary", "arbitrary", "arbitrary")),
        name="hgrn2",
    )(h, lb, gain)


def _lru_kernel(in_ref, cw_ref, cb_ref, wa_ref, ba_ref, wx_ref, bx_ref, lam_ref, gain_ref, o_ref,
                tail_ref, h_ref):
    r = pl.program_id(1)

    @pl.when(r == 0)
    def _():
        tail_ref[...] = jnp.zeros_like(tail_ref)
        h_ref[...] = jnp.zeros_like(h_ref)

    rows = in_ref.shape[0]
    x = in_ref[:, 0:LRU_WIDTH]
    gr = in_ref[:, LRU_WIDTH:2 * LRU_WIDTH]
    tail = tail_ref[...]
    row8 = lax.broadcasted_iota(jnp.int32, (8, LRU_WIDTH), 0)

    xc = x * cw_ref[CONV_WIDTH - 1:CONV_WIDTH, :] + cb_ref[...]
    for k in range(1, CONV_WIDTH):
        xs = pltpu.roll(x, k, axis=0)
        head = jnp.where(row8 < k, pltpu.roll(tail, k, axis=0), xs[0:8, :])
        xs = jnp.concatenate([head, xs[8:, :]], axis=0)
        xc = xc + xs * cw_ref[CONV_WIDTH - 1 - k:CONV_WIDTH - k, :]
    tail_ref[...] = x[rows - 8:rows, :]

    xcb = xc.astype(BF16)
    rg = jax.nn.sigmoid(jnp.dot(xcb, wa_ref[...], preferred_element_type=F32) + ba_ref[...])
    ig = jax.nn.sigmoid(jnp.dot(xcb, wx_ref[...], preferred_element_type=F32) + bx_ref[...])
    log_a = -LRU_C * rg * jax.nn.softplus(-lam_ref[...])
    a = jnp.exp(log_a)
    th = jnp.tanh(log_a)
    neg_expm1 = -2.0 * th / (1.0 - th)
    u = jnp.sqrt(jnp.maximum(neg_expm1, 0.0)) * (ig * xc)

    row = lax.broadcasted_iota(jnp.int32, (rows, LRU_WIDTH), 0)
    d = 1
    while d < rows:
        keep = row >= d
        a_s = jnp.where(keep, pltpu.roll(a, d, axis=0), 1.0)
        u_s = jnp.where(keep, pltpu.roll(u, d, axis=0), 0.0)
        u = a * u_s + u
        a = a * a_s
        d *= 2
    hcur = u + a * h_ref[0:1, :]
    h_ref[...] = jnp.broadcast_to(hcur[rows - 1:rows, :], h_ref.shape)

    y = hcur * lax.rsqrt(jnp.mean(hcur * hcur, axis=-1, keepdims=True) + RMS_EPS) * gain_ref[...]
    o_ref[...] = (y * jax.nn.gelu(gr)).astype(o_ref.dtype)


def _rglru(h, cw, cb, wa, ba, wx, bx, lam, gain, bsz, seq):
    t = bsz * seq
    rows = min(LRU_ROWS, seq)
    nr = seq // rows
    vec = pl.BlockSpec((1, LRU_WIDTH), lambda b, r: (0, 0))
    mat = pl.BlockSpec((LRU_WIDTH, LRU_WIDTH), lambda b, r: (0, 0))
    return pl.pallas_call(
        _lru_kernel,
        grid=(bsz, nr),
        in_specs=[pl.BlockSpec((rows, 2 * LRU_WIDTH), lambda b, r: (b * nr + r, COL_LRU // (2 * LRU_WIDTH))),
                  pl.BlockSpec((CONV_WIDTH, LRU_WIDTH), lambda b, r: (0, 0)),
                  vec, mat, vec, mat, vec, vec, vec],
        out_specs=pl.BlockSpec((rows, LRU_WIDTH), lambda b, r: (b * nr + r, 0)),
        out_shape=jax.ShapeDtypeStruct((t, LRU_WIDTH), BF16),
        scratch_shapes=[pltpu.VMEM((8, LRU_WIDTH), F32), pltpu.VMEM((8, LRU_WIDTH), F32)],
        compiler_params=_cparams(("arbitrary", "arbitrary")),
        name="rglru",
    )(h, cw, cb, wa, ba, wx, bx, lam, gain)


def _layer_norm(z, g, b):
    mu = jnp.mean(z, axis=-1, keepdims=True)
    zc = z - mu
    var = jnp.mean(zc * zc, axis=-1, keepdims=True)
    return zc * lax.rsqrt(var + LN_EPS) * g + b


def _outproj_kernel(ya_ref, yh_ref, yl_ref, x_ref, wa_ref, wh_ref, wl_ref, g_ref, b_ref, rw_ref, rb_ref,
                    x1_ref, idx_ref, rank_ref, gate_ref, counts_ref, carry_ref):
    i = pl.program_id(0)

    @pl.when(i == 0)
    def _():
        carry_ref[...] = jnp.zeros_like(carry_ref)

    mixed = jnp.dot(ya_ref[...], wa_ref[...], preferred_element_type=F32)
    mixed = mixed + jnp.dot(yh_ref[...], wh_ref[...], preferred_element_type=F32)
    mixed = mixed + jnp.dot(yl_ref[...], wl_ref[...], preferred_element_type=F32)
    x1 = _layer_norm(DEEPNORM_ALPHA * x_ref[...] + mixed, g_ref[...], b_ref[...])
    x1_ref[...] = x1

    tm = x1.shape[0]
    logits = lax.dot_general(rw_ref[...], x1, (((1,), (1,)), ((), ())),
                             precision=lax.Precision.HIGHEST, preferred_element_type=F32)
    scores = jax.nn.sigmoid(logits)
    sel = scores + rb_ref[...]
    shp = (N_GROUPS, EXPERTS_PER_GROUP, tm)
    s3 = sel.reshape(shp)
    sc3 = scores.reshape(shp)
    e_in = lax.broadcasted_iota(jnp.int32, shp, 1)
    g_id = lax.broadcasted_iota(jnp.int32, shp, 0)
    e_id = g_id * EXPERTS_PER_GROUP + e_in

    m1 = jnp.max(s3, axis=1, keepdims=True)
    i1 = jnp.min(jnp.where(s3 == m1, e_in, EXPERTS_PER_GROUP), axis=1, keepdims=True)
    m2 = jnp.max(jnp.where(e_in == i1, NEG_PICKED, s3), axis=1, keepdims=True)
    gs = m1 + m2
    g1 = lax.broadcasted_iota(jnp.int32, gs.shape, 0)
    gsel = jnp.zeros(gs.shape, jnp.int32)
    cur = gs
    for _ in range(TOPK_GROUPS):
        m = jnp.max(cur, axis=0, keepdims=True)
        pick = g1 == jnp.min(jnp.where(cur == m, g1, N_GROUPS), axis=0, keepdims=True)
        gsel = jnp.where(pick, 1, gsel)
        cur = jnp.where(pick, NEG_PICKED, cur)
    cur = jnp.where(gsel > 0, s3, NEG_BIG)

    def pick_sum(pick, vals):
        return jnp.sum(jnp.sum(jnp.where(pick, vals, 0.0), axis=1, keepdims=True), axis=0, keepdims=True).reshape(1, tm)

    idx_rows, w_rows, picks = [], [], []
    onehot = jnp.zeros(shp, F32)
    for _ in range(TOPK):
        m = jnp.max(jnp.max(cur, axis=1, keepdims=True), axis=0, keepdims=True)
        cand = jnp.where(cur == m, e_id, N_EXPERTS)
        ii = jnp.min(jnp.min(cand, axis=1, keepdims=True), axis=0, keepdims=True)
        pick = e_id == ii
        picks.append(pick)
        onehot = jnp.where(pick, 1.0, onehot)
        w_rows.append(pick_sum(pick, sc3))
        idx_rows.append(ii.reshape(1, tm))
        cur = jnp.where(pick, NEG_PICKED, cur)
    w = jnp.concatenate(w_rows, axis=0)
    idx = jnp.concatenate(idx_rows, axis=0)
    gates = w / jnp.sum(w, axis=0, keepdims=True) * ROUTED_SCALE

    oh = onehot.reshape(N_EXPERTS, tm).astype(BF16)
    r_i = lax.broadcasted_iota(jnp.int32, (tm, tm), 0)
    c_i = lax.broadcasted_iota(jnp.int32, (tm, tm), 1)
    before = jnp.where(r_i < c_i, 1.0, 0.0).astype(BF16)
    carry = carry_ref[...]
    prefix3 = (jnp.dot(oh, before, preferred_element_type=F32) + carry).reshape(shp)
    rank = jnp.concatenate([pick_sum(pk, prefix3) for pk in picks], axis=0).astype(jnp.int32)
    carry = carry + jnp.dot(oh, jnp.ones((tm, tm), BF16), preferred_element_type=F32)
    carry_ref[...] = carry
    counts_ref[...] = carry.astype(jnp.int32)

    for c in range(tm // IDX_TILE):
        sl = slice(c * IDX_TILE, (c + 1) * IDX_TILE)
        idx_ref[c] = idx[:, sl]
        rank_ref[c] = rank[:, sl]
        gate_ref[c] = gates[:, sl]


def _outproj_router(ya, yh, yl, x2d, w_out, ln_g, ln_b, rw_t, rb):
    t = x2d.shape[0]
    tm = min(OUT_TM, t)
    nt = tm // IDX_TILE
    rowb = lambda w: pl.BlockSpec((tm, w), lambda i: (i, 0))
    full = lambda a, b: pl.BlockSpec((a, b), lambda i: (0, 0))
    tiles = pl.BlockSpec((nt, TOPK, IDX_TILE), lambda i: (i, 0, 0))
    wa = pl.BlockSpec((ATTN_WIDTH, D_MODEL), lambda i: (0, 0))
    wh = pl.BlockSpec((HG_WIDTH, D_MODEL), lambda i: (ATTN_WIDTH // HG_WIDTH, 0))
    wl = pl.BlockSpec((LRU_WIDTH, D_MODEL), lambda i: ((ATTN_WIDTH + HG_WIDTH) // LRU_WIDTH, 0))
    tile_shape = lambda dt: jax.ShapeDtypeStruct((t // IDX_TILE, TOPK, IDX_TILE), dt)
    return pl.pallas_call(
        _outproj_kernel,
        grid=(t // tm,),
        in_specs=[rowb(ATTN_WIDTH), rowb(HG_WIDTH), rowb(LRU_WIDTH), rowb(D_MODEL), wa, wh, wl,
                  full(1, D_MODEL), full(1, D_MODEL), full(N_EXPERTS, D_MODEL), full(N_EXPERTS, 1)],
        out_specs=[rowb(D_MODEL), tiles, tiles, tiles, full(N_EXPERTS, tm)],
        out_shape=[jax.ShapeDtypeStruct((t, D_MODEL), F32),
                   tile_shape(jnp.int32), tile_shape(jnp.int32), tile_shape(F32),
                   jax.ShapeDtypeStruct((N_EXPERTS, tm), jnp.int32)],
        scratch_shapes=[pltpu.VMEM((N_EXPERTS, tm), F32)],
        compiler_params=_cparams(("arbitrary",)),
        name="outproj_router",
    )(ya, yh, yl, x2d, w_out, w_out, w_out, ln_g, ln_b, rw_t, rb)


def _dispatch_kernel(pend_ref, padded_ref, dest_hbm, x_ref, p_ref, sw1_ref, sw3_ref, sw2_ref, pw_ref,
                     gw_ref, gb_ref, xs_hbm, sp_ref, dest_smem, zeros_ref, rows_ref, sem_idx, sem_rows, sem_zero):
    i = pl.program_id(0)
    tm = x_ref.shape[0]
    bm = zeros_ref.shape[0] // PACK_TILES
    nt = tm // IDX_TILE

    @pl.when(i == 0)
    def _():
        zeros_ref[...] = jnp.zeros_like(zeros_ref)

        def fill(e, carry):
            @pl.when(padded_ref[e] > 0)
            def _():
                start = pl.multiple_of(pend_ref[e] - bm, bm)
                pltpu.make_async_copy(zeros_ref, xs_hbm.at[pl.ds(start * PACK_TILES, bm * PACK_TILES)], sem_zero).start()
            return carry

        def fill_wait(e, carry):
            @pl.when(padded_ref[e] > 0)
            def _():
                pltpu.make_async_copy(zeros_ref, xs_hbm.at[pl.ds(0, bm * PACK_TILES)], sem_zero).wait()
            return carry

        def fill_tail(blk, carry):
            start = pl.multiple_of(blk * bm, bm)
            pltpu.make_async_copy(zeros_ref, xs_hbm.at[pl.ds(start * PACK_TILES, bm * PACK_TILES)], sem_zero).start()
            return carry

        def fill_tail_wait(blk, carry):
            pltpu.make_async_copy(zeros_ref, xs_hbm.at[pl.ds(0, bm * PACK_TILES)], sem_zero).wait()
            return carry

        n_used = pend_ref[N_EXPERTS - 1] // bm
        n_blocks = xs_hbm.shape[0] // (bm * PACK_TILES)
        lax.fori_loop(0, N_EXPERTS, fill, 0)
        lax.fori_loop(n_used, n_blocks, fill_tail, 0)
        lax.fori_loop(0, N_EXPERTS, fill_wait, 0)
        lax.fori_loop(n_used, n_blocks, fill_tail_wait, 0)

    idx_cp = pltpu.make_async_copy(dest_hbm.at[pl.ds(i * nt, nt)], dest_smem, sem_idx)
    idx_cp.start()
    x = x_ref[...]
    _to_row_tiles(rows_ref, _pack_bf16_pairs(x))
    idx_cp.wait()

    for c in range(nt):
        for r in range(IDX_TILE):
            row = c * IDX_TILE + r
            for j in range(TOPK):
                d = dest_smem[c, j, r]
                pltpu.make_async_copy(rows_ref.at[pl.ds(row * PACK_TILES, PACK_TILES)],
                                      xs_hbm.at[pl.ds(d * PACK_TILES, PACK_TILES)], sem_rows).start(priority=j % 2)

    xb = x.astype(BF16)
    h1 = jnp.dot(xb, sw1_ref[...], preferred_element_type=F32)
    h3 = jnp.dot(xb, sw3_ref[...], preferred_element_type=F32)
    shared = jnp.dot((jax.nn.silu(h1) * h3).astype(BF16), sw2_ref[...], preferred_element_type=F32)
    gate = jax.nn.sigmoid(jnp.dot(xb, gw_ref[...], preferred_element_type=F32) + gb_ref[...])
    ple = gate * jnp.dot(p_ref[...].astype(BF16), pw_ref[...], preferred_element_type=F32)
    sp_ref[...] = shared + ple

    n_copied = TOPK * tm * PACK_TILES
    pltpu.make_async_copy(xs_hbm.at[pl.ds(0, n_copied)], xs_hbm.at[pl.ds(0, n_copied)], sem_rows).wait()


def _dispatch(layer, pend, padded, dest_tiles, x1, p3, sw1, sw3, sw2, pw, gw, gb, n_rows, bm):
    t = x1.shape[0]
    tm = min(DISP_TM, t)
    full = lambda a, b: pl.BlockSpec((a, b), lambda i, pe, pa: (0, 0))
    grid_spec = pltpu.PrefetchScalarGridSpec(
        num_scalar_prefetch=2,
        grid=(t // tm,),
        in_specs=[pl.BlockSpec(memory_space=pl.ANY),
                  pl.BlockSpec((tm, D_MODEL), lambda i, pe, pa: (i, 0)),
                  pl.BlockSpec((None, tm, PLE_DIM), lambda i, pe, pa: (layer, i, 0)),
                  full(D_MODEL, EXPERT_DIM), full(D_MODEL, EXPERT_DIM), full(EXPERT_DIM, D_MODEL),
                  full(PLE_DIM, D_MODEL), full(D_MODEL, D_MODEL), full(1, D_MODEL)],
        out_specs=[pl.BlockSpec(memory_space=pl.ANY),
                   pl.BlockSpec((tm, D_MODEL), lambda i, pe, pa: (i, 0))],
        scratch_shapes=[pltpu.SMEM((tm // IDX_TILE, TOPK, IDX_TILE), jnp.int32),
                        pltpu.VMEM((bm * PACK_TILES, LANES), jnp.uint32),
                        pltpu.VMEM((tm * PACK_TILES, LANES), jnp.uint32),
                        pltpu.SemaphoreType.DMA, pltpu.SemaphoreType.DMA, pltpu.SemaphoreType.DMA],
    )
    return pl.pallas_call(
        _dispatch_kernel,
        grid_spec=grid_spec,
        out_shape=[jax.ShapeDtypeStruct((n_rows * PACK_TILES, LANES), jnp.uint32),
                   jax.ShapeDtypeStruct((t, D_MODEL), F32)],
        compiler_params=_cparams(("arbitrary",)),
        name="dispatch_shared_ple",
    )(pend, padded, dest_tiles, x1, p3, sw1, sw3, sw2, pw, gw, gb)


def _moe_kernel(be_ref, nused_ref, x_ref, w1_ref, w3_ref, w2_ref, y_ref, w1b, w3b, w2b):
    n = pl.program_id(0)

    @pl.when(n < nused_ref[0])
    def _():
        e = be_ref[n]
        e_prev = be_ref[jnp.maximum(n - 1, 0)]

        @pl.when((n == 0) | (e != e_prev))
        def _():
            w1b[...] = w1_ref[...].astype(BF16)
            w3b[...] = w3_ref[...].astype(BF16)
            w2b[...] = w2_ref[...].astype(BF16)

        sub = min(MOE_SUB, x_ref.shape[0] // PACK_TILES)
        for c in range(x_ref.shape[0] // (PACK_TILES * sub)):
            xb = _unpack_bf16_pairs(_from_row_tiles(x_ref, PACK_TILES, base=c * sub * PACK_TILES, rows=sub))
            h1 = jnp.dot(xb, w1b[...], preferred_element_type=F32)
            h3 = jnp.dot(xb, w3b[...], preferred_element_type=F32)
            act = (jax.nn.silu(h1) * h3).astype(BF16)
            y = jnp.dot(act, w2b[...], preferred_element_type=F32)
            _to_row_tiles(y_ref, _pack_bf16_pairs(y), base=c * sub * PACK_TILES)

    @pl.when(n >= nused_ref[0])
    def _():
        y_ref[...] = jnp.zeros_like(y_ref)


def _moe(layer, block_expert, n_used, xs, w1, w3, w2, bm):
    n_blocks = xs.shape[0] // (bm * PACK_TILES)
    wspec = lambda a, b: pl.BlockSpec((None, None, a, b), lambda n, be, nu: (layer, be[n], 0, 0))
    grid_spec = pltpu.PrefetchScalarGridSpec(
        num_scalar_prefetch=2,
        grid=(n_blocks,),
        in_specs=[pl.BlockSpec((bm * PACK_TILES, LANES), lambda n, be, nu: (jnp.minimum(n, nu[0] - 1), 0)),
                  wspec(D_MODEL, EXPERT_DIM), wspec(D_MODEL, EXPERT_DIM), wspec(EXPERT_DIM, D_MODEL)],
        out_specs=pl.BlockSpec((bm * PACK_TILES, LANES), lambda n, be, nu: (n, 0)),
        scratch_shapes=[pltpu.VMEM((D_MODEL, EXPERT_DIM), BF16),
                        pltpu.VMEM((D_MODEL, EXPERT_DIM), BF16),
                        pltpu.VMEM((EXPERT_DIM, D_MODEL), BF16)],
    )
    return pl.pallas_call(
        _moe_kernel,
        grid_spec=grid_spec,
        out_shape=jax.ShapeDtypeStruct((n_blocks * bm * PACK_TILES, LANES), jnp.uint32),
        compiler_params=_cparams(("arbitrary",)),
        name="moe_experts",
    )(block_expert, n_used, xs, w1, w3, w2)


def _final_kernel(dest_hbm, gate_hbm, y_hbm, x_ref, sp_ref, g_ref, b_ref, o_ref,
                  dest_smem, gate_smem, ybuf, routed, sem_idx, sem_rows):
    i = pl.program_id(0)
    n = pl.num_programs(0)
    tm = x_ref.shape[0]
    nbuf = ybuf.shape[0]
    ahead = nbuf - 1

    def idx_copies(tile, s):
        return (pltpu.make_async_copy(dest_hbm.at[tile], dest_smem.at[s], sem_idx.at[s]),
                pltpu.make_async_copy(gate_hbm.at[tile], gate_smem.at[s], sem_idx.at[s]))

    def start_idx(tile, s):
        for cp in idx_copies(tile, s):
            cp.start()

    def wait_idx(tile, s):
        for cp in idx_copies(tile, s):
            cp.wait()

    def issue_row(s, r):
        for j in range(TOPK):
            d = dest_smem[s, j, r]
            pltpu.make_async_copy(y_hbm.at[pl.ds(d * PACK_TILES, PACK_TILES)],
                                  ybuf.at[s, j, pl.ds(r * PACK_TILES, PACK_TILES)], sem_rows.at[s]).start(priority=j % 2)

    def issue_rows_loop(s):
        def issue(r, carry):
            issue_row(s, r)
            return carry

        lax.fori_loop(0, tm, issue, 0)

    @pl.when(i == 0)
    def _():
        for k in range(ahead):
            @pl.when(k < n)
            def _(k=k):
                start_idx(k, k)
                wait_idx(k, k)
                issue_rows_loop(k)

        @pl.when(ahead < n)
        def _():
            start_idx(ahead, ahead)

    def finish():
        routed2d = jnp.concatenate([_from_row_tiles(routed.at[0], PACK_TILES),
                                    _from_row_tiles(routed.at[1], PACK_TILES)], axis=1)
        z = DEEPNORM_ALPHA * x_ref[...] + routed2d + sp_ref[...]
        o_ref[...] = _layer_norm(z, g_ref[...], b_ref[...])

    def step(slot):
        for j in range(TOPK):
            pltpu.make_async_copy(y_hbm.at[pl.ds(0, tm * PACK_TILES)], ybuf.at[slot, j], sem_rows.at[slot]).wait()

        def combine(r, carry):
            rows = pl.ds(pl.multiple_of(r * PACK_TILES, PACK_TILES), PACK_TILES)
            acc_lo = acc_hi = None
            for j in range(TOPK):
                w = ybuf[slot, j, rows, :]
                g = gate_smem[slot, j, r]
                lo = g * lax.bitcast_convert_type(w << 16, F32)
                hi = g * lax.bitcast_convert_type(w & jnp.uint32(0xFFFF0000), F32)
                acc_lo = lo if acc_lo is None else acc_lo + lo
                acc_hi = hi if acc_hi is None else acc_hi + hi
            routed[0, rows, :] = acc_lo
            routed[1, rows, :] = acc_hi
            return carry

        lax.fori_loop(0, tm, combine, 0)

        @pl.when(i + nbuf < n)
        def _():
            start_idx(i + nbuf, slot)

        @pl.when(i + ahead < n)
        def _():
            s_new = (slot + ahead) % nbuf
            wait_idx(i + ahead, s_new)
            for r in range(tm):
                issue_row(s_new, r)
            finish()

        @pl.when(i + ahead >= n)
        def _():
            finish()

    for k in range(nbuf):
        pl.when(i % nbuf == k)(lambda k=k: step(k))


def _final(dest_tiles, gate_tiles, y_sorted, x1, sp, ln_g, ln_b):
    t = x1.shape[0]
    tm = min(FIN_TM, t)
    full = lambda a, b: pl.BlockSpec((a, b), lambda i: (0, 0))
    rowb = pl.BlockSpec((tm, D_MODEL), lambda i: (i, 0))
    return pl.pallas_call(
        _final_kernel,
        grid=(t // tm,),
        in_specs=[pl.BlockSpec(memory_space=pl.ANY), pl.BlockSpec(memory_space=pl.ANY),
                  pl.BlockSpec(memory_space=pl.ANY), rowb, rowb, full(1, D_MODEL), full(1, D_MODEL)],
        out_specs=rowb,
        out_shape=jax.ShapeDtypeStruct((t, D_MODEL), F32),
        scratch_shapes=[pltpu.SMEM((FIN_BUFS, TOPK, tm), jnp.int32),
                        pltpu.SMEM((FIN_BUFS, TOPK, tm), F32),
                        pltpu.VMEM((FIN_BUFS, TOPK, tm * PACK_TILES, LANES), jnp.uint32),
                        pltpu.VMEM((2, tm * PACK_TILES, LANES), F32),
                        pltpu.SemaphoreType.DMA((FIN_BUFS,)),
                        pltpu.SemaphoreType.DMA((FIN_BUFS,))],
        compiler_params=_cparams(("arbitrary",)),
        name="combine_ln",
    )(dest_tiles, gate_tiles, y_sorted, x1, sp, ln_g, ln_b)


def _dispatch_plan(idx_tiles, rank_tiles, counts, bm, n_blocks):
    padded = ((counts + bm - 1) // bm) * bm
    pend = jnp.cumsum(padded).astype(jnp.int32)
    pstart = pend - padded
    experts = jnp.arange(N_EXPERTS, dtype=jnp.int32)
    start_of = jnp.sum(jnp.where(idx_tiles[..., None] == experts, pstart, 0), axis=-1)
    dest_tiles = (start_of + rank_tiles).astype(jnp.int32)
    block_row = jnp.arange(n_blocks, dtype=jnp.int32) * bm
    block_expert = jnp.minimum(jnp.sum((pend[None, :] <= block_row[:, None]).astype(jnp.int32), axis=1),
                               N_EXPERTS - 1)
    n_used = (pend[-1] // bm).reshape(1)
    return pend, padded.astype(jnp.int32), dest_tiles, block_expert, n_used


def _rotary_lane_tables(positions):
    inv_freq = ROPE_THETA ** (-jnp.arange(0, ROT_DIM, 2, dtype=F32) / ROT_DIM)
    ang = positions.astype(F32).reshape(-1)[:, None] * inv_freq
    cos, sin = jnp.cos(ang), jnp.sin(ang)
    half = ROT_DIM // 2
    t = ang.shape[0]
    one = jnp.ones((t, HEAD_DIM - ROT_DIM), F32)
    zero = jnp.zeros((t, HEAD_DIM - ROT_DIM), F32)
    zh = jnp.zeros((t, half), F32)
    cos64 = jnp.concatenate([cos, cos, one], axis=1)
    sa64 = jnp.concatenate([zh, sin, zero], axis=1)
    sb64 = jnp.concatenate([-sin, zh, zero], axis=1)
    tile2 = lambda m: jnp.concatenate([m, m], axis=1)
    return tile2(cos64), tile2(sa64), tile2(sb64)


def _permute_in_cols(w):
    off = [0]
    for s in IN_SIZES:
        off.append(off[-1] + s)
    aq, ak, av, hq, hf, hi, hg, lx, lg = [w[:, off[k]:off[k + 1]] for k in range(9)]
    parts = []
    for hd in range(HG_HEADS):
        sl = slice(hd * HG_HEAD_DIM, (hd + 1) * HG_HEAD_DIM)
        parts += [hq[:, sl], hf[:, sl], hi[:, sl], hg[:, sl]]
    parts += [aq, lx, lg, ak, av]
    return jnp.concatenate(parts, axis=1)


def _block_diag(w):
    hds, d, _ = w.shape
    eye = jnp.eye(hds, dtype=w.dtype)
    return (eye[:, None, :, None] * w[:, :, None, :]).reshape(hds * d, hds * d)


def kernel(x, p, positions, w_in, w_out, attn_sinks, attn_norm, hg_lb_logits, hg_norm, lru_conv_w, lru_conv_b, lru_wa, lru_ba, lru_wx, lru_bx, lru_lambda, lru_norm, ln1_g, ln1_b, router_w, router_b, exp_w1, exp_w3, exp_w2, sh_w1, sh_w3, sh_w2, ple_w, ple_gate_w, ple_gate_b, ln2_g, ln2_b):
    bsz, seq, _ = x.shape
    t = bsz * seq
    depth = w_in.shape[0]
    lb_sm = jax.nn.softmax(hg_lb_logits.astype(F32), axis=0)
    hg_lb = jnp.maximum(jnp.cumsum(lb_sm, axis=0) - lb_sm[0], 0.0)
    cos_t, sa_t, sb_t = _rotary_lane_tables(positions)
    p3 = p.reshape(depth, t, PLE_DIM)
    bm = min(MOE_BM, t)
    n_blocks = t * TOPK // bm + N_EXPERTS
    row = lambda v: v.reshape(1, -1)

    xc = x.reshape(t, D_MODEL)
    for i in range(depth):
        h = _inproj(xc, _permute_in_cols(w_in[i].astype(BF16)))
        ya = _attention(h, attn_sinks[i], cos_t, sa_t, sb_t, row(attn_norm[i]), bsz, seq)
        yh = _hgrn2(h, row(hg_lb[i]), row(hg_norm[i]), bsz, seq)
        yl = _rglru(h, lru_conv_w[i], row(lru_conv_b[i]), _block_diag(lru_wa[i]).astype(BF16), row(lru_ba[i]),
                    _block_diag(lru_wx[i]).astype(BF16), row(lru_bx[i]), row(lru_lambda[i]), row(lru_norm[i]),
                    bsz, seq)
        x1, idx_tiles, rank_tiles, gate_tiles, counts = _outproj_router(
            ya, yh, yl, xc, w_out[i].astype(BF16), row(ln1_g[i]), row(ln1_b[i]),
            router_w[i].T, router_b[i].reshape(N_EXPERTS, 1))
        pend, padded, dest_tiles, block_expert, n_used = _dispatch_plan(idx_tiles, rank_tiles, counts[:, 0], bm, n_blocks)
        xs, sp = _dispatch(i, pend, padded, dest_tiles, x1, p3, sh_w1[i].astype(BF16), sh_w3[i].astype(BF16),
                           sh_w2[i].astype(BF16), ple_w[i].astype(BF16), ple_gate_w[i].astype(BF16),
                           row(ple_gate_b[i]), n_blocks * bm, bm)
        y_sorted = _moe(i, block_expert, n_used, xs, exp_w1, exp_w3, exp_w2, bm)
        xc = _final(dest_tiles, gate_tiles, y_sorted, x1, sp, row(ln2_g[i]), row(ln2_b[i]))
    return xc.reshape(bsz, seq, D_MODEL)
```

```python
import jax
import jax.numpy as jnp
from jax import lax
from jax.experimental import pallas as pl
from jax.experimental.pallas import tpu as pltpu

F32 = jnp.float32
BF16 = jnp.bfloat16

D_MODEL = 2048
PLE_DIM = 256
N_HEADS = 16
KV_HEADS = 2
Q_PER_KV = N_HEADS // KV_HEADS
HEAD_DIM = 64
ATTN_WIDTH = N_HEADS * HEAD_DIM
KV_WIDTH = KV_HEADS * HEAD_DIM
ATTN_BLOCK = 128
ROT_DIM = HEAD_DIM // 4
ROPE_THETA = 500000.0
HG_HEADS = 4
HG_HEAD_DIM = 128
HG_WIDTH = HG_HEADS * HG_HEAD_DIM
HG_CHUNK = 64
HG_SUB = 8
LRU_HEADS = 8
LRU_HEAD_DIM = 64
LRU_WIDTH = LRU_HEADS * LRU_HEAD_DIM
CONV_WIDTH = 4
LRU_C = 8.0
IN_SIZES = (ATTN_WIDTH, KV_WIDTH, KV_WIDTH, HG_WIDTH, HG_WIDTH, HG_WIDTH, HG_WIDTH, LRU_WIDTH, LRU_WIDTH)
IN_COLS = sum(IN_SIZES)
MIX_WIDTH = ATTN_WIDTH + HG_WIDTH + LRU_WIDTH
N_EXPERTS = 64
TOPK = 8
N_GROUPS = 8
TOPK_GROUPS = 4
EXPERTS_PER_GROUP = N_EXPERTS // N_GROUPS
EXPERT_DIM = 512
ROUTED_SCALE = 2.5
DEPTH = 2
DEEPNORM_ALPHA = (2 * DEPTH) ** 0.25
RMS_EPS = 1e-6
LN_EPS = 1e-5
NEG_BIG = -1e30
NEG_PICKED = -3e38
TINY = 1e-30

LANES = 128
ROW_TILES = D_MODEL // LANES
PACK_TILES = ROW_TILES // 2
VMEM_LIMIT = 56 * 1024 * 1024

COL_HG = 0
COL_Q = COL_HG + 4 * HG_WIDTH
COL_LRU = COL_Q + ATTN_WIDTH
COL_KV = COL_LRU + 2 * LRU_WIDTH

INPROJ_TM = 512
INPROJ_TN = IN_COLS // 2
HG_ROWS = 512
HG_GROUP = 4
LRU_ROWS = 256
OUT_TM = 512
MOE_BM = 512
MOE_SUB = 512
IDX_TILE = 128
DISP_TM = 256
FIN_TM = IDX_TILE
FIN_BUFS = 3


def _cparams(sem):
    return pltpu.CompilerParams(dimension_semantics=sem, vmem_limit_bytes=VMEM_LIMIT)


def _to_row_tiles(ref, val, base=0):
    rows, n = val.shape[0], val.shape[1] // LANES
    for s in range(n):
        ref[pl.ds(base + s, rows, stride=n), :] = val[:, s * LANES:(s + 1) * LANES]


def _from_row_tiles(ref, n, base=0, rows=None):
    rows = ref.shape[0] // n if rows is None else rows
    return jnp.concatenate([ref[pl.ds(base + s, rows, stride=n), :] for s in range(n)], axis=1)


def _pack_bf16_pairs(x):
    c = x.shape[1] // 2
    as_bits = lambda v: lax.bitcast_convert_type(v.astype(BF16).astype(F32), jnp.uint32)
    return (as_bits(x[:, c:]) & jnp.uint32(0xFFFF0000)) | (as_bits(x[:, :c]) >> 16)


def _unpack_bf16_pairs(w):
    lo = lax.bitcast_convert_type(w << 16, F32).astype(BF16)
    hi = lax.bitcast_convert_type(w & jnp.uint32(0xFFFF0000), F32).astype(BF16)
    return jnp.concatenate([lo, hi], axis=1)


def _inproj_kernel(x_ref, w_ref, o_ref):
    o_ref[...] = jnp.dot(x_ref[...].astype(BF16), w_ref[...], preferred_element_type=F32)


def _inproj(x2d, w):
    t = x2d.shape[0]
    tm = min(INPROJ_TM, t)
    return pl.pallas_call(
        _inproj_kernel,
        grid=(IN_COLS // INPROJ_TN, t // tm),
        in_specs=[pl.BlockSpec((tm, D_MODEL), lambda j, i: (i, 0)),
                  pl.BlockSpec((D_MODEL, INPROJ_TN), lambda j, i: (0, j))],
        out_specs=pl.BlockSpec((tm, INPROJ_TN), lambda j, i: (i, j)),
        out_shape=jax.ShapeDtypeStruct((t, IN_COLS), F32),
        compiler_params=_cparams(("arbitrary", "arbitrary")),
        name="inproj",
    )(x2d, w)


def _attn_kernel(sink_ref, q_ref, kv_ref, cos_ref, sa_ref, sb_ref, gain_ref, o_ref, kprev, vprev):
    n = pl.program_id(1)

    @pl.when(n == 0)
    def _():
        kprev[...] = jnp.zeros_like(kprev)
        vprev[...] = jnp.zeros_like(vprev)

    cos = cos_ref[...]
    sa = sa_ref[...]
    sb = sb_ref[...]

    def rot(t):
        return t * cos + pltpu.roll(t, 8, axis=1) * sa + pltpu.roll(t, LANES - 8, axis=1) * sb

    kc = rot(kv_ref[:, 0:KV_WIDTH])
    vc = kv_ref[:, KV_WIDTH:2 * KV_WIDTH]
    kb = jnp.concatenate([kprev[...], kc], axis=0)
    vb = jnp.concatenate([vprev[...], vc], axis=0)
    kb_sw = pltpu.roll(kb, HEAD_DIM, axis=1)
    vb_sw = pltpu.roll(vb, HEAD_DIM, axis=1)
    nkeys = 2 * ATTN_BLOCK
    low = lax.broadcasted_iota(jnp.int32, (nkeys, LANES), 1) < HEAD_DIM
    low_q = lax.broadcasted_iota(jnp.int32, (ATTN_BLOCK, LANES), 1) < HEAD_DIM

    k2, v2 = [], []
    for kvh in range(KV_HEADS):
        k_lo, k_hi = (kb, kb_sw) if kvh == 0 else (kb_sw, kb)
        v_lo, v_hi = (vb, vb_sw) if kvh == 0 else (vb_sw, vb)
        k2.append(jnp.concatenate([jnp.where(low, k_lo, 0.0), jnp.where(low, 0.0, k_hi)], axis=0).astype(BF16))
        top = jnp.concatenate([jnp.where(low, v_lo, 0.0), jnp.where(low, 1.0, 0.0)], axis=1)
        bot = jnp.concatenate([jnp.where(low, 0.0, v_hi), jnp.where(low, 0.0, 1.0)], axis=1)
        v2.append(jnp.concatenate([top, bot], axis=0).astype(BF16))

    row = lax.broadcasted_iota(jnp.int32, (ATTN_BLOCK, 2 * nkeys), 0)
    col = lax.broadcasted_iota(jnp.int32, (ATTN_BLOCK, 2 * nkeys), 1) % nkeys
    rel = row + ATTN_BLOCK - col
    allowed = (rel >= 0) & (rel < ATTN_BLOCK) & ((col >= ATTN_BLOCK) | (n > 0))

    outs = []
    for pair in range(N_HEADS // 2):
        kvh = (2 * pair) // Q_PER_KV
        qt = (rot(q_ref[:, pair * LANES:(pair + 1) * LANES]) * (HEAD_DIM ** -0.5)).astype(BF16)
        s = lax.dot_general(qt, k2[kvh], (((1,), (1,)), ((), ())), preferred_element_type=F32)
        s = jnp.where(allowed, s, NEG_BIG)
        sink_a = sink_ref[2 * pair]
        sink_b = sink_ref[2 * pair + 1]
        m_a = jnp.maximum(jnp.max(s[:, :nkeys], axis=-1, keepdims=True), sink_a)
        m_b = jnp.maximum(jnp.max(s[:, nkeys:], axis=-1, keepdims=True), sink_b)
        pr = jnp.concatenate([jnp.exp(s[:, :nkeys] - m_a), jnp.exp(s[:, nkeys:] - m_b)], axis=1)
        acc = jnp.dot(pr.astype(BF16), v2[kvh], preferred_element_type=F32)
        sink_term = jnp.where(low_q, jnp.exp(sink_a - m_a), jnp.exp(sink_b - m_b))
        outs.append(acc[:, :LANES] / (acc[:, LANES:] + sink_term))
    o = jnp.concatenate(outs, axis=1)
    y = o * lax.rsqrt(jnp.mean(o * o, axis=-1, keepdims=True) + RMS_EPS) * gain_ref[...]
    o_ref[...] = y.astype(o_ref.dtype)
    kprev[...] = kc
    vprev[...] = vc


def _attention(h, sinks, cos_t, sa_t, sb_t, gain, bsz, seq):
    t = bsz * seq
    nb = seq // ATTN_BLOCK
    rowmap = lambda b, n: (b * nb + n, 0)
    return pl.pallas_call(
        _attn_kernel,
        grid=(bsz, nb),
        in_specs=[pl.BlockSpec(memory_space=pltpu.SMEM),
                  pl.BlockSpec((ATTN_BLOCK, ATTN_WIDTH), lambda b, n: (b * nb + n, COL_Q // ATTN_WIDTH)),
                  pl.BlockSpec((ATTN_BLOCK, 2 * KV_WIDTH), lambda b, n: (b * nb + n, COL_KV // (2 * KV_WIDTH))),
                  pl.BlockSpec((ATTN_BLOCK, LANES), rowmap),
                  pl.BlockSpec((ATTN_BLOCK, LANES), rowmap),
                  pl.BlockSpec((ATTN_BLOCK, LANES), rowmap),
                  pl.BlockSpec((1, ATTN_WIDTH), lambda b, n: (0, 0))],
        out_specs=pl.BlockSpec((ATTN_BLOCK, ATTN_WIDTH), rowmap),
        out_shape=jax.ShapeDtypeStruct((t, ATTN_WIDTH), BF16),
        scratch_shapes=[pltpu.VMEM((ATTN_BLOCK, KV_WIDTH), F32), pltpu.VMEM((ATTN_BLOCK, KV_WIDTH), F32)],
        compiler_params=_cparams(("arbitrary", "arbitrary")),
        name="attn",
    )(sinks, h, h, cos_t, sa_t, sb_t, gain)


def _cumsum_rows(x):
    rows = x.shape[0]
    row = lax.broadcasted_iota(jnp.int32, x.shape, 0)
    d = 1
    while d < rows:
        x = x + jnp.where(row >= d, pltpu.roll(x, d, axis=0), 0.0)
        d *= 2
    return x


def _hgrn_kernel(in_ref, lb_ref, gain_ref, o_ref, st_ref):
    r = pl.program_id(2)

    @pl.when(r == 0)
    def _():
        st_ref[...] = jnp.zeros_like(st_ref)

    c = HG_CHUNK
    nsub = c // HG_SUB
    ones_kk = jnp.ones((HG_HEAD_DIM, HG_HEAD_DIM), BF16)
    row_c = lax.broadcasted_iota(jnp.int32, (c, HG_HEAD_DIM), 0)
    row_s = lax.broadcasted_iota(jnp.int32, (HG_SUB, HG_HEAD_DIM), 0)
    nt = (((1,), (1,)), ((), ()))
    tn = (((0,), (0,)), ((), ()))

    def one_head(r0, hh):
        col = hh * 4 * HG_HEAD_DIM
        hcol = slice(hh * HG_HEAD_DIM, (hh + 1) * HG_HEAD_DIM)
        lb = lb_ref[:, hcol]
        gain = gain_ref[:, hcol]
        q = in_ref[pl.ds(r0, c), col:col + HG_HEAD_DIM]
        fp = in_ref[pl.ds(r0, c), col + HG_HEAD_DIM:col + 2 * HG_HEAD_DIM]
        v = in_ref[pl.ds(r0, c), col + 2 * HG_HEAD_DIM:col + 3 * HG_HEAD_DIM]
        g = in_ref[pl.ds(r0, c), col + 3 * HG_HEAD_DIM:col + 4 * HG_HEAD_DIM]
        qf = jax.nn.silu(q)
        f = lb + (1.0 - lb) * jax.nn.sigmoid(fp)
        logf = jnp.log(jnp.maximum(f, TINY))
        kf = (1.0 - lb) * jax.nn.sigmoid(-fp)
        b = _cumsum_rows(logf)
        vb = v.astype(BF16)

        s_rows = [jnp.zeros((HG_SUB, c), F32)]
        for i in range(1, nsub):
            lo = i * HG_SUB
            ref_b = b[lo - 1:lo, :]
            qi = qf[lo:lo + HG_SUB, :] * jnp.exp(b[lo:lo + HG_SUB, :] - ref_b)
            ki = kf * jnp.exp(jnp.where(row_c < lo, ref_b - b, NEG_BIG))
            s_rows.append(lax.dot_general(qi.astype(BF16), ki.astype(BF16), nt, preferred_element_type=F32))
        scores = jnp.concatenate(s_rows, axis=0)
        o = jnp.dot(scores.astype(BF16), vb, preferred_element_type=F32)

        o_diag = []
        for j in range(nsub):
            lo = j * HG_SUB
            bs = b[lo:lo + HG_SUB, :]
            ks = kf[lo:lo + HG_SUB, :]
            qs = qf[lo:lo + HG_SUB, :]
            vs = v[lo:lo + HG_SUB, :]
            tiles = []
            for tt in range(HG_SUB):
                e = jnp.where(row_s <= tt, bs[tt:tt + 1, :] - bs, NEG_BIG)
                tiles.append(jnp.exp(e) * ks * qs[tt:tt + 1, :])
            w = jnp.concatenate(tiles, axis=0).astype(BF16)
            dfull = jnp.dot(w, ones_kk, preferred_element_type=F32)
            contrib = dfull.reshape(HG_SUB, HG_SUB, HG_HEAD_DIM) * vs[None, :, :]
            o_diag.append(jnp.sum(contrib, axis=1))
        o = o + jnp.concatenate(o_diag, axis=0)

        st = st_ref[hh]
        qb = (qf * jnp.exp(b)).astype(BF16)
        o = o + lax.dot_general(qb, st.astype(BF16), nt, preferred_element_type=F32)
        b_last = b[c - 1:c, :]
        kn = (kf * jnp.exp(b_last - b)).astype(BF16)
        st_ref[hh] = st * jnp.exp(b_last) + lax.dot_general(vb, kn, tn, preferred_element_type=F32)

        y = o * lax.rsqrt(jnp.mean(o * o, axis=-1, keepdims=True) + RMS_EPS) * gain
        o_ref[pl.ds(r0, c), hcol] = (y * jax.nn.silu(g)).astype(o_ref.dtype)

    def chunk(ci, carry):
        r0 = pl.multiple_of(ci * c, c)
        for hh in range(st_ref.shape[0]):
            one_head(r0, hh)
        return carry

    lax.fori_loop(0, in_ref.shape[0] // c, chunk, 0)


def _hgrn2(h, lb, gain, bsz, seq):
    t = bsz * seq
    rows = min(HG_ROWS, seq)
    nr = seq // rows
    hd_block0 = COL_HG // (4 * HG_HEAD_DIM)
    grp = HG_GROUP
    return pl.pallas_call(
        _hgrn_kernel,
        grid=(bsz, HG_HEADS // grp, nr),
        in_specs=[pl.BlockSpec((rows, grp * 4 * HG_HEAD_DIM), lambda b, hd, r: (b * nr + r, hd_block0 // grp + hd)),
                  pl.BlockSpec((1, grp * HG_HEAD_DIM), lambda b, hd, r: (0, hd)),
                  pl.BlockSpec((1, grp * HG_HEAD_DIM), lambda b, hd, r: (0, hd))],
        out_specs=pl.BlockSpec((rows, grp * HG_HEAD_DIM), lambda b, hd, r: (b * nr + r, hd)),
        out_shape=jax.ShapeDtypeStruct((t, HG_WIDTH), BF16),
        scratch_shapes=[pltpu.VMEM((grp, HG_HEAD_DIM, HG_HEAD_DIM), F32)],
        compiler_params=_cparams(("arbitrary", "arbitrary", "arbitrary")),
        name="hgrn2",
    )(h, lb, gain)


def _lru_kernel(in_ref, cw_ref, cb_ref, wa_ref, ba_ref, wx_ref, bx_ref, lam_ref, gain_ref, o_ref,
                tail_ref, h_ref):
    r = pl.program_id(1)

    @pl.when(r == 0)
    def _():
        tail_ref[...] = jnp.zeros_like(tail_ref)
        h_ref[...] = jnp.zeros_like(h_ref)

    rows = in_ref.shape[0]
    x = in_ref[:, 0:LRU_WIDTH]
    gr = in_ref[:, LRU_WIDTH:2 * LRU_WIDTH]
    tail = tail_ref[...]
    row8 = lax.broadcasted_iota(jnp.int32, (8, LRU_WIDTH), 0)

    xc = x * cw_ref[CONV_WIDTH - 1:CONV_WIDTH, :] + cb_ref[...]
    for k in range(1, CONV_WIDTH):
        xs = pltpu.roll(x, k, axis=0)
        head = jnp.where(row8 < k, pltpu.roll(tail, k, axis=0), xs[0:8, :])
        xs = jnp.concatenate([head, xs[8:, :]], axis=0)
        xc = xc + xs * cw_ref[CONV_WIDTH - 1 - k:CONV_WIDTH - k, :]
    tail_ref[...] = x[rows - 8:rows, :]

    xcb = xc.astype(BF16)
    rg = jax.nn.sigmoid(jnp.dot(xcb, wa_ref[...], preferred_element_type=F32) + ba_ref[...])
    ig = jax.nn.sigmoid(jnp.dot(xcb, wx_ref[...], preferred_element_type=F32) + bx_ref[...])
    log_a = -LRU_C * rg * jax.nn.softplus(-lam_ref[...])
    a = jnp.exp(log_a)
    th = jnp.tanh(log_a)
    neg_expm1 = -2.0 * th / (1.0 - th)
    u = jnp.sqrt(jnp.maximum(neg_expm1, 0.0)) * (ig * xc)

    row = lax.broadcasted_iota(jnp.int32, (rows, LRU_WIDTH), 0)
    d = 1
    while d < rows:
        keep = row >= d
        a_s = jnp.where(keep, pltpu.roll(a, d, axis=0), 1.0)
        u_s = jnp.where(keep, pltpu.roll(u, d, axis=0), 0.0)
        u = a * u_s + u
        a = a * a_s
        d *= 2
    hcur = u + a * h_ref[0:1, :]
    h_ref[...] = jnp.broadcast_to(hcur[rows - 1:rows, :], h_ref.shape)

    y = hcur * lax.rsqrt(jnp.mean(hcur * hcur, axis=-1, keepdims=True) + RMS_EPS) * gain_ref[...]
    o_ref[...] = (y * jax.nn.gelu(gr)).astype(o_ref.dtype)


def _rglru(h, cw, cb, wa, ba, wx, bx, lam, gain, bsz, seq):
    t = bsz * seq
    rows = min(LRU_ROWS, seq)
    nr = seq // rows
    vec = pl.BlockSpec((1, LRU_WIDTH), lambda b, r: (0, 0))
    mat = pl.BlockSpec((LRU_WIDTH, LRU_WIDTH), lambda b, r: (0, 0))
    return pl.pallas_call(
        _lru_kernel,
        grid=(bsz, nr),
        in_specs=[pl.BlockSpec((rows, 2 * LRU_WIDTH), lambda b, r: (b * nr + r, COL_LRU // (2 * LRU_WIDTH))),
                  pl.BlockSpec((CONV_WIDTH, LRU_WIDTH), lambda b, r: (0, 0)),
                  vec, mat, vec, mat, vec, vec, vec],
        out_specs=pl.BlockSpec((rows, LRU_WIDTH), lambda b, r: (b * nr + r, 0)),
        out_shape=jax.ShapeDtypeStruct((t, LRU_WIDTH), BF16),
        scratch_shapes=[pltpu.VMEM((8, LRU_WIDTH), F32), pltpu.VMEM((8, LRU_WIDTH), F32)],
        compiler_params=_cparams(("arbitrary", "arbitrary")),
        name="rglru",
    )(h, cw, cb, wa, ba, wx, bx, lam, gain)


def _layer_norm(z, g, b):
    mu = jnp.mean(z, axis=-1, keepdims=True)
    zc = z - mu
    var = jnp.mean(zc * zc, axis=-1, keepdims=True)
    return zc * lax.rsqrt(var + LN_EPS) * g + b


def _outproj_kernel(ya_ref, yh_ref, yl_ref, x_ref, wa_ref, wh_ref, wl_ref, g_ref, b_ref, rw_ref, rb_ref,
                    x1_ref, idx_ref, rank_ref, gate_ref, counts_ref, carry_ref):
    i = pl.program_id(0)

    @pl.when(i == 0)
    def _():
        carry_ref[...] = jnp.zeros_like(carry_ref)

    mixed = jnp.dot(ya_ref[...], wa_ref[...], preferred_element_type=F32)
    mixed = mixed + jnp.dot(yh_ref[...], wh_ref[...], preferred_element_type=F32)
    mixed = mixed + jnp.dot(yl_ref[...], wl_ref[...], preferred_element_type=F32)
    x1 = _layer_norm(DEEPNORM_ALPHA * x_ref[...] + mixed, g_ref[...], b_ref[...])
    x1_ref[...] = x1

    tm = x1.shape[0]
    logits = lax.dot_general(rw_ref[...], x1.astype(BF16), (((1,), (1,)), ((), ())), preferred_element_type=F32)
    scores = jax.nn.sigmoid(logits)
    sel = scores + rb_ref[...]
    shp = (N_GROUPS, EXPERTS_PER_GROUP, tm)
    s3 = sel.reshape(shp)
    sc3 = scores.reshape(shp)
    e_in = lax.broadcasted_iota(jnp.int32, shp, 1)
    g_id = lax.broadcasted_iota(jnp.int32, shp, 0)
    e_id = g_id * EXPERTS_PER_GROUP + e_in

    m1 = jnp.max(s3, axis=1, keepdims=True)
    i1 = jnp.min(jnp.where(s3 == m1, e_in, EXPERTS_PER_GROUP), axis=1, keepdims=True)
    m2 = jnp.max(jnp.where(e_in == i1, NEG_PICKED, s3), axis=1, keepdims=True)
    gs = m1 + m2
    g1 = lax.broadcasted_iota(jnp.int32, gs.shape, 0)
    gsel = jnp.zeros(gs.shape, jnp.int32)
    cur = gs
    for _ in range(TOPK_GROUPS):
        m = jnp.max(cur, axis=0, keepdims=True)
        pick = g1 == jnp.min(jnp.where(cur == m, g1, N_GROUPS), axis=0, keepdims=True)
        gsel = jnp.where(pick, 1, gsel)
        cur = jnp.where(pick, NEG_PICKED, cur)
    cur = jnp.where(gsel > 0, s3, NEG_BIG)

    def pick_sum(pick, vals):
        return jnp.sum(jnp.sum(jnp.where(pick, vals, 0.0), axis=1, keepdims=True), axis=0, keepdims=True).reshape(1, tm)

    idx_rows, w_rows, picks = [], [], []
    onehot = jnp.zeros(shp, F32)
    for _ in range(TOPK):
        m = jnp.max(jnp.max(cur, axis=1, keepdims=True), axis=0, keepdims=True)
        cand = jnp.where(cur == m, e_id, N_EXPERTS)
        ii = jnp.min(jnp.min(cand, axis=1, keepdims=True), axis=0, keepdims=True)
        pick = e_id == ii
        picks.append(pick)
        onehot = jnp.where(pick, 1.0, onehot)
        w_rows.append(pick_sum(pick, sc3))
        idx_rows.append(ii.reshape(1, tm))
        cur = jnp.where(pick, NEG_PICKED, cur)
    w = jnp.concatenate(w_rows, axis=0)
    idx = jnp.concatenate(idx_rows, axis=0)
    gates = w / jnp.sum(w, axis=0, keepdims=True) * ROUTED_SCALE

    oh = onehot.reshape(N_EXPERTS, tm).astype(BF16)
    r_i = lax.broadcasted_iota(jnp.int32, (tm, tm), 0)
    c_i = lax.broadcasted_iota(jnp.int32, (tm, tm), 1)
    before = jnp.where(r_i < c_i, 1.0, 0.0).astype(BF16)
    carry = carry_ref[...]
    prefix3 = (jnp.dot(oh, before, preferred_element_type=F32) + carry).reshape(shp)
    rank = jnp.concatenate([pick_sum(pk, prefix3) for pk in picks], axis=0).astype(jnp.int32)
    carry = carry + jnp.dot(oh, jnp.ones((tm, tm), BF16), preferred_element_type=F32)
    carry_ref[...] = carry
    counts_ref[...] = carry.astype(jnp.int32)

    for c in range(tm // IDX_TILE):
        sl = slice(c * IDX_TILE, (c + 1) * IDX_TILE)
        idx_ref[c] = idx[:, sl]
        rank_ref[c] = rank[:, sl]
        gate_ref[c] = gates[:, sl]


def _outproj_router(ya, yh, yl, x2d, w_out, ln_g, ln_b, rw_t, rb):
    t = x2d.shape[0]
    tm = min(OUT_TM, t)
    nt = tm // IDX_TILE
    rowb = lambda w: pl.BlockSpec((tm, w), lambda i: (i, 0))
    full = lambda a, b: pl.BlockSpec((a, b), lambda i: (0, 0))
    tiles = pl.BlockSpec((nt, TOPK, IDX_TILE), lambda i: (i, 0, 0))
    wa = pl.BlockSpec((ATTN_WIDTH, D_MODEL), lambda i: (0, 0))
    wh = pl.BlockSpec((HG_WIDTH, D_MODEL), lambda i: (ATTN_WIDTH // HG_WIDTH, 0))
    wl = pl.BlockSpec((LRU_WIDTH, D_MODEL), lambda i: ((ATTN_WIDTH + HG_WIDTH) // LRU_WIDTH, 0))
    tile_shape = lambda dt: jax.ShapeDtypeStruct((t // IDX_TILE, TOPK, IDX_TILE), dt)
    return pl.pallas_call(
        _outproj_kernel,
        grid=(t // tm,),
        in_specs=[rowb(ATTN_WIDTH), rowb(HG_WIDTH), rowb(LRU_WIDTH), rowb(D_MODEL), wa, wh, wl,
                  full(1, D_MODEL), full(1, D_MODEL), full(N_EXPERTS, D_MODEL), full(N_EXPERTS, 1)],
        out_specs=[rowb(D_MODEL), tiles, tiles, tiles, full(N_EXPERTS, tm)],
        out_shape=[jax.ShapeDtypeStruct((t, D_MODEL), F32),
                   tile_shape(jnp.int32), tile_shape(jnp.int32), tile_shape(F32),
                   jax.ShapeDtypeStruct((N_EXPERTS, tm), jnp.int32)],
        scratch_shapes=[pltpu.VMEM((N_EXPERTS, tm), F32)],
        compiler_params=_cparams(("arbitrary",)),
        name="outproj_router",
    )(ya, yh, yl, x2d, w_out, w_out, w_out, ln_g, ln_b, rw_t, rb)


def _dispatch_kernel(pend_ref, padded_ref, dest_hbm, x_ref, p_ref, sw1_ref, sw3_ref, sw2_ref, pw_ref,
                     gw_ref, gb_ref, xs_hbm, sp_ref, dest_smem, zeros_ref, rows_ref, sem_idx, sem_rows, sem_zero):
    i = pl.program_id(0)
    tm = x_ref.shape[0]
    bm = zeros_ref.shape[0] // PACK_TILES
    nt = tm // IDX_TILE

    @pl.when(i == 0)
    def _():
        zeros_ref[...] = jnp.zeros_like(zeros_ref)

        def fill(e, carry):
            @pl.when(padded_ref[e] > 0)
            def _():
                start = pl.multiple_of(pend_ref[e] - bm, bm)
                pltpu.make_async_copy(zeros_ref, xs_hbm.at[pl.ds(start * PACK_TILES, bm * PACK_TILES)], sem_zero).start()
            return carry

        def fill_wait(e, carry):
            @pl.when(padded_ref[e] > 0)
            def _():
                pltpu.make_async_copy(zeros_ref, xs_hbm.at[pl.ds(0, bm * PACK_TILES)], sem_zero).wait()
            return carry

        def fill_tail(blk, carry):
            start = pl.multiple_of(blk * bm, bm)
            pltpu.make_async_copy(zeros_ref, xs_hbm.at[pl.ds(start * PACK_TILES, bm * PACK_TILES)], sem_zero).start()
            return carry

        def fill_tail_wait(blk, carry):
            pltpu.make_async_copy(zeros_ref, xs_hbm.at[pl.ds(0, bm * PACK_TILES)], sem_zero).wait()
            return carry

        n_used = pend_ref[N_EXPERTS - 1] // bm
        n_blocks = xs_hbm.shape[0] // (bm * PACK_TILES)
        lax.fori_loop(0, N_EXPERTS, fill, 0)
        lax.fori_loop(n_used, n_blocks, fill_tail, 0)
        lax.fori_loop(0, N_EXPERTS, fill_wait, 0)
        lax.fori_loop(n_used, n_blocks, fill_tail_wait, 0)

    idx_cp = pltpu.make_async_copy(dest_hbm.at[pl.ds(i * nt, nt)], dest_smem, sem_idx)
    idx_cp.start()
    x = x_ref[...]
    _to_row_tiles(rows_ref, _pack_bf16_pairs(x))
    idx_cp.wait()

    for c in range(nt):
        for r in range(IDX_TILE):
            row = c * IDX_TILE + r
            for j in range(TOPK):
                d = dest_smem[c, j, r]
                pltpu.make_async_copy(rows_ref.at[pl.ds(row * PACK_TILES, PACK_TILES)],
                                      xs_hbm.at[pl.ds(d * PACK_TILES, PACK_TILES)], sem_rows).start(priority=j % 2)

    xb = x.astype(BF16)
    h1 = jnp.dot(xb, sw1_ref[...], preferred_element_type=F32)
    h3 = jnp.dot(xb, sw3_ref[...], preferred_element_type=F32)
    shared = jnp.dot((jax.nn.silu(h1) * h3).astype(BF16), sw2_ref[...], preferred_element_type=F32)
    gate = jax.nn.sigmoid(jnp.dot(xb, gw_ref[...], preferred_element_type=F32) + gb_ref[...])
    ple = gate * jnp.dot(p_ref[...].astype(BF16), pw_ref[...], preferred_element_type=F32)
    sp_ref[...] = shared + ple

    n_copied = TOPK * tm * PACK_TILES
    pltpu.make_async_copy(xs_hbm.at[pl.ds(0, n_copied)], xs_hbm.at[pl.ds(0, n_copied)], sem_rows).wait()


def _dispatch(layer, pend, padded, dest_tiles, x1, p3, sw1, sw3, sw2, pw, gw, gb, n_rows, bm):
    t = x1.shape[0]
    tm = min(DISP_TM, t)
    full = lambda a, b: pl.BlockSpec((a, b), lambda i, pe, pa: (0, 0))
    grid_spec = pltpu.PrefetchScalarGridSpec(
        num_scalar_prefetch=2,
        grid=(t // tm,),
        in_specs=[pl.BlockSpec(memory_space=pl.ANY),
                  pl.BlockSpec((tm, D_MODEL), lambda i, pe, pa: (i, 0)),
                  pl.BlockSpec((None, tm, PLE_DIM), lambda i, pe, pa: (layer, i, 0)),
                  full(D_MODEL, EXPERT_DIM), full(D_MODEL, EXPERT_DIM), full(EXPERT_DIM, D_MODEL),
                  full(PLE_DIM, D_MODEL), full(D_MODEL, D_MODEL), full(1, D_MODEL)],
        out_specs=[pl.BlockSpec(memory_space=pl.ANY),
                   pl.BlockSpec((tm, D_MODEL), lambda i, pe, pa: (i, 0))],
        scratch_shapes=[pltpu.SMEM((tm // IDX_TILE, TOPK, IDX_TILE), jnp.int32),
                        pltpu.VMEM((bm * PACK_TILES, LANES), jnp.uint32),
                        pltpu.VMEM((tm * PACK_TILES, LANES), jnp.uint32),
                        pltpu.SemaphoreType.DMA, pltpu.SemaphoreType.DMA, pltpu.SemaphoreType.DMA],
    )
    return pl.pallas_call(
        _dispatch_kernel,
        grid_spec=grid_spec,
        out_shape=[jax.ShapeDtypeStruct((n_rows * PACK_TILES, LANES), jnp.uint32),
                   jax.ShapeDtypeStruct((t, D_MODEL), F32)],
        compiler_params=_cparams(("arbitrary",)),
        name="dispatch_shared_ple",
    )(pend, padded, dest_tiles, x1, p3, sw1, sw3, sw2, pw, gw, gb)


def _moe_kernel(be_ref, nused_ref, x_ref, w1_ref, w3_ref, w2_ref, y_ref, w1b, w3b, w2b):
    n = pl.program_id(0)

    @pl.when(n < nused_ref[0])
    def _():
        e = be_ref[n]
        e_prev = be_ref[jnp.maximum(n - 1, 0)]

        @pl.when((n == 0) | (e != e_prev))
        def _():
            w1b[...] = w1_ref[...].astype(BF16)
            w3b[...] = w3_ref[...].astype(BF16)
            w2b[...] = w2_ref[...].astype(BF16)

        sub = min(MOE_SUB, x_ref.shape[0] // PACK_TILES)
        for c in range(x_ref.shape[0] // (PACK_TILES * sub)):
            xb = _unpack_bf16_pairs(_from_row_tiles(x_ref, PACK_TILES, base=c * sub * PACK_TILES, rows=sub))
            h1 = jnp.dot(xb, w1b[...], preferred_element_type=F32)
            h3 = jnp.dot(xb, w3b[...], preferred_element_type=F32)
            act = (jax.nn.silu(h1) * h3).astype(BF16)
            y = jnp.dot(act, w2b[...], preferred_element_type=F32)
            _to_row_tiles(y_ref, _pack_bf16_pairs(y), base=c * sub * PACK_TILES)

    @pl.when(n >= nused_ref[0])
    def _():
        y_ref[...] = jnp.zeros_like(y_ref)


def _moe(layer, block_expert, n_used, xs, w1, w3, w2, bm):
    n_blocks = xs.shape[0] // (bm * PACK_TILES)
    wspec = lambda a, b: pl.BlockSpec((None, None, a, b), lambda n, be, nu: (layer, be[n], 0, 0))
    grid_spec = pltpu.PrefetchScalarGridSpec(
        num_scalar_prefetch=2,
        grid=(n_blocks,),
        in_specs=[pl.BlockSpec((bm * PACK_TILES, LANES), lambda n, be, nu: (jnp.minimum(n, nu[0] - 1), 0)),
                  wspec(D_MODEL, EXPERT_DIM), wspec(D_MODEL, EXPERT_DIM), wspec(EXPERT_DIM, D_MODEL)],
        out_specs=pl.BlockSpec((bm * PACK_TILES, LANES), lambda n, be, nu: (n, 0)),
        scratch_shapes=[pltpu.VMEM((D_MODEL, EXPERT_DIM), BF16),
                        pltpu.VMEM((D_MODEL, EXPERT_DIM), BF16),
                        pltpu.VMEM((EXPERT_DIM, D_MODEL), BF16)],
    )
    return pl.pallas_call(
        _moe_kernel,
        grid_spec=grid_spec,
        out_shape=jax.ShapeDtypeStruct((n_blocks * bm * PACK_TILES, LANES), jnp.uint32),
        compiler_params=_cparams(("arbitrary",)),
        name="moe_experts",
    )(block_expert, n_used, xs, w1, w3, w2)


def _final_kernel(dest_hbm, gate_hbm, y_hbm, x_ref, sp_ref, g_ref, b_ref, o_ref,
                  dest_smem, gate_smem, ybuf, routed, sem_idx, sem_rows):
    i = pl.program_id(0)
    n = pl.num_programs(0)
    tm = x_ref.shape[0]
    nbuf = ybuf.shape[0]
    ahead = nbuf - 1

    def idx_copies(tile, s):
        return (pltpu.make_async_copy(dest_hbm.at[tile], dest_smem.at[s], sem_idx.at[s]),
                pltpu.make_async_copy(gate_hbm.at[tile], gate_smem.at[s], sem_idx.at[s]))

    def start_idx(tile, s):
        for cp in idx_copies(tile, s):
            cp.start()

    def wait_idx(tile, s):
        for cp in idx_copies(tile, s):
            cp.wait()

    def issue_row(s, r):
        for j in range(TOPK):
            d = dest_smem[s, j, r]
            pltpu.make_async_copy(y_hbm.at[pl.ds(d * PACK_TILES, PACK_TILES)],
                                  ybuf.at[s, j, pl.ds(r * PACK_TILES, PACK_TILES)], sem_rows.at[s]).start(priority=j % 2)

    def issue_rows_loop(s):
        def issue(r, carry):
            issue_row(s, r)
            return carry

        lax.fori_loop(0, tm, issue, 0)

    @pl.when(i == 0)
    def _():
        for k in range(ahead):
            @pl.when(k < n)
            def _(k=k):
                start_idx(k, k)
                wait_idx(k, k)
                issue_rows_loop(k)

        @pl.when(ahead < n)
        def _():
            start_idx(ahead, ahead)

    def finish():
        routed2d = jnp.concatenate([_from_row_tiles(routed.at[0], PACK_TILES),
                                    _from_row_tiles(routed.at[1], PACK_TILES)], axis=1)
        z = DEEPNORM_ALPHA * x_ref[...] + routed2d + sp_ref[...]
        o_ref[...] = _layer_norm(z, g_ref[...], b_ref[...])

    def step(slot):
        for j in range(TOPK):
            pltpu.make_async_copy(y_hbm.at[pl.ds(0, tm * PACK_TILES)], ybuf.at[slot, j], sem_rows.at[slot]).wait()

        def combine(r, carry):
            rows = pl.ds(pl.multiple_of(r * PACK_TILES, PACK_TILES), PACK_TILES)
            acc_lo = acc_hi = None
            for j in range(TOPK):
                w = ybuf[slot, j, rows, :]
                g = gate_smem[slot, j, r]
                lo = g * lax.bitcast_convert_type(w << 16, F32)
                hi = g * lax.bitcast_convert_type(w & jnp.uint32(0xFFFF0000), F32)
                acc_lo = lo if acc_lo is None else acc_lo + lo
                acc_hi = hi if acc_hi is None else acc_hi + hi
            routed[0, rows, :] = acc_lo
            routed[1, rows, :] = acc_hi
            return carry

        lax.fori_loop(0, tm, combine, 0)

        @pl.when(i + nbuf < n)
        def _():
            start_idx(i + nbuf, slot)

        @pl.when(i + ahead < n)
        def _():
            s_new = (slot + ahead) % nbuf
            wait_idx(i + ahead, s_new)
            for r in range(tm):
                issue_row(s_new, r)
            finish()

        @pl.when(i + ahead >= n)
        def _():
            finish()

    for k in range(nbuf):
        pl.when(i % nbuf == k)(lambda k=k: step(k))


def _final(dest_tiles, gate_tiles, y_sorted, x1, sp, ln_g, ln_b):
    t = x1.shape[0]
    tm = min(FIN_TM, t)
    full = lambda a, b: pl.BlockSpec((a, b), lambda i: (0, 0))
    rowb = pl.BlockSpec((tm, D_MODEL), lambda i: (i, 0))
    return pl.pallas_call(
        _final_kernel,
        grid=(t // tm,),
        in_specs=[pl.BlockSpec(memory_space=pl.ANY), pl.BlockSpec(memory_space=pl.ANY),
                  pl.BlockSpec(memory_space=pl.ANY), rowb, rowb, full(1, D_MODEL), full(1, D_MODEL)],
        out_specs=rowb,
        out_shape=jax.ShapeDtypeStruct((t, D_MODEL), F32),
        scratch_shapes=[pltpu.SMEM((FIN_BUFS, TOPK, tm), jnp.int32),
                        pltpu.SMEM((FIN_BUFS, TOPK, tm), F32),
                        pltpu.VMEM((FIN_BUFS, TOPK, tm * PACK_TILES, LANES), jnp.uint32),
                        pltpu.VMEM((2, tm * PACK_TILES, LANES), F32),
                        pltpu.SemaphoreType.DMA((FIN_BUFS,)),
                        pltpu.SemaphoreType.DMA((FIN_BUFS,))],
        compiler_params=_cparams(("arbitrary",)),
        name="combine_ln",
    )(dest_tiles, gate_tiles, y_sorted, x1, sp, ln_g, ln_b)


def _dispatch_plan(idx_tiles, rank_tiles, counts, bm, n_blocks):
    padded = ((counts + bm - 1) // bm) * bm
    pend = jnp.cumsum(padded).astype(jnp.int32)
    pstart = pend - padded
    experts = jnp.arange(N_EXPERTS, dtype=jnp.int32)
    start_of = jnp.sum(jnp.where(idx_tiles[..., None] == experts, pstart, 0), axis=-1)
    dest_tiles = (start_of + rank_tiles).astype(jnp.int32)
    block_row = jnp.arange(n_blocks, dtype=jnp.int32) * bm
    block_expert = jnp.minimum(jnp.sum((pend[None, :] <= block_row[:, None]).astype(jnp.int32), axis=1),
                               N_EXPERTS - 1)
    n_used = (pend[-1] // bm).reshape(1)
    return pend, padded.astype(jnp.int32), dest_tiles, block_expert, n_used


def _rotary_lane_tables(positions):
    inv_freq = ROPE_THETA ** (-jnp.arange(0, ROT_DIM, 2, dtype=F32) / ROT_DIM)
    ang = positions.astype(F32).reshape(-1)[:, None] * inv_freq
    cos, sin = jnp.cos(ang), jnp.sin(ang)
    half = ROT_DIM // 2
    t = ang.shape[0]
    one = jnp.ones((t, HEAD_DIM - ROT_DIM), F32)
    zero = jnp.zeros((t, HEAD_DIM - ROT_DIM), F32)
    zh = jnp.zeros((t, half), F32)
    cos64 = jnp.concatenate([cos, cos, one], axis=1)
    sa64 = jnp.concatenate([zh, sin, zero], axis=1)
    sb64 = jnp.concatenate([-sin, zh, zero], axis=1)
    tile2 = lambda m: jnp.concatenate([m, m], axis=1)
    return tile2(cos64), tile2(sa64), tile2(sb64)


def _permute_in_cols(w):
    off = [0]
    for s in IN_SIZES:
        off.append(off[-1] + s)
    aq, ak, av, hq, hf, hi, hg, lx, lg = [w[:, off[k]:off[k + 1]] for k in range(9)]
    parts = []
    for hd in range(HG_HEADS):
        sl = slice(hd * HG_HEAD_DIM, (hd + 1) * HG_HEAD_DIM)
        parts += [hq[:, sl], hf[:, sl], hi[:, sl], hg[:, sl]]
    parts += [aq, lx, lg, ak, av]
    return jnp.concatenate(parts, axis=1)


def _block_diag(w):
    hds, d, _ = w.shape
    eye = jnp.eye(hds, dtype=w.dtype)
    return (eye[:, None, :, None] * w[:, :, None, :]).reshape(hds * d, hds * d)


def kernel(x, p, positions, w_in, w_out, attn_sinks, attn_norm, hg_lb_logits, hg_norm, lru_conv_w, lru_conv_b, lru_wa, lru_ba, lru_wx, lru_bx, lru_lambda, lru_norm, ln1_g, ln1_b, router_w, router_b, exp_w1, exp_w3, exp_w2, sh_w1, sh_w3, sh_w2, ple_w, ple_gate_w, ple_gate_b, ln2_g, ln2_b):
    bsz, seq, _ = x.shape
    t = bsz * seq
    depth = w_in.shape[0]
    lb_sm = jax.nn.softmax(hg_lb_logits.astype(F32), axis=0)
    hg_lb = jnp.maximum(jnp.cumsum(lb_sm, axis=0) - lb_sm[0], 0.0)
    cos_t, sa_t, sb_t = _rotary_lane_tables(positions)
    p3 = p.reshape(depth, t, PLE_DIM)
    bm = min(MOE_BM, t)
    n_blocks = t * TOPK // bm + N_EXPERTS
    row = lambda v: v.reshape(1, -1)

    xc = x.reshape(t, D_MODEL)
    for i in range(depth):
        h = _inproj(xc, _permute_in_cols(w_in[i].astype(BF16)))
        ya = _attention(h, attn_sinks[i], cos_t, sa_t, sb_t, row(attn_norm[i]), bsz, seq)
        yh = _hgrn2(h, row(hg_lb[i]), row(hg_norm[i]), bsz, seq)
        yl = _rglru(h, lru_conv_w[i], row(lru_conv_b[i]), _block_diag(lru_wa[i]).astype(BF16), row(lru_ba[i]),
                    _block_diag(lru_wx[i]).astype(BF16), row(lru_bx[i]), row(lru_lambda[i]), row(lru_norm[i]),
                    bsz, seq)
        x1, idx_tiles, rank_tiles, gate_tiles, counts = _outproj_router(
            ya, yh, yl, xc, w_out[i].astype(BF16), row(ln1_g[i]), row(ln1_b[i]),
            router_w[i].T.astype(BF16), router_b[i].reshape(N_EXPERTS, 1))
        pend, padded, dest_tiles, block_expert, n_used = _dispatch_plan(idx_tiles, rank_tiles, counts[:, 0], bm, n_blocks)
        xs, sp = _dispatch(i, pend, padded, dest_tiles, x1, p3, sh_w1[i].astype(BF16), sh_w3[i].astype(BF16),
                           sh_w2[i].astype(BF16), ple_w[i].astype(BF16), ple_gate_w[i].astype(BF16),
                           row(ple_gate_b[i]), n_blocks * bm, bm)
        y_sorted = _moe(i, block_expert, n_used, xs, exp_w1, exp_w3, exp_w2, bm)
        xc = _final(dest_tiles, gate_tiles, y_sorted, x1, sp, row(ln2_g[i]), row(ln2_b[i]))
    return xc.reshape(bsz, seq, D_MODEL)
```

```python
import jax
import jax.numpy as jnp
from jax import lax
from jax.experimental import pallas as pl
from jax.experimental.pallas import tpu as pltpu

F32 = jnp.float32
BF16 = jnp.bfloat16

D_MODEL = 2048
PLE_DIM = 256
N_HEADS = 16
KV_HEADS = 2
Q_PER_KV = N_HEADS // KV_HEADS
HEAD_DIM = 64
ATTN_WIDTH = N_HEADS * HEAD_DIM
KV_WIDTH = KV_HEADS * HEAD_DIM
ATTN_BLOCK = 128
ROT_DIM = HEAD_DIM // 4
ROPE_THETA = 500000.0
HG_HEADS = 4
HG_HEAD_DIM = 128
HG_WIDTH = HG_HEADS * HG_HEAD_DIM
HG_CHUNK = 64
HG_SUB = 8
LRU_HEADS = 8
LRU_HEAD_DIM = 64
LRU_WIDTH = LRU_HEADS * LRU_HEAD_DIM
CONV_WIDTH = 4
LRU_C = 8.0
IN_SIZES = (ATTN_WIDTH, KV_WIDTH, KV_WIDTH, HG_WIDTH, HG_WIDTH, HG_WIDTH, HG_WIDTH, LRU_WIDTH, LRU_WIDTH)
IN_COLS = sum(IN_SIZES)
MIX_WIDTH = ATTN_WIDTH + HG_WIDTH + LRU_WIDTH
N_EXPERTS = 64
TOPK = 8
N_GROUPS = 8
TOPK_GROUPS = 4
EXPERTS_PER_GROUP = N_EXPERTS // N_GROUPS
EXPERT_DIM = 512
ROUTED_SCALE = 2.5
DEPTH = 2
DEEPNORM_ALPHA = (2 * DEPTH) ** 0.25
RMS_EPS = 1e-6
LN_EPS = 1e-5
NEG_BIG = -1e30
NEG_PICKED = -3e38
TINY = 1e-30

LANES = 128
ROW_TILES = D_MODEL // LANES
PACK_TILES = ROW_TILES // 2
VMEM_LIMIT = 56 * 1024 * 1024

COL_HG = 0
COL_Q = COL_HG + 4 * HG_WIDTH
COL_LRU = COL_Q + ATTN_WIDTH
COL_KV = COL_LRU + 2 * LRU_WIDTH

INPROJ_TM = 512
INPROJ_TN = IN_COLS // 2
HG_ROWS = 512
HG_GROUP = 4
LRU_ROWS = 256
OUT_TM = 512
MOE_BM = 512
MOE_SUB = 512
IDX_TILE = 128
DISP_TM = 256
FIN_TM = IDX_TILE
FIN_BUFS = 3


def _cparams(sem):
    return pltpu.CompilerParams(dimension_semantics=sem, vmem_limit_bytes=VMEM_LIMIT)


def _to_row_tiles(ref, val, base=0):
    rows, n = val.shape[0], val.shape[1] // LANES
    for s in range(n):
        ref[pl.ds(base + s, rows, stride=n), :] = val[:, s * LANES:(s + 1) * LANES]


def _from_row_tiles(ref, n, base=0, rows=None):
    rows = ref.shape[0] // n if rows is None else rows
    return jnp.concatenate([ref[pl.ds(base + s, rows, stride=n), :] for s in range(n)], axis=1)


def _pack_bf16_pairs(x):
    c = x.shape[1] // 2
    as_bits = lambda v: lax.bitcast_convert_type(v.astype(BF16).astype(F32), jnp.uint32)
    return (as_bits(x[:, c:]) & jnp.uint32(0xFFFF0000)) | (as_bits(x[:, :c]) >> 16)


def _unpack_bf16_pairs(w):
    lo = lax.bitcast_convert_type(w << 16, F32).astype(BF16)
    hi = lax.bitcast_convert_type(w & jnp.uint32(0xFFFF0000), F32).astype(BF16)
    return jnp.concatenate([lo, hi], axis=1)


def _inproj_kernel(x_ref, w_ref, o_ref):
    o_ref[...] = jnp.dot(x_ref[...].astype(BF16), w_ref[...], preferred_element_type=F32)


def _inproj(x2d, w):
    t = x2d.shape[0]
    tm = min(INPROJ_TM, t)
    return pl.pallas_call(
        _inproj_kernel,
        grid=(IN_COLS // INPROJ_TN, t // tm),
        in_specs=[pl.BlockSpec((tm, D_MODEL), lambda j, i: (i, 0)),
                  pl.BlockSpec((D_MODEL, INPROJ_TN), lambda j, i: (0, j))],
        out_specs=pl.BlockSpec((tm, INPROJ_TN), lambda j, i: (i, j)),
        out_shape=jax.ShapeDtypeStruct((t, IN_COLS), F32),
        compiler_params=_cparams(("arbitrary", "arbitrary")),
        name="inproj",
    )(x2d, w)


def _attn_kernel(sink_ref, q_ref, kv_ref, cos_ref, sa_ref, sb_ref, gain_ref, o_ref, kprev, vprev):
    n = pl.program_id(1)

    @pl.when(n == 0)
    def _():
        kprev[...] = jnp.zeros_like(kprev)
        vprev[...] = jnp.zeros_like(vprev)

    cos = cos_ref[...]
    sa = sa_ref[...]
    sb = sb_ref[...]

    def rot(t):
        return t * cos + pltpu.roll(t, 8, axis=1) * sa + pltpu.roll(t, LANES - 8, axis=1) * sb

    kc = rot(kv_ref[:, 0:KV_WIDTH])
    vc = kv_ref[:, KV_WIDTH:2 * KV_WIDTH]
    kb = jnp.concatenate([kprev[...], kc], axis=0)
    vb = jnp.concatenate([vprev[...], vc], axis=0)
    kb_sw = pltpu.roll(kb, HEAD_DIM, axis=1)
    vb_sw = pltpu.roll(vb, HEAD_DIM, axis=1)
    nkeys = 2 * ATTN_BLOCK
    low = lax.broadcasted_iota(jnp.int32, (nkeys, LANES), 1) < HEAD_DIM
    low_q = lax.broadcasted_iota(jnp.int32, (ATTN_BLOCK, LANES), 1) < HEAD_DIM

    k2, v2 = [], []
    for kvh in range(KV_HEADS):
        k_lo, k_hi = (kb, kb_sw) if kvh == 0 else (kb_sw, kb)
        v_lo, v_hi = (vb, vb_sw) if kvh == 0 else (vb_sw, vb)
        k2.append(jnp.concatenate([jnp.where(low, k_lo, 0.0), jnp.where(low, 0.0, k_hi)], axis=0).astype(BF16))
        top = jnp.concatenate([jnp.where(low, v_lo, 0.0), jnp.where(low, 1.0, 0.0)], axis=1)
        bot = jnp.concatenate([jnp.where(low, 0.0, v_hi), jnp.where(low, 0.0, 1.0)], axis=1)
        v2.append(jnp.concatenate([top, bot], axis=0).astype(BF16))

    row = lax.broadcasted_iota(jnp.int32, (ATTN_BLOCK, 2 * nkeys), 0)
    col = lax.broadcasted_iota(jnp.int32, (ATTN_BLOCK, 2 * nkeys), 1) % nkeys
    rel = row + ATTN_BLOCK - col
    allowed = (rel >= 0) & (rel < ATTN_BLOCK) & ((col >= ATTN_BLOCK) | (n > 0))

    outs = []
    for pair in range(N_HEADS // 2):
        kvh = (2 * pair) // Q_PER_KV
        qt = (rot(q_ref[:, pair * LANES:(pair + 1) * LANES]) * (HEAD_DIM ** -0.5)).astype(BF16)
        s = lax.dot_general(qt, k2[kvh], (((1,), (1,)), ((), ())), preferred_element_type=F32)
        s = jnp.where(allowed, s, NEG_BIG)
        sink_a = sink_ref[2 * pair]
        sink_b = sink_ref[2 * pair + 1]
        m_a = jnp.maximum(jnp.max(s[:, :nkeys], axis=-1, keepdims=True), sink_a)
        m_b = jnp.maximum(jnp.max(s[:, nkeys:], axis=-1, keepdims=True), sink_b)
        pr = jnp.concatenate([jnp.exp(s[:, :nkeys] - m_a), jnp.exp(s[:, nkeys:] - m_b)], axis=1)
        acc = jnp.dot(pr.astype(BF16), v2[kvh], preferred_element_type=F32)
        sink_term = jnp.where(low_q, jnp.exp(sink_a - m_a), jnp.exp(sink_b - m_b))
        outs.append(acc[:, :LANES] / (acc[:, LANES:] + sink_term))
    o = jnp.concatenate(outs, axis=1)
    y = o * lax.rsqrt(jnp.mean(o * o, axis=-1, keepdims=True) + RMS_EPS) * gain_ref[...]
    o_ref[...] = y.astype(o_ref.dtype)
    kprev[...] = kc
    vprev[...] = vc


def _attention(h, sinks, cos_t, sa_t, sb_t, gain, bsz, seq):
    t = bsz * seq
    nb = seq // ATTN_BLOCK
    rowmap = lambda b, n: (b * nb + n, 0)
    return pl.pallas_call(
        _attn_kernel,
        grid=(bsz, nb),
        in_specs=[pl.BlockSpec(memory_space=pltpu.SMEM),
                  pl.BlockSpec((ATTN_BLOCK, ATTN_WIDTH), lambda b, n: (b * nb + n, COL_Q // ATTN_WIDTH)),
                  pl.BlockSpec((ATTN_BLOCK, 2 * KV_WIDTH), lambda b, n: (b * nb + n, COL_KV // (2 * KV_WIDTH))),
                  pl.BlockSpec((ATTN_BLOCK, LANES), rowmap),
                  pl.BlockSpec((ATTN_BLOCK, LANES), rowmap),
                  pl.BlockSpec((ATTN_BLOCK, LANES), rowmap),
                  pl.BlockSpec((1, ATTN_WIDTH), lambda b, n: (0, 0))],
        out_specs=pl.BlockSpec((ATTN_BLOCK, ATTN_WIDTH), rowmap),
        out_shape=jax.ShapeDtypeStruct((t, ATTN_WIDTH), BF16),
        scratch_shapes=[pltpu.VMEM((ATTN_BLOCK, KV_WIDTH), F32), pltpu.VMEM((ATTN_BLOCK, KV_WIDTH), F32)],
        compiler_params=_cparams(("arbitrary", "arbitrary")),
        name="attn",
    )(sinks, h, h, cos_t, sa_t, sb_t, gain)


def _cumsum_rows(x):
    rows = x.shape[0]
    row = lax.broadcasted_iota(jnp.int32, x.shape, 0)
    d = 1
    while d < rows:
        x = x + jnp.where(row >= d, pltpu.roll(x, d, axis=0), 0.0)
        d *= 2
    return x


def _hgrn_kernel(in_ref, lb_ref, gain_ref, o_ref, st_ref):
    r = pl.program_id(2)

    @pl.when(r == 0)
    def _():
        st_ref[...] = jnp.zeros_like(st_ref)

    c = HG_CHUNK
    nsub = c // HG_SUB
    ones_kk = jnp.ones((HG_HEAD_DIM, HG_HEAD_DIM), BF16)
    row_c = lax.broadcasted_iota(jnp.int32, (c, HG_HEAD_DIM), 0)
    row_s = lax.broadcasted_iota(jnp.int32, (HG_SUB, HG_HEAD_DIM), 0)
    nt = (((1,), (1,)), ((), ()))
    tn = (((0,), (0,)), ((), ()))

    def one_head(r0, hh):
        col = hh * 4 * HG_HEAD_DIM
        hcol = slice(hh * HG_HEAD_DIM, (hh + 1) * HG_HEAD_DIM)
        lb = lb_ref[:, hcol]
        gain = gain_ref[:, hcol]
        q = in_ref[pl.ds(r0, c), col:col + HG_HEAD_DIM]
        fp = in_ref[pl.ds(r0, c), col + HG_HEAD_DIM:col + 2 * HG_HEAD_DIM]
        v = in_ref[pl.ds(r0, c), col + 2 * HG_HEAD_DIM:col + 3 * HG_HEAD_DIM]
        g = in_ref[pl.ds(r0, c), col + 3 * HG_HEAD_DIM:col + 4 * HG_HEAD_DIM]
        qf = jax.nn.silu(q)
        f = lb + (1.0 - lb) * jax.nn.sigmoid(fp)
        logf = jnp.log(jnp.maximum(f, TINY))
        kf = (1.0 - lb) * jax.nn.sigmoid(-fp)
        b = _cumsum_rows(logf)
        vb = v.astype(BF16)

        s_rows = [jnp.zeros((HG_SUB, c), F32)]
        for i in range(1, nsub):
            lo = i * HG_SUB
            ref_b = b[lo - 1:lo, :]
            qi = qf[lo:lo + HG_SUB, :] * jnp.exp(b[lo:lo + HG_SUB, :] - ref_b)
            ki = kf * jnp.exp(jnp.where(row_c < lo, ref_b - b, NEG_BIG))
            s_rows.append(lax.dot_general(qi.astype(BF16), ki.astype(BF16), nt, preferred_element_type=F32))
        scores = jnp.concatenate(s_rows, axis=0)
        o = jnp.dot(scores.astype(BF16), vb, preferred_element_type=F32)

        o_diag = []
        for j in range(nsub):
            lo = j * HG_SUB
            bs = b[lo:lo + HG_SUB, :]
            ks = kf[lo:lo + HG_SUB, :]
            qs = qf[lo:lo + HG_SUB, :]
            vs = v[lo:lo + HG_SUB, :]
            tiles = []
            for tt in range(HG_SUB):
                e = jnp.where(row_s <= tt, bs[tt:tt + 1, :] - bs, NEG_BIG)
                tiles.append(jnp.exp(e) * ks * qs[tt:tt + 1, :])
            w = jnp.concatenate(tiles, axis=0).astype(BF16)
            dfull = jnp.dot(w, ones_kk, preferred_element_type=F32)
            contrib = dfull.reshape(HG_SUB, HG_SUB, HG_HEAD_DIM) * vs[None, :, :]
            o_diag.append(jnp.sum(contrib, axis=1))
        o = o + jnp.concatenate(o_diag, axis=0)

        st = st_ref[hh]
        qb = (qf * jnp.exp(b)).astype(BF16)
        o = o + lax.dot_general(qb, st.astype(BF16), nt, preferred_element_type=F32)
        b_last = b[c - 1:c, :]
        kn = (kf * jnp.exp(b_last - b)).astype(BF16)
        st_ref[hh] = st * jnp.exp(b_last) + lax.dot_general(vb, kn, tn, preferred_element_type=F32)

        y = o * lax.rsqrt(jnp.mean(o * o, axis=-1, keepdims=True) + RMS_EPS) * gain
        o_ref[pl.ds(r0, c), hcol] = (y * jax.nn.silu(g)).astype(o_ref.dtype)

    def chunk(ci, carry):
        r0 = pl.multiple_of(ci * c, c)
        for hh in range(st_ref.shape[0]):
            one_head(r0, hh)
        return carry

    lax.fori_loop(0, in_ref.shape[0] // c, chunk, 0)


def _hgrn2(h, lb, gain, bsz, seq):
    t = bsz * seq
    rows = min(HG_ROWS, seq)
    nr = seq // rows
    hd_block0 = COL_HG // (4 * HG_HEAD_DIM)
    grp = HG_GROUP
    return pl.pallas_call(
        _hgrn_kernel,
        grid=(bsz, HG_HEADS // grp, nr),
        in_specs=[pl.BlockSpec((rows, grp * 4 * HG_HEAD_DIM), lambda b, hd, r: (b * nr + r, hd_block0 // grp + hd)),
                  pl.BlockSpec((1, grp * HG_HEAD_DIM), lambda b, hd, r: (0, hd)),
                  pl.BlockSpec((1, grp * HG_HEAD_DIM), lambda b, hd, r: (0, hd))],
        out_specs=pl.BlockSpec((rows, grp * HG_HEAD_DIM), lambda b, hd, r: (b * nr + r, hd)),
        out_shape=jax.ShapeDtypeStruct((t, HG_WIDTH), BF16),
        scratch_shapes=[pltpu.VMEM((grp, HG_HEAD_DIM, HG_HEAD_DIM), F32)],
        compiler_params=_cparams(("arbitrary", "arbitrary", "arbitrary")),
        name="hgrn2",
    )(h, lb, gain)


def _lru_kernel(in_ref, cw_ref, cb_ref, wa_ref, ba_ref, wx_ref, bx_ref, lam_ref, gain_ref, o_ref,
                tail_ref, h_ref):
    r = pl.program_id(1)

    @pl.when(r == 0)
    def _():
        tail_ref[...] = jnp.zeros_like(tail_ref)
        h_ref[...] = jnp.zeros_like(h_ref)

    rows = in_ref.shape[0]
    x = in_ref[:, 0:LRU_WIDTH]
    gr = in_ref[:, LRU_WIDTH:2 * LRU_WIDTH]
    tail = tail_ref[...]
    row8 = lax.broadcasted_iota(jnp.int32, (8, LRU_WIDTH), 0)

    xc = x * cw_ref[CONV_WIDTH - 1:CONV_WIDTH, :] + cb_ref[...]
    for k in range(1, CONV_WIDTH):
        xs = pltpu.roll(x, k, axis=0)
        head = jnp.where(row8 < k, pltpu.roll(tail, k, axis=0), xs[0:8, :])
        xs = jnp.concatenate([head, xs[8:, :]], axis=0)
        xc = xc + xs * cw_ref[CONV_WIDTH - 1 - k:CONV_WIDTH - k, :]
    tail_ref[...] = x[rows - 8:rows, :]

    xcb = xc.astype(BF16)
    rg = jax.nn.sigmoid(jnp.dot(xcb, wa_ref[...], preferred_element_type=F32) + ba_ref[...])
    ig = jax.nn.sigmoid(jnp.dot(xcb, wx_ref[...], preferred_element_type=F32) + bx_ref[...])
    log_a = -LRU_C * rg * jax.nn.softplus(-lam_ref[...])
    a = jnp.exp(log_a)
    th = jnp.tanh(log_a)
    neg_expm1 = -2.0 * th / (1.0 - th)
    u = jnp.sqrt(jnp.maximum(neg_expm1, 0.0)) * (ig * xc)

    row = lax.broadcasted_iota(jnp.int32, (rows, LRU_WIDTH), 0)
    d = 1
    while d < rows:
        keep = row >= d
        a_s = jnp.where(keep, pltpu.roll(a, d, axis=0), 1.0)
        u_s = jnp.where(keep, pltpu.roll(u, d, axis=0), 0.0)
        u = a * u_s + u
        a = a * a_s
        d *= 2
    hcur = u + a * h_ref[0:1, :]
    h_ref[...] = jnp.broadcast_to(hcur[rows - 1:rows, :], h_ref.shape)

    y = hcur * lax.rsqrt(jnp.mean(hcur * hcur, axis=-1, keepdims=True) + RMS_EPS) * gain_ref[...]
    o_ref[...] = (y * jax.nn.gelu(gr)).astype(o_ref.dtype)


def _rglru(h, cw, cb, wa, ba, wx, bx, lam, gain, bsz, seq):
    t = bsz * seq
    rows = min(LRU_ROWS, seq)
    nr = seq // rows
    vec = pl.BlockSpec((1, LRU_WIDTH), lambda b, r: (0, 0))
    mat = pl.BlockSpec((LRU_WIDTH, LRU_WIDTH), lambda b, r: (0, 0))
    return pl.pallas_call(
        _lru_kernel,
        grid=(bsz, nr),
        in_specs=[pl.BlockSpec((rows, 2 * LRU_WIDTH), lambda b, r: (b * nr + r, COL_LRU // (2 * LRU_WIDTH))),
                  pl.BlockSpec((CONV_WIDTH, LRU_WIDTH), lambda b, r: (0, 0)),
                  vec, mat, vec, mat, vec, vec, vec],
        out_specs=pl.BlockSpec((rows, LRU_WIDTH), lambda b, r: (b * nr + r, 0)),
        out_shape=jax.ShapeDtypeStruct((t, LRU_WIDTH), BF16),
        scratch_shapes=[pltpu.VMEM((8, LRU_WIDTH), F32), pltpu.VMEM((8, LRU_WIDTH), F32)],
        compiler_params=_cparams(("arbitrary", "arbitrary")),
        name="rglru",
    )(h, cw, cb, wa, ba, wx, bx, lam, gain)


def _layer_norm(z, g, b):
    mu = jnp.mean(z, axis=-1, keepdims=True)
    zc = z - mu
    var = jnp.mean(zc * zc, axis=-1, keepdims=True)
    return zc * lax.rsqrt(var + LN_EPS) * g + b


def _outproj_kernel(ya_ref, yh_ref, yl_ref, x_ref, wa_ref, wh_ref, wl_ref, g_ref, b_ref, rw_ref, rb_ref,
                    x1_ref, idx_ref, rank_ref, gate_ref, counts_ref, carry_ref):
    i = pl.program_id(0)

    @pl.when(i == 0)
    def _():
        carry_ref[...] = jnp.zeros_like(carry_ref)

    mixed = jnp.dot(ya_ref[...], wa_ref[...], preferred_element_type=F32)
    mixed = mixed + jnp.dot(yh_ref[...], wh_ref[...], preferred_element_type=F32)
    mixed = mixed + jnp.dot(yl_ref[...], wl_ref[...], preferred_element_type=F32)
    x1 = _layer_norm(DEEPNORM_ALPHA * x_ref[...] + mixed, g_ref[...], b_ref[...])
    x1_ref[...] = x1

    tm = x1.shape[0]
    logits = lax.dot_general(rw_ref[...], x1.astype(BF16), (((1,), (1,)), ((), ())), preferred_element_type=F32)
    scores = jax.nn.sigmoid(logits)
    sel = scores + rb_ref[...]
    shp = (N_GROUPS, EXPERTS_PER_GROUP, tm)
    s3 = sel.reshape(shp)
    sc3 = scores.reshape(shp)
    e_in = lax.broadcasted_iota(jnp.int32, shp, 1)
    g_id = lax.broadcasted_iota(jnp.int32, shp, 0)
    e_id = g_id * EXPERTS_PER_GROUP + e_in

    m1 = jnp.max(s3, axis=1, keepdims=True)
    i1 = jnp.min(jnp.where(s3 == m1, e_in, EXPERTS_PER_GROUP), axis=1, keepdims=True)
    m2 = jnp.max(jnp.where(e_in == i1, NEG_PICKED, s3), axis=1, keepdims=True)
    gs = m1 + m2
    g1 = lax.broadcasted_iota(jnp.int32, gs.shape, 0)
    gsel = jnp.zeros(gs.shape, jnp.int32)
    cur = gs
    for _ in range(TOPK_GROUPS):
        m = jnp.max(cur, axis=0, keepdims=True)
        pick = g1 == jnp.min(jnp.where(cur == m, g1, N_GROUPS), axis=0, keepdims=True)
        gsel = jnp.where(pick, 1, gsel)
        cur = jnp.where(pick, NEG_PICKED, cur)
    cur = jnp.where(gsel > 0, s3, NEG_BIG)

    def pick_sum(pick, vals):
        return jnp.sum(jnp.sum(jnp.where(pick, vals, 0.0), axis=1, keepdims=True), axis=0, keepdims=True).reshape(1, tm)

    idx_rows, w_rows, picks = [], [], []
    onehot = jnp.zeros(shp, F32)
    for _ in range(TOPK):
        m = jnp.max(jnp.max(cur, axis=1, keepdims=True), axis=0, keepdims=True)
        cand = jnp.where(cur == m, e_id, N_EXPERTS)
        ii = jnp.min(jnp.min(cand, axis=1, keepdims=True), axis=0, keepdims=True)
        pick = e_id == ii
        picks.append(pick)
        onehot = jnp.where(pick, 1.0, onehot)
        w_rows.append(pick_sum(pick, sc3))
        idx_rows.append(ii.reshape(1, tm))
        cur = jnp.where(pick, NEG_PICKED, cur)
    w = jnp.concatenate(w_rows, axis=0)
    idx = jnp.concatenate(idx_rows, axis=0)
    gates = w / jnp.sum(w, axis=0, keepdims=True) * ROUTED_SCALE

    oh = onehot.reshape(N_EXPERTS, tm).astype(BF16)
    r_i = lax.broadcasted_iota(jnp.int32, (tm, tm), 0)
    c_i = lax.broadcasted_iota(jnp.int32, (tm, tm), 1)
    before = jnp.where(r_i < c_i, 1.0, 0.0).astype(BF16)
    carry = carry_ref[...]
    prefix3 = (jnp.dot(oh, before, preferred_element_type=F32) + carry).reshape(shp)
    rank = jnp.concatenate([pick_sum(pk, prefix3) for pk in picks], axis=0).astype(jnp.int32)
    carry = carry + jnp.dot(oh, jnp.ones((tm, tm), BF16), preferred_element_type=F32)
    carry_ref[...] = carry
    counts_ref[...] = carry.astype(jnp.int32)

    for c in range(tm // IDX_TILE):
        sl = slice(c * IDX_TILE, (c + 1) * IDX_TILE)
        idx_ref[c] = idx[:, sl]
        rank_ref[c] = rank[:, sl]
        gate_ref[c] = gates[:, sl]


def _outproj_router(ya, yh, yl, x2d, w_out, ln_g, ln_b, rw_t, rb):
    t = x2d.shape[0]
    tm = min(OUT_TM, t)
    nt = tm // IDX_TILE
    rowb = lambda w: pl.BlockSpec((tm, w), lambda i: (i, 0))
    full = lambda a, b: pl.BlockSpec((a, b), lambda i: (0, 0))
    tiles = pl.BlockSpec((nt, TOPK, IDX_TILE), lambda i: (i, 0, 0))
    wa = pl.BlockSpec((ATTN_WIDTH, D_MODEL), lambda i: (0, 0))
    wh = pl.BlockSpec((HG_WIDTH, D_MODEL), lambda i: (ATTN_WIDTH // HG_WIDTH, 0))
    wl = pl.BlockSpec((LRU_WIDTH, D_MODEL), lambda i: ((ATTN_WIDTH + HG_WIDTH) // LRU_WIDTH, 0))
    tile_shape = lambda dt: jax.ShapeDtypeStruct((t // IDX_TILE, TOPK, IDX_TILE), dt)
    return pl.pallas_call(
        _outproj_kernel,
        grid=(t // tm,),
        in_specs=[rowb(ATTN_WIDTH), rowb(HG_WIDTH), rowb(LRU_WIDTH), rowb(D_MODEL), wa, wh, wl,
                  full(1, D_MODEL), full(1, D_MODEL), full(N_EXPERTS, D_MODEL), full(N_EXPERTS, 1)],
        out_specs=[rowb(D_MODEL), tiles, tiles, tiles, full(N_EXPERTS, tm)],
        out_shape=[jax.ShapeDtypeStruct((t, D_MODEL), F32),
                   tile_shape(jnp.int32), tile_shape(jnp.int32), tile_shape(F32),
                   jax.ShapeDtypeStruct((N_EXPERTS, tm), jnp.int32)],
        scratch_shapes=[pltpu.VMEM((N_EXPERTS, tm), F32)],
        compiler_params=_cparams(("arbitrary",)),
        name="outproj_router",
    )(ya, yh, yl, x2d, w_out, w_out, w_out, ln_g, ln_b, rw_t, rb)


def _dispatch_kernel(pend_ref, padded_ref, dest_hbm, x_ref, p_ref, sw1_ref, sw3_ref, sw2_ref, pw_ref,
                     gw_ref, gb_ref, xs_hbm, sp_ref, dest_smem, zeros_ref, rows_ref, sem_idx, sem_rows, sem_zero):
    i = pl.program_id(0)
    tm = x_ref.shape[0]
    bm = zeros_ref.shape[0] // PACK_TILES
    nt = tm // IDX_TILE

    @pl.when(i == 0)
    def _():
        zeros_ref[...] = jnp.zeros_like(zeros_ref)

        def fill(e, carry):
            @pl.when(padded_ref[e] > 0)
            def _():
                start = pl.multiple_of(pend_ref[e] - bm, bm)
                pltpu.make_async_copy(zeros_ref, xs_hbm.at[pl.ds(start * PACK_TILES, bm * PACK_TILES)], sem_zero).start()
            return carry

        def fill_wait(e, carry):
            @pl.when(padded_ref[e] > 0)
            def _():
                pltpu.make_async_copy(zeros_ref, xs_hbm.at[pl.ds(0, bm * PACK_TILES)], sem_zero).wait()
            return carry

        def fill_tail(blk, carry):
            start = pl.multiple_of(blk * bm, bm)
            pltpu.make_async_copy(zeros_ref, xs_hbm.at[pl.ds(start * PACK_TILES, bm * PACK_TILES)], sem_zero).start()
            return carry

        def fill_tail_wait(blk, carry):
            pltpu.make_async_copy(zeros_ref, xs_hbm.at[pl.ds(0, bm * PACK_TILES)], sem_zero).wait()
            return carry

        n_used = pend_ref[N_EXPERTS - 1] // bm
        n_blocks = xs_hbm.shape[0] // (bm * PACK_TILES)
        lax.fori_loop(0, N_EXPERTS, fill, 0)
        lax.fori_loop(n_used, n_blocks, fill_tail, 0)
        lax.fori_loop(0, N_EXPERTS, fill_wait, 0)
        lax.fori_loop(n_used, n_blocks, fill_tail_wait, 0)

    n = pl.num_programs(0)

    def idx_copy(step, s):
        return pltpu.make_async_copy(dest_hbm.at[pl.ds(step * nt, nt)], dest_smem.at[s], sem_idx.at[s])

    def wait_rows(s):
        n_copied = TOPK * tm * PACK_TILES
        pltpu.make_async_copy(xs_hbm.at[pl.ds(0, n_copied)], xs_hbm.at[pl.ds(0, n_copied)], sem_rows.at[s]).wait()

    @pl.when(i == 0)
    def _():
        idx_copy(0, 0).start()

    def step(slot):
        @pl.when(i + 1 < n)
        def _():
            idx_copy(i + 1, 1 - slot).start()

        x = x_ref[...]
        _to_row_tiles(rows_ref.at[slot], _pack_bf16_pairs(x))
        idx_copy(i, slot).wait()

        for c in range(nt):
            for r in range(IDX_TILE):
                row = c * IDX_TILE + r
                for j in range(TOPK):
                    d = dest_smem[slot, c, j, r]
                    pltpu.make_async_copy(rows_ref.at[slot, pl.ds(row * PACK_TILES, PACK_TILES)],
                                          xs_hbm.at[pl.ds(d * PACK_TILES, PACK_TILES)],
                                          sem_rows.at[slot]).start(priority=j % 2)

        xb = x.astype(BF16)
        h1 = jnp.dot(xb, sw1_ref[...], preferred_element_type=F32)
        h3 = jnp.dot(xb, sw3_ref[...], preferred_element_type=F32)
        shared = jnp.dot((jax.nn.silu(h1) * h3).astype(BF16), sw2_ref[...], preferred_element_type=F32)
        gate = jax.nn.sigmoid(jnp.dot(xb, gw_ref[...], preferred_element_type=F32) + gb_ref[...])
        ple = gate * jnp.dot(p_ref[...].astype(BF16), pw_ref[...], preferred_element_type=F32)
        sp_ref[...] = shared + ple

        @pl.when(i > 0)
        def _():
            wait_rows(1 - slot)

        @pl.when(i == n - 1)
        def _():
            wait_rows(slot)

    for k in range(2):
        pl.when(i % 2 == k)(lambda k=k: step(k))


def _dispatch(layer, pend, padded, dest_tiles, x1, p3, sw1, sw3, sw2, pw, gw, gb, n_rows, bm):
    t = x1.shape[0]
    tm = min(DISP_TM, t)
    full = lambda a, b: pl.BlockSpec((a, b), lambda i, pe, pa: (0, 0))
    grid_spec = pltpu.PrefetchScalarGridSpec(
        num_scalar_prefetch=2,
        grid=(t // tm,),
        in_specs=[pl.BlockSpec(memory_space=pl.ANY),
                  pl.BlockSpec((tm, D_MODEL), lambda i, pe, pa: (i, 0)),
                  pl.BlockSpec((None, tm, PLE_DIM), lambda i, pe, pa: (layer, i, 0)),
                  full(D_MODEL, EXPERT_DIM), full(D_MODEL, EXPERT_DIM), full(EXPERT_DIM, D_MODEL),
                  full(PLE_DIM, D_MODEL), full(D_MODEL, D_MODEL), full(1, D_MODEL)],
        out_specs=[pl.BlockSpec(memory_space=pl.ANY),
                   pl.BlockSpec((tm, D_MODEL), lambda i, pe, pa: (i, 0))],
        scratch_shapes=[pltpu.SMEM((2, tm // IDX_TILE, TOPK, IDX_TILE), jnp.int32),
                        pltpu.VMEM((bm * PACK_TILES, LANES), jnp.uint32),
                        pltpu.VMEM((2, tm * PACK_TILES, LANES), jnp.uint32),
                        pltpu.SemaphoreType.DMA((2,)), pltpu.SemaphoreType.DMA((2,)), pltpu.SemaphoreType.DMA],
    )
    return pl.pallas_call(
        _dispatch_kernel,
        grid_spec=grid_spec,
        out_shape=[jax.ShapeDtypeStruct((n_rows * PACK_TILES, LANES), jnp.uint32),
                   jax.ShapeDtypeStruct((t, D_MODEL), F32)],
        compiler_params=_cparams(("arbitrary",)),
        name="dispatch_shared_ple",
    )(pend, padded, dest_tiles, x1, p3, sw1, sw3, sw2, pw, gw, gb)


def _moe_kernel(be_ref, nused_ref, x_ref, w1_ref, w3_ref, w2_ref, y_ref, w1b, w3b, w2b):
    n = pl.program_id(0)

    @pl.when(n < nused_ref[0])
    def _():
        e = be_ref[n]
        e_prev = be_ref[jnp.maximum(n - 1, 0)]

        @pl.when((n == 0) | (e != e_prev))
        def _():
            w1b[...] = w1_ref[...].astype(BF16)
            w3b[...] = w3_ref[...].astype(BF16)
            w2b[...] = w2_ref[...].astype(BF16)

        sub = min(MOE_SUB, x_ref.shape[0] // PACK_TILES)
        for c in range(x_ref.shape[0] // (PACK_TILES * sub)):
            xb = _unpack_bf16_pairs(_from_row_tiles(x_ref, PACK_TILES, base=c * sub * PACK_TILES, rows=sub))
            h1 = jnp.dot(xb, w1b[...], preferred_element_type=F32)
            h3 = jnp.dot(xb, w3b[...], preferred_element_type=F32)
            act = (jax.nn.silu(h1) * h3).astype(BF16)
            y = jnp.dot(act, w2b[...], preferred_element_type=F32)
            _to_row_tiles(y_ref, _pack_bf16_pairs(y), base=c * sub * PACK_TILES)

    @pl.when(n >= nused_ref[0])
    def _():
        y_ref[...] = jnp.zeros_like(y_ref)


def _moe(layer, block_expert, n_used, xs, w1, w3, w2, bm):
    n_blocks = xs.shape[0] // (bm * PACK_TILES)
    wspec = lambda a, b: pl.BlockSpec((None, None, a, b), lambda n, be, nu: (layer, be[n], 0, 0))
    grid_spec = pltpu.PrefetchScalarGridSpec(
        num_scalar_prefetch=2,
        grid=(n_blocks,),
        in_specs=[pl.BlockSpec((bm * PACK_TILES, LANES), lambda n, be, nu: (jnp.minimum(n, nu[0] - 1), 0)),
                  wspec(D_MODEL, EXPERT_DIM), wspec(D_MODEL, EXPERT_DIM), wspec(EXPERT_DIM, D_MODEL)],
        out_specs=pl.BlockSpec((bm * PACK_TILES, LANES), lambda n, be, nu: (n, 0)),
        scratch_shapes=[pltpu.VMEM((D_MODEL, EXPERT_DIM), BF16),
                        pltpu.VMEM((D_MODEL, EXPERT_DIM), BF16),
                        pltpu.VMEM((EXPERT_DIM, D_MODEL), BF16)],
    )
    return pl.pallas_call(
        _moe_kernel,
        grid_spec=grid_spec,
        out_shape=jax.ShapeDtypeStruct((n_blocks * bm * PACK_TILES, LANES), jnp.uint32),
        compiler_params=_cparams(("arbitrary",)),
        name="moe_experts",
    )(block_expert, n_used, xs, w1, w3, w2)


def _final_kernel(dest_hbm, gate_hbm, y_hbm, x_ref, sp_ref, g_ref, b_ref, o_ref,
                  dest_smem, gate_smem, ybuf, routed, sem_idx, sem_rows):
    i = pl.program_id(0)
    n = pl.num_programs(0)
    tm = x_ref.shape[0]
    nbuf = ybuf.shape[0]
    ahead = nbuf - 1

    def idx_copies(tile, s):
        return (pltpu.make_async_copy(dest_hbm.at[tile], dest_smem.at[s], sem_idx.at[s]),
                pltpu.make_async_copy(gate_hbm.at[tile], gate_smem.at[s], sem_idx.at[s]))

    def start_idx(tile, s):
        for cp in idx_copies(tile, s):
            cp.start()

    def wait_idx(tile, s):
        for cp in idx_copies(tile, s):
            cp.wait()

    def issue_row(s, r):
        for j in range(TOPK):
            d = dest_smem[s, j, r]
            pltpu.make_async_copy(y_hbm.at[pl.ds(d * PACK_TILES, PACK_TILES)],
                                  ybuf.at[s, j, pl.ds(r * PACK_TILES, PACK_TILES)], sem_rows.at[s]).start(priority=j % 2)

    def issue_rows_loop(s):
        def issue(r, carry):
            issue_row(s, r)
            return carry

        lax.fori_loop(0, tm, issue, 0)

    @pl.when(i == 0)
    def _():
        for k in range(ahead):
            @pl.when(k < n)
            def _(k=k):
                start_idx(k, k)
                wait_idx(k, k)
                issue_rows_loop(k)

        @pl.when(ahead < n)
        def _():
            start_idx(ahead, ahead)

    def finish():
        routed2d = jnp.concatenate([_from_row_tiles(routed.at[0], PACK_TILES),
                                    _from_row_tiles(routed.at[1], PACK_TILES)], axis=1)
        z = DEEPNORM_ALPHA * x_ref[...] + routed2d + sp_ref[...]
        o_ref[...] = _layer_norm(z, g_ref[...], b_ref[...])

    def step(slot):
        for j in range(TOPK):
            pltpu.make_async_copy(y_hbm.at[pl.ds(0, tm * PACK_TILES)], ybuf.at[slot, j], sem_rows.at[slot]).wait()

        def combine(r, carry):
            rows = pl.ds(pl.multiple_of(r * PACK_TILES, PACK_TILES), PACK_TILES)
            acc_lo = acc_hi = None
            for j in range(TOPK):
                w = ybuf[slot, j, rows, :]
                g = gate_smem[slot, j, r]
                lo = g * lax.bitcast_convert_type(w << 16, F32)
                hi = g * lax.bitcast_convert_type(w & jnp.uint32(0xFFFF0000), F32)
                acc_lo = lo if acc_lo is None else acc_lo + lo
                acc_hi = hi if acc_hi is None else acc_hi + hi
            routed[0, rows, :] = acc_lo
            routed[1, rows, :] = acc_hi
            return carry

        lax.fori_loop(0, tm, combine, 0, unroll=4)

        @pl.when(i + nbuf < n)
        def _():
            start_idx(i + nbuf, slot)

        @pl.when(i + ahead < n)
        def _():
            s_new = (slot + ahead) % nbuf
            wait_idx(i + ahead, s_new)
            for r in range(tm):
                issue_row(s_new, r)
            finish()

        @pl.when(i + ahead >= n)
        def _():
            finish()

    for k in range(nbuf):
        pl.when(i % nbuf == k)(lambda k=k: step(k))


def _final(dest_tiles, gate_tiles, y_sorted, x1, sp, ln_g, ln_b):
    t = x1.shape[0]
    tm = min(FIN_TM, t)
    full = lambda a, b: pl.BlockSpec((a, b), lambda i: (0, 0))
    rowb = pl.BlockSpec((tm, D_MODEL), lambda i: (i, 0))
    return pl.pallas_call(
        _final_kernel,
        grid=(t // tm,),
        in_specs=[pl.BlockSpec(memory_space=pl.ANY), pl.BlockSpec(memory_space=pl.ANY),
                  pl.BlockSpec(memory_space=pl.ANY), rowb, rowb, full(1, D_MODEL), full(1, D_MODEL)],
        out_specs=rowb,
        out_shape=jax.ShapeDtypeStruct((t, D_MODEL), F32),
        scratch_shapes=[pltpu.SMEM((FIN_BUFS, TOPK, tm), jnp.int32),
                        pltpu.SMEM((FIN_BUFS, TOPK, tm), F32),
                        pltpu.VMEM((FIN_BUFS, TOPK, tm * PACK_TILES, LANES), jnp.uint32),
                        pltpu.VMEM((2, tm * PACK_TILES, LANES), F32),
                        pltpu.SemaphoreType.DMA((FIN_BUFS,)),
                        pltpu.SemaphoreType.DMA((FIN_BUFS,))],
        compiler_params=_cparams(("arbitrary",)),
        name="combine_ln",
    )(dest_tiles, gate_tiles, y_sorted, x1, sp, ln_g, ln_b)


def _dispatch_plan(idx_tiles, rank_tiles, counts, bm, n_blocks):
    padded = ((counts + bm - 1) // bm) * bm
    pend = jnp.cumsum(padded).astype(jnp.int32)
    pstart = pend - padded
    experts = jnp.arange(N_EXPERTS, dtype=jnp.int32)
    start_of = jnp.sum(jnp.where(idx_tiles[..., None] == experts, pstart, 0), axis=-1)
    dest_tiles = (start_of + rank_tiles).astype(jnp.int32)
    block_row = jnp.arange(n_blocks, dtype=jnp.int32) * bm
    block_expert = jnp.minimum(jnp.sum((pend[None, :] <= block_row[:, None]).astype(jnp.int32), axis=1),
                               N_EXPERTS - 1)
    n_used = (pend[-1] // bm).reshape(1)
    return pend, padded.astype(jnp.int32), dest_tiles, block_expert, n_used


def _rotary_lane_tables(positions):
    inv_freq = ROPE_THETA ** (-jnp.arange(0, ROT_DIM, 2, dtype=F32) / ROT_DIM)
    half = ROT_DIM // 2
    dim = jnp.arange(LANES, dtype=jnp.int32) % HEAD_DIM
    lane_freq = jnp.where(dim < ROT_DIM, inv_freq[dim % half], 0.0)
    ang = positions.astype(F32).reshape(-1)[:, None] * lane_freq
    sin = jnp.sin(ang)
    sa = jnp.where((dim >= half) & (dim < ROT_DIM), sin, 0.0)
    sb = jnp.where(dim < half, -sin, 0.0)
    return jnp.cos(ang), sa, sb


def _permute_in_cols(w):
    off = [0]
    for s in IN_SIZES:
        off.append(off[-1] + s)
    aq, ak, av, hq, hf, hi, hg, lx, lg = [w[:, off[k]:off[k + 1]] for k in range(9)]
    parts = []
    for hd in range(HG_HEADS):
        sl = slice(hd * HG_HEAD_DIM, (hd + 1) * HG_HEAD_DIM)
        parts += [hq[:, sl], hf[:, sl], hi[:, sl], hg[:, sl]]
    parts += [aq, lx, lg, ak, av]
    return jnp.concatenate(parts, axis=1)


def _block_diag(w):
    hds, d, _ = w.shape
    eye = jnp.eye(hds, dtype=w.dtype)
    return (eye[:, None, :, None] * w[:, :, None, :]).reshape(hds * d, hds * d)


def kernel(x, p, positions, w_in, w_out, attn_sinks, attn_norm, hg_lb_logits, hg_norm, lru_conv_w, lru_conv_b, lru_wa, lru_ba, lru_wx, lru_bx, lru_lambda, lru_norm, ln1_g, ln1_b, router_w, router_b, exp_w1, exp_w3, exp_w2, sh_w1, sh_w3, sh_w2, ple_w, ple_gate_w, ple_gate_b, ln2_g, ln2_b):
    bsz, seq, _ = x.shape
    t = bsz * seq
    depth = w_in.shape[0]
    lb_sm = jax.nn.softmax(hg_lb_logits.astype(F32), axis=0)
    hg_lb = jnp.maximum(jnp.cumsum(lb_sm, axis=0) - lb_sm[0], 0.0)
    cos_t, sa_t, sb_t = _rotary_lane_tables(positions)
    p3 = p.reshape(depth, t, PLE_DIM)
    bm = min(MOE_BM, t)
    n_blocks = t * TOPK // bm + N_EXPERTS
    row = lambda v: v.reshape(1, -1)

    xc = x.reshape(t, D_MODEL)
    for i in range(depth):
        h = _inproj(xc, _permute_in_cols(w_in[i].astype(BF16)))
        ya = _attention(h, attn_sinks[i], cos_t, sa_t, sb_t, row(attn_norm[i]), bsz, seq)
        yh = _hgrn2(h, row(hg_lb[i]), row(hg_norm[i]), bsz, seq)
        yl = _rglru(h, lru_conv_w[i], row(lru_conv_b[i]), _block_diag(lru_wa[i]).astype(BF16), row(lru_ba[i]),
                    _block_diag(lru_wx[i]).astype(BF16), row(lru_bx[i]), row(lru_lambda[i]), row(lru_norm[i]),
                    bsz, seq)
        x1, idx_tiles, rank_tiles, gate_tiles, counts = _outproj_router(
            ya, yh, yl, xc, w_out[i].astype(BF16), row(ln1_g[i]), row(ln1_b[i]),
            router_w[i].T.astype(BF16), router_b[i].reshape(N_EXPERTS, 1))
        pend, padded, dest_tiles, block_expert, n_used = _dispatch_plan(idx_tiles, rank_tiles, counts[:, 0], bm, n_blocks)
        xs, sp = _dispatch(i, pend, padded, dest_tiles, x1, p3, sh_w1[i].astype(BF16), sh_w3[i].astype(BF16),
                           sh_w2[i].astype(BF16), ple_w[i].astype(BF16), ple_gate_w[i].astype(BF16),
                           row(ple_gate_b[i]), n_blocks * bm, bm)
        y_sorted = _moe(i, block_expert, n_used, xs, exp_w1, exp_w3, exp_w2, bm)
        xc = _final(dest_tiles, gate_tiles, y_sorted, x1, sp, row(ln2_g[i]), row(ln2_b[i]))
    return xc.reshape(bsz, seq, D_MODEL)
```

```python
import jax
import jax.numpy as jnp
from jax import lax
from jax.experimental import pallas as pl
from jax.experimental.pallas import tpu as pltpu

F32 = jnp.float32
BF16 = jnp.bfloat16

D_MODEL = 2048
PLE_DIM = 256
N_HEADS = 16
KV_HEADS = 2
Q_PER_KV = N_HEADS // KV_HEADS
HEAD_DIM = 64
ATTN_WIDTH = N_HEADS * HEAD_DIM
KV_WIDTH = KV_HEADS * HEAD_DIM
ATTN_BLOCK = 128
ROT_DIM = HEAD_DIM // 4
ROPE_THETA = 500000.0
HG_HEADS = 4
HG_HEAD_DIM = 128
HG_WIDTH = HG_HEADS * HG_HEAD_DIM
HG_CHUNK = 64
HG_SUB = 8
LRU_HEADS = 8
LRU_HEAD_DIM = 64
LRU_WIDTH = LRU_HEADS * LRU_HEAD_DIM
CONV_WIDTH = 4
LRU_C = 8.0
IN_SIZES = (ATTN_WIDTH, KV_WIDTH, KV_WIDTH, HG_WIDTH, HG_WIDTH, HG_WIDTH, HG_WIDTH, LRU_WIDTH, LRU_WIDTH)
IN_COLS = sum(IN_SIZES)
MIX_WIDTH = ATTN_WIDTH + HG_WIDTH + LRU_WIDTH
N_EXPERTS = 64
TOPK = 8
N_GROUPS = 8
TOPK_GROUPS = 4
EXPERTS_PER_GROUP = N_EXPERTS // N_GROUPS
EXPERT_DIM = 512
ROUTED_SCALE = 2.5
DEPTH = 2
DEEPNORM_ALPHA = (2 * DEPTH) ** 0.25
RMS_EPS = 1e-6
LN_EPS = 1e-5
NEG_BIG = -1e30
NEG_PICKED = -3e38
TINY = 1e-30

LANES = 128
ROW_TILES = D_MODEL // LANES
PACK_TILES = ROW_TILES // 2
VMEM_LIMIT = 56 * 1024 * 1024

COL_HG = 0
COL_Q = COL_HG + 4 * HG_WIDTH
COL_LRU = COL_Q + ATTN_WIDTH
COL_KV = COL_LRU + 2 * LRU_WIDTH

INPROJ_TM = 512
INPROJ_TN = IN_COLS // 2
ATTN_STEP_BLOCKS = 2
HG_ROWS = 512
HG_GROUP = 4
LRU_ROWS = 256
OUT_TM = 512
MOE_BM = 512
MOE_SUB = 512
IDX_TILE = 128
DISP_TM = 256
FIN_TM = IDX_TILE
FIN_BUFS = 3


def _cparams(sem):
    return pltpu.CompilerParams(dimension_semantics=sem, vmem_limit_bytes=VMEM_LIMIT)


def _to_row_tiles(ref, val, base=0):
    rows, n = val.shape[0], val.shape[1] // LANES
    for s in range(n):
        ref[pl.ds(base + s, rows, stride=n), :] = val[:, s * LANES:(s + 1) * LANES]


def _from_row_tiles(ref, n, base=0, rows=None):
    rows = ref.shape[0] // n if rows is None else rows
    return jnp.concatenate([ref[pl.ds(base + s, rows, stride=n), :] for s in range(n)], axis=1)


def _pack_bf16_pairs(x):
    c = x.shape[1] // 2
    as_bits = lambda v: lax.bitcast_convert_type(v.astype(BF16).astype(F32), jnp.uint32)
    return (as_bits(x[:, c:]) & jnp.uint32(0xFFFF0000)) | (as_bits(x[:, :c]) >> 16)


def _unpack_bf16_pairs(w):
    lo = lax.bitcast_convert_type(w << 16, F32).astype(BF16)
    hi = lax.bitcast_convert_type(w & jnp.uint32(0xFFFF0000), F32).astype(BF16)
    return jnp.concatenate([lo, hi], axis=1)


def _inproj_kernel(x_ref, w_ref, o_ref):
    o_ref[...] = jnp.dot(x_ref[...].astype(BF16), w_ref[...], preferred_element_type=F32)


def _inproj(x2d, w):
    t = x2d.shape[0]
    tm = min(INPROJ_TM, t)
    return pl.pallas_call(
        _inproj_kernel,
        grid=(IN_COLS // INPROJ_TN, t // tm),
        in_specs=[pl.BlockSpec((tm, D_MODEL), lambda j, i: (i, 0)),
                  pl.BlockSpec((D_MODEL, INPROJ_TN), lambda j, i: (0, j))],
        out_specs=pl.BlockSpec((tm, INPROJ_TN), lambda j, i: (i, j)),
        out_shape=jax.ShapeDtypeStruct((t, IN_COLS), F32),
        compiler_params=_cparams(("arbitrary", "arbitrary")),
        name="inproj",
    )(x2d, w)


def _attn_kernel(sink_ref, q_ref, kv_ref, cos_ref, sa_ref, sb_ref, gain_ref, o_ref, kprev, vprev):
    n = pl.program_id(1)

    @pl.when(n == 0)
    def _():
        kprev[...] = jnp.zeros_like(kprev)
        vprev[...] = jnp.zeros_like(vprev)

    kp, vp, has_prev = kprev[...], vprev[...], n > 0
    for blk in range(q_ref.shape[0] // ATTN_BLOCK):
        kp, vp = _attn_block(blk * ATTN_BLOCK, kp, vp, has_prev, sink_ref, q_ref, kv_ref, cos_ref, sa_ref, sb_ref,
                             gain_ref, o_ref)
        has_prev = True
    kprev[...] = kp
    vprev[...] = vp


def _attn_block(r0, kp, vp, has_prev, sink_ref, q_ref, kv_ref, cos_ref, sa_ref, sb_ref, gain_ref, o_ref):
    rows = slice(r0, r0 + ATTN_BLOCK)
    cos = cos_ref[rows, :]
    sa = sa_ref[rows, :]
    sb = sb_ref[rows, :]

    def rot(t):
        return t * cos + pltpu.roll(t, 8, axis=1) * sa + pltpu.roll(t, LANES - 8, axis=1) * sb

    kc = rot(kv_ref[rows, 0:KV_WIDTH])
    vc = kv_ref[rows, KV_WIDTH:2 * KV_WIDTH]
    kb = jnp.concatenate([kp, kc], axis=0)
    vb = jnp.concatenate([vp, vc], axis=0)
    kb_sw = pltpu.roll(kb, HEAD_DIM, axis=1)
    vb_sw = pltpu.roll(vb, HEAD_DIM, axis=1)
    nkeys = 2 * ATTN_BLOCK
    low = lax.broadcasted_iota(jnp.int32, (nkeys, LANES), 1) < HEAD_DIM
    low_q = lax.broadcasted_iota(jnp.int32, (ATTN_BLOCK, LANES), 1) < HEAD_DIM

    k2, v2 = [], []
    for kvh in range(KV_HEADS):
        k_lo, k_hi = (kb, kb_sw) if kvh == 0 else (kb_sw, kb)
        v_lo, v_hi = (vb, vb_sw) if kvh == 0 else (vb_sw, vb)
        k2.append(jnp.concatenate([jnp.where(low, k_lo, 0.0), jnp.where(low, 0.0, k_hi)], axis=0).astype(BF16))
        top = jnp.concatenate([jnp.where(low, v_lo, 0.0), jnp.where(low, 1.0, 0.0)], axis=1)
        bot = jnp.concatenate([jnp.where(low, 0.0, v_hi), jnp.where(low, 0.0, 1.0)], axis=1)
        v2.append(jnp.concatenate([top, bot], axis=0).astype(BF16))

    row = lax.broadcasted_iota(jnp.int32, (ATTN_BLOCK, 2 * nkeys), 0)
    col = lax.broadcasted_iota(jnp.int32, (ATTN_BLOCK, 2 * nkeys), 1) % nkeys
    rel = row + ATTN_BLOCK - col
    allowed = (rel >= 0) & (rel < ATTN_BLOCK) & ((col >= ATTN_BLOCK) | has_prev)

    outs = []
    for pair in range(N_HEADS // 2):
        kvh = (2 * pair) // Q_PER_KV
        qt = (rot(q_ref[rows, pair * LANES:(pair + 1) * LANES]) * (HEAD_DIM ** -0.5)).astype(BF16)
        s = lax.dot_general(qt, k2[kvh], (((1,), (1,)), ((), ())), preferred_element_type=F32)
        s = jnp.where(allowed, s, NEG_BIG)
        sink_a = sink_ref[2 * pair]
        sink_b = sink_ref[2 * pair + 1]
        m_a = jnp.maximum(jnp.max(s[:, :nkeys], axis=-1, keepdims=True), sink_a)
        m_b = jnp.maximum(jnp.max(s[:, nkeys:], axis=-1, keepdims=True), sink_b)
        pr = jnp.concatenate([jnp.exp(s[:, :nkeys] - m_a), jnp.exp(s[:, nkeys:] - m_b)], axis=1)
        acc = jnp.dot(pr.astype(BF16), v2[kvh], preferred_element_type=F32)
        sink_term = jnp.where(low_q, jnp.exp(sink_a - m_a), jnp.exp(sink_b - m_b))
        outs.append(acc[:, :LANES] / (acc[:, LANES:] + sink_term))
    o = jnp.concatenate(outs, axis=1)
    y = o * lax.rsqrt(jnp.mean(o * o, axis=-1, keepdims=True) + RMS_EPS) * gain_ref[...]
    o_ref[rows, :] = y.astype(o_ref.dtype)
    return kc, vc


def _attention(h, sinks, cos_t, sa_t, sb_t, gain, bsz, seq):
    t = bsz * seq
    rows = ATTN_STEP_BLOCKS * ATTN_BLOCK
    nb = seq // rows
    rowmap = lambda b, n: (b * nb + n, 0)
    return pl.pallas_call(
        _attn_kernel,
        grid=(bsz, nb),
        in_specs=[pl.BlockSpec(memory_space=pltpu.SMEM),
                  pl.BlockSpec((rows, ATTN_WIDTH), lambda b, n: (b * nb + n, COL_Q // ATTN_WIDTH)),
                  pl.BlockSpec((rows, 2 * KV_WIDTH), lambda b, n: (b * nb + n, COL_KV // (2 * KV_WIDTH))),
                  pl.BlockSpec((rows, LANES), rowmap),
                  pl.BlockSpec((rows, LANES), rowmap),
                  pl.BlockSpec((rows, LANES), rowmap),
                  pl.BlockSpec((1, ATTN_WIDTH), lambda b, n: (0, 0))],
        out_specs=pl.BlockSpec((rows, ATTN_WIDTH), rowmap),
        out_shape=jax.ShapeDtypeStruct((t, ATTN_WIDTH), BF16),
        scratch_shapes=[pltpu.VMEM((ATTN_BLOCK, KV_WIDTH), F32), pltpu.VMEM((ATTN_BLOCK, KV_WIDTH), F32)],
        compiler_params=_cparams(("arbitrary", "arbitrary")),
        name="attn",
    )(sinks, h, h, cos_t, sa_t, sb_t, gain)


def _cumsum_rows(x):
    rows = x.shape[0]
    row = lax.broadcasted_iota(jnp.int32, x.shape, 0)
    d = 1
    while d < rows:
        x = x + jnp.where(row >= d, pltpu.roll(x, d, axis=0), 0.0)
        d *= 2
    return x


def _hgrn_kernel(in_ref, lb_ref, gain_ref, o_ref, st_ref):
    r = pl.program_id(2)

    @pl.when(r == 0)
    def _():
        st_ref[...] = jnp.zeros_like(st_ref)

    c = HG_CHUNK
    nsub = c // HG_SUB
    ones_kk = jnp.ones((HG_HEAD_DIM, HG_HEAD_DIM), BF16)
    row_c = lax.broadcasted_iota(jnp.int32, (c, HG_HEAD_DIM), 0)
    row_s = lax.broadcasted_iota(jnp.int32, (HG_SUB, HG_HEAD_DIM), 0)
    nt = (((1,), (1,)), ((), ()))
    tn = (((0,), (0,)), ((), ()))

    def one_head(r0, hh):
        col = hh * 4 * HG_HEAD_DIM
        hcol = slice(hh * HG_HEAD_DIM, (hh + 1) * HG_HEAD_DIM)
        lb = lb_ref[:, hcol]
        gain = gain_ref[:, hcol]
        q = in_ref[pl.ds(r0, c), col:col + HG_HEAD_DIM]
        fp = in_ref[pl.ds(r0, c), col + HG_HEAD_DIM:col + 2 * HG_HEAD_DIM]
        v = in_ref[pl.ds(r0, c), col + 2 * HG_HEAD_DIM:col + 3 * HG_HEAD_DIM]
        g = in_ref[pl.ds(r0, c), col + 3 * HG_HEAD_DIM:col + 4 * HG_HEAD_DIM]
        qf = jax.nn.silu(q)
        f = lb + (1.0 - lb) * jax.nn.sigmoid(fp)
        logf = jnp.log(jnp.maximum(f, TINY))
        kf = (1.0 - lb) * jax.nn.sigmoid(-fp)
        b = _cumsum_rows(logf)
        vb = v.astype(BF16)

        s_rows = [jnp.zeros((HG_SUB, c), F32)]
        for i in range(1, nsub):
            lo = i * HG_SUB
            ref_b = b[lo - 1:lo, :]
            qi = qf[lo:lo + HG_SUB, :] * jnp.exp(b[lo:lo + HG_SUB, :] - ref_b)
            ki = kf * jnp.exp(jnp.where(row_c < lo, ref_b - b, NEG_BIG))
            s_rows.append(lax.dot_general(qi.astype(BF16), ki.astype(BF16), nt, preferred_element_type=F32))
        scores = jnp.concatenate(s_rows, axis=0)
        o = jnp.dot(scores.astype(BF16), vb, preferred_element_type=F32)

        o_diag = []
        for j in range(nsub):
            lo = j * HG_SUB
            bs = b[lo:lo + HG_SUB, :]
            ks = kf[lo:lo + HG_SUB, :]
            qs = qf[lo:lo + HG_SUB, :]
            vs = v[lo:lo + HG_SUB, :]
            tiles = []
            for tt in range(HG_SUB):
                e = jnp.where(row_s <= tt, bs[tt:tt + 1, :] - bs, NEG_BIG)
                tiles.append(jnp.exp(e) * ks * qs[tt:tt + 1, :])
            w = jnp.concatenate(tiles, axis=0).astype(BF16)
            dfull = jnp.dot(w, ones_kk, preferred_element_type=F32)
            contrib = dfull.reshape(HG_SUB, HG_SUB, HG_HEAD_DIM) * vs[None, :, :]
            o_diag.append(jnp.sum(contrib, axis=1))
        o = o + jnp.concatenate(o_diag, axis=0)

        st = st_ref[hh]
        qb = (qf * jnp.exp(b)).astype(BF16)
        o = o + lax.dot_general(qb, st.astype(BF16), nt, preferred_element_type=F32)
        b_last = b[c - 1:c, :]
        kn = (kf * jnp.exp(b_last - b)).astype(BF16)
        st_ref[hh] = st * jnp.exp(b_last) + lax.dot_general(vb, kn, tn, preferred_element_type=F32)

        y = o * lax.rsqrt(jnp.mean(o * o, axis=-1, keepdims=True) + RMS_EPS) * gain
        o_ref[pl.ds(r0, c), hcol] = (y * jax.nn.silu(g)).astype(o_ref.dtype)

    def chunk(ci, carry):
        r0 = pl.multiple_of(ci * c, c)
        for hh in range(st_ref.shape[0]):
            one_head(r0, hh)
        return carry

    lax.fori_loop(0, in_ref.shape[0] // c, chunk, 0)


def _hgrn2(h, lb, gain, bsz, seq):
    t = bsz * seq
    rows = min(HG_ROWS, seq)
    nr = seq // rows
    hd_block0 = COL_HG // (4 * HG_HEAD_DIM)
    grp = HG_GROUP
    return pl.pallas_call(
        _hgrn_kernel,
        grid=(bsz, HG_HEADS // grp, nr),
        in_specs=[pl.BlockSpec((rows, grp * 4 * HG_HEAD_DIM), lambda b, hd, r: (b * nr + r, hd_block0 // grp + hd)),
                  pl.BlockSpec((1, grp * HG_HEAD_DIM), lambda b, hd, r: (0, hd)),
                  pl.BlockSpec((1, grp * HG_HEAD_DIM), lambda b, hd, r: (0, hd))],
        out_specs=pl.BlockSpec((rows, grp * HG_HEAD_DIM), lambda b, hd, r: (b * nr + r, hd)),
        out_shape=jax.ShapeDtypeStruct((t, HG_WIDTH), BF16),
        scratch_shapes=[pltpu.VMEM((grp, HG_HEAD_DIM, HG_HEAD_DIM), F32)],
        compiler_params=_cparams(("arbitrary", "arbitrary", "arbitrary")),
        name="hgrn2",
    )(h, lb, gain)


def _lru_kernel(in_ref, cw_ref, cb_ref, wa_ref, ba_ref, wx_ref, bx_ref, lam_ref, gain_ref, o_ref,
                tail_ref, h_ref):
    r = pl.program_id(1)

    @pl.when(r == 0)
    def _():
        tail_ref[...] = jnp.zeros_like(tail_ref)
        h_ref[...] = jnp.zeros_like(h_ref)

    rows = in_ref.shape[0]
    x = in_ref[:, 0:LRU_WIDTH]
    gr = in_ref[:, LRU_WIDTH:2 * LRU_WIDTH]
    tail = tail_ref[...]
    row8 = lax.broadcasted_iota(jnp.int32, (8, LRU_WIDTH), 0)

    xc = x * cw_ref[CONV_WIDTH - 1:CONV_WIDTH, :] + cb_ref[...]
    for k in range(1, CONV_WIDTH):
        xs = pltpu.roll(x, k, axis=0)
        head = jnp.where(row8 < k, pltpu.roll(tail, k, axis=0), xs[0:8, :])
        xs = jnp.concatenate([head, xs[8:, :]], axis=0)
        xc = xc + xs * cw_ref[CONV_WIDTH - 1 - k:CONV_WIDTH - k, :]
    tail_ref[...] = x[rows - 8:rows, :]

    xcb = xc.astype(BF16)
    rg = jax.nn.sigmoid(jnp.dot(xcb, wa_ref[...], preferred_element_type=F32) + ba_ref[...])
    ig = jax.nn.sigmoid(jnp.dot(xcb, wx_ref[...], preferred_element_type=F32) + bx_ref[...])
    log_a = -LRU_C * rg * jax.nn.softplus(-lam_ref[...])
    a = jnp.exp(log_a)
    th = jnp.tanh(log_a)
    neg_expm1 = -2.0 * th / (1.0 - th)
    u = jnp.sqrt(jnp.maximum(neg_expm1, 0.0)) * (ig * xc)

    row = lax.broadcasted_iota(jnp.int32, (rows, LRU_WIDTH), 0)
    d = 1
    while d < rows:
        keep = row >= d
        a_s = jnp.where(keep, pltpu.roll(a, d, axis=0), 1.0)
        u_s = jnp.where(keep, pltpu.roll(u, d, axis=0), 0.0)
        u = a * u_s + u
        a = a * a_s
        d *= 2
    hcur = u + a * h_ref[0:1, :]
    h_ref[...] = jnp.broadcast_to(hcur[rows - 1:rows, :], h_ref.shape)

    y = hcur * lax.rsqrt(jnp.mean(hcur * hcur, axis=-1, keepdims=True) + RMS_EPS) * gain_ref[...]
    o_ref[...] = (y * jax.nn.gelu(gr)).astype(o_ref.dtype)


def _rglru(h, cw, cb, wa, ba, wx, bx, lam, gain, bsz, seq):
    t = bsz * seq
    rows = min(LRU_ROWS, seq)
    nr = seq // rows
    vec = pl.BlockSpec((1, LRU_WIDTH), lambda b, r: (0, 0))
    mat = pl.BlockSpec((LRU_WIDTH, LRU_WIDTH), lambda b, r: (0, 0))
    return pl.pallas_call(
        _lru_kernel,
        grid=(bsz, nr),
        in_specs=[pl.BlockSpec((rows, 2 * LRU_WIDTH), lambda b, r: (b * nr + r, COL_LRU // (2 * LRU_WIDTH))),
                  pl.BlockSpec((CONV_WIDTH, LRU_WIDTH), lambda b, r: (0, 0)),
                  vec, mat, vec, mat, vec, vec, vec],
        out_specs=pl.BlockSpec((rows, LRU_WIDTH), lambda b, r: (b * nr + r, 0)),
        out_shape=jax.ShapeDtypeStruct((t, LRU_WIDTH), BF16),
        scratch_shapes=[pltpu.VMEM((8, LRU_WIDTH), F32), pltpu.VMEM((8, LRU_WIDTH), F32)],
        compiler_params=_cparams(("arbitrary", "arbitrary")),
        name="rglru",
    )(h, cw, cb, wa, ba, wx, bx, lam, gain)


def _layer_norm(z, g, b):
    mu = jnp.mean(z, axis=-1, keepdims=True)
    zc = z - mu
    var = jnp.mean(zc * zc, axis=-1, keepdims=True)
    return zc * lax.rsqrt(var + LN_EPS) * g + b


def _outproj_kernel(ya_ref, yh_ref, yl_ref, x_ref, wa_ref, wh_ref, wl_ref, g_ref, b_ref, rw_ref, rb_ref,
                    x1_ref, idx_ref, rank_ref, gate_ref, counts_ref, carry_ref):
    i = pl.program_id(0)

    @pl.when(i == 0)
    def _():
        carry_ref[...] = jnp.zeros_like(carry_ref)

    mixed = jnp.dot(ya_ref[...], wa_ref[...], preferred_element_type=F32)
    mixed = mixed + jnp.dot(yh_ref[...], wh_ref[...], preferred_element_type=F32)
    mixed = mixed + jnp.dot(yl_ref[...], wl_ref[...], preferred_element_type=F32)
    x1 = _layer_norm(DEEPNORM_ALPHA * x_ref[...] + mixed, g_ref[...], b_ref[...])
    x1_ref[...] = x1

    tm = x1.shape[0]
    logits = lax.dot_general(rw_ref[...], x1.astype(BF16), (((1,), (1,)), ((), ())), preferred_element_type=F32)
    scores = jax.nn.sigmoid(logits)
    sel = scores + rb_ref[...]
    shp = (N_GROUPS, EXPERTS_PER_GROUP, tm)
    s3 = sel.reshape(shp)
    sc3 = scores.reshape(shp)
    e_in = lax.broadcasted_iota(jnp.int32, shp, 1)
    g_id = lax.broadcasted_iota(jnp.int32, shp, 0)
    e_id = g_id * EXPERTS_PER_GROUP + e_in

    m1 = jnp.max(s3, axis=1, keepdims=True)
    i1 = jnp.min(jnp.where(s3 == m1, e_in, EXPERTS_PER_GROUP), axis=1, keepdims=True)
    m2 = jnp.max(jnp.where(e_in == i1, NEG_PICKED, s3), axis=1, keepdims=True)
    gs = m1 + m2
    g1 = lax.broadcasted_iota(jnp.int32, gs.shape, 0)
    gsel = jnp.zeros(gs.shape, jnp.int32)
    cur = gs
    for _ in range(TOPK_GROUPS):
        m = jnp.max(cur, axis=0, keepdims=True)
        pick = g1 == jnp.min(jnp.where(cur == m, g1, N_GROUPS), axis=0, keepdims=True)
        gsel = jnp.where(pick, 1, gsel)
        cur = jnp.where(pick, NEG_PICKED, cur)
    cur = jnp.where(gsel > 0, s3, NEG_BIG)

    def pick_sum(pick, vals):
        return jnp.sum(jnp.sum(jnp.where(pick, vals, 0.0), axis=1, keepdims=True), axis=0, keepdims=True).reshape(1, tm)

    idx_rows, w_rows, picks = [], [], []
    onehot = jnp.zeros(shp, F32)
    for _ in range(TOPK):
        m = jnp.max(jnp.max(cur, axis=1, keepdims=True), axis=0, keepdims=True)
        cand = jnp.where(cur == m, e_id, N_EXPERTS)
        ii = jnp.min(jnp.min(cand, axis=1, keepdims=True), axis=0, keepdims=True)
        pick = e_id == ii
        picks.append(pick)
        onehot = jnp.where(pick, 1.0, onehot)
        w_rows.append(pick_sum(pick, sc3))
        idx_rows.append(ii.reshape(1, tm))
        cur = jnp.where(pick, NEG_PICKED, cur)
    w = jnp.concatenate(w_rows, axis=0)
    idx = jnp.concatenate(idx_rows, axis=0)
    gates = w / jnp.sum(w, axis=0, keepdims=True) * ROUTED_SCALE

    oh = onehot.reshape(N_EXPERTS, tm).astype(BF16)
    r_i = lax.broadcasted_iota(jnp.int32, (tm, tm), 0)
    c_i = lax.broadcasted_iota(jnp.int32, (tm, tm), 1)
    before = jnp.where(r_i < c_i, 1.0, 0.0).astype(BF16)
    carry = carry_ref[...]
    prefix3 = (jnp.dot(oh, before, preferred_element_type=F32) + carry).reshape(shp)
    rank = jnp.concatenate([pick_sum(pk, prefix3) for pk in picks], axis=0).astype(jnp.int32)
    carry = carry + jnp.dot(oh, jnp.ones((tm, tm), BF16), preferred_element_type=F32)
    carry_ref[...] = carry
    counts_ref[...] = carry.astype(jnp.int32)

    for c in range(tm // IDX_TILE):
        sl = slice(c * IDX_TILE, (c + 1) * IDX_TILE)
        idx_ref[c] = idx[:, sl]
        rank_ref[c] = rank[:, sl]
        gate_ref[c] = gates[:, sl]


def _outproj_router(ya, yh, yl, x2d, w_out, ln_g, ln_b, rw_t, rb):
    t = x2d.shape[0]
    tm = min(OUT_TM, t)
    nt = tm // IDX_TILE
    rowb = lambda w: pl.BlockSpec((tm, w), lambda i: (i, 0))
    full = lambda a, b: pl.BlockSpec((a, b), lambda i: (0, 0))
    tiles = pl.BlockSpec((nt, TOPK, IDX_TILE), lambda i: (i, 0, 0))
    wa = pl.BlockSpec((ATTN_WIDTH, D_MODEL), lambda i: (0, 0))
    wh = pl.BlockSpec((HG_WIDTH, D_MODEL), lambda i: (ATTN_WIDTH // HG_WIDTH, 0))
    wl = pl.BlockSpec((LRU_WIDTH, D_MODEL), lambda i: ((ATTN_WIDTH + HG_WIDTH) // LRU_WIDTH, 0))
    tile_shape = lambda dt: jax.ShapeDtypeStruct((t // IDX_TILE, TOPK, IDX_TILE), dt)
    return pl.pallas_call(
        _outproj_kernel,
        grid=(t // tm,),
        in_specs=[rowb(ATTN_WIDTH), rowb(HG_WIDTH), rowb(LRU_WIDTH), rowb(D_MODEL), wa, wh, wl,
                  full(1, D_MODEL), full(1, D_MODEL), full(N_EXPERTS, D_MODEL), full(N_EXPERTS, 1)],
        out_specs=[rowb(D_MODEL), tiles, tiles, tiles, full(N_EXPERTS, tm)],
        out_shape=[jax.ShapeDtypeStruct((t, D_MODEL), F32),
                   tile_shape(jnp.int32), tile_shape(jnp.int32), tile_shape(F32),
                   jax.ShapeDtypeStruct((N_EXPERTS, tm), jnp.int32)],
        scratch_shapes=[pltpu.VMEM((N_EXPERTS, tm), F32)],
        compiler_params=_cparams(("arbitrary",)),
        name="outproj_router",
    )(ya, yh, yl, x2d, w_out, w_out, w_out, ln_g, ln_b, rw_t, rb)


def _dispatch_kernel(pend_ref, padded_ref, dest_hbm, x_ref, p_ref, sw1_ref, sw3_ref, sw2_ref, pw_ref,
                     gw_ref, gb_ref, xs_hbm, sp_ref, dest_smem, zeros_ref, rows_ref, sem_idx, sem_rows, sem_zero):
    i = pl.program_id(0)
    tm = x_ref.shape[0]
    bm = zeros_ref.shape[0] // PACK_TILES
    nt = tm // IDX_TILE

    @pl.when(i == 0)
    def _():
        zeros_ref[...] = jnp.zeros_like(zeros_ref)

        def fill(e, carry):
            @pl.when(padded_ref[e] > 0)
            def _():
                start = pl.multiple_of(pend_ref[e] - bm, bm)
                pltpu.make_async_copy(zeros_ref, xs_hbm.at[pl.ds(start * PACK_TILES, bm * PACK_TILES)], sem_zero).start()
            return carry

        def fill_wait(e, carry):
            @pl.when(padded_ref[e] > 0)
            def _():
                pltpu.make_async_copy(zeros_ref, xs_hbm.at[pl.ds(0, bm * PACK_TILES)], sem_zero).wait()
            return carry

        def fill_tail(blk, carry):
            start = pl.multiple_of(blk * bm, bm)
            pltpu.make_async_copy(zeros_ref, xs_hbm.at[pl.ds(start * PACK_TILES, bm * PACK_TILES)], sem_zero).start()
            return carry

        def fill_tail_wait(blk, carry):
            pltpu.make_async_copy(zeros_ref, xs_hbm.at[pl.ds(0, bm * PACK_TILES)], sem_zero).wait()
            return carry

        n_used = pend_ref[N_EXPERTS - 1] // bm
        n_blocks = xs_hbm.shape[0] // (bm * PACK_TILES)
        lax.fori_loop(0, N_EXPERTS, fill, 0)
        lax.fori_loop(n_used, n_blocks, fill_tail, 0)
        lax.fori_loop(0, N_EXPERTS, fill_wait, 0)
        lax.fori_loop(n_used, n_blocks, fill_tail_wait, 0)

    n = pl.num_programs(0)

    def idx_copy(step, s):
        return pltpu.make_async_copy(dest_hbm.at[pl.ds(step * nt, nt)], dest_smem.at[s], sem_idx.at[s])

    def wait_rows(s):
        n_copied = TOPK * tm * PACK_TILES
        pltpu.make_async_copy(xs_hbm.at[pl.ds(0, n_copied)], xs_hbm.at[pl.ds(0, n_copied)], sem_rows.at[s]).wait()

    @pl.when(i == 0)
    def _():
        idx_copy(0, 0).start()

    def step(slot):
        @pl.when(i + 1 < n)
        def _():
            idx_copy(i + 1, 1 - slot).start()

        x = x_ref[...]
        _to_row_tiles(rows_ref.at[slot], _pack_bf16_pairs(x))
        idx_copy(i, slot).wait()

        for c in range(nt):
            for r in range(IDX_TILE):
                row = c * IDX_TILE + r
                for j in range(TOPK):
                    d = dest_smem[slot, c, j, r]
                    pltpu.make_async_copy(rows_ref.at[slot, pl.ds(row * PACK_TILES, PACK_TILES)],
                                          xs_hbm.at[pl.ds(d * PACK_TILES, PACK_TILES)],
                                          sem_rows.at[slot]).start(priority=j % 2)

        xb = x.astype(BF16)
        h1 = jnp.dot(xb, sw1_ref[...], preferred_element_type=F32)
        h3 = jnp.dot(xb, sw3_ref[...], preferred_element_type=F32)
        shared = jnp.dot((jax.nn.silu(h1) * h3).astype(BF16), sw2_ref[...], preferred_element_type=F32)
        gate = jax.nn.sigmoid(jnp.dot(xb, gw_ref[...], preferred_element_type=F32) + gb_ref[...])
        ple = gate * jnp.dot(p_ref[...].astype(BF16), pw_ref[...], preferred_element_type=F32)
        sp_ref[...] = shared + ple

        @pl.when(i > 0)
        def _():
            wait_rows(1 - slot)

        @pl.when(i == n - 1)
        def _():
            wait_rows(slot)

    for k in range(2):
        pl.when(i % 2 == k)(lambda k=k: step(k))


def _dispatch(layer, pend, padded, dest_tiles, x1, p3, sw1, sw3, sw2, pw, gw, gb, n_rows, bm):
    t = x1.shape[0]
    tm = min(DISP_TM, t)
    full = lambda a, b: pl.BlockSpec((a, b), lambda i, pe, pa: (0, 0))
    grid_spec = pltpu.PrefetchScalarGridSpec(
        num_scalar_prefetch=2,
        grid=(t // tm,),
        in_specs=[pl.BlockSpec(memory_space=pl.ANY),
                  pl.BlockSpec((tm, D_MODEL), lambda i, pe, pa: (i, 0)),
                  pl.BlockSpec((None, tm, PLE_DIM), lambda i, pe, pa: (layer, i, 0)),
                  full(D_MODEL, EXPERT_DIM), full(D_MODEL, EXPERT_DIM), full(EXPERT_DIM, D_MODEL),
                  full(PLE_DIM, D_MODEL), full(D_MODEL, D_MODEL), full(1, D_MODEL)],
        out_specs=[pl.BlockSpec(memory_space=pl.ANY),
                   pl.BlockSpec((tm, D_MODEL), lambda i, pe, pa: (i, 0))],
        scratch_shapes=[pltpu.SMEM((2, tm // IDX_TILE, TOPK, IDX_TILE), jnp.int32),
                        pltpu.VMEM((bm * PACK_TILES, LANES), jnp.uint32),
                        pltpu.VMEM((2, tm * PACK_TILES, LANES), jnp.uint32),
                        pltpu.SemaphoreType.DMA((2,)), pltpu.SemaphoreType.DMA((2,)), pltpu.SemaphoreType.DMA],
    )
    return pl.pallas_call(
        _dispatch_kernel,
        grid_spec=grid_spec,
        out_shape=[jax.ShapeDtypeStruct((n_rows * PACK_TILES, LANES), jnp.uint32),
                   jax.ShapeDtypeStruct((t, D_MODEL), F32)],
        compiler_params=_cparams(("arbitrary",)),
        name="dispatch_shared_ple",
    )(pend, padded, dest_tiles, x1, p3, sw1, sw3, sw2, pw, gw, gb)


def _moe_kernel(be_ref, nused_ref, x_ref, w1_ref, w3_ref, w2_ref, y_ref, w1b, w3b, w2b):
    n = pl.program_id(0)

    @pl.when(n < nused_ref[0])
    def _():
        e = be_ref[n]
        e_prev = be_ref[jnp.maximum(n - 1, 0)]

        @pl.when((n == 0) | (e != e_prev))
        def _():
            w1b[...] = w1_ref[...].astype(BF16)
            w3b[...] = w3_ref[...].astype(BF16)
            w2b[...] = w2_ref[...].astype(BF16)

        sub = min(MOE_SUB, x_ref.shape[0] // PACK_TILES)
        for c in range(x_ref.shape[0] // (PACK_TILES * sub)):
            xb = _unpack_bf16_pairs(_from_row_tiles(x_ref, PACK_TILES, base=c * sub * PACK_TILES, rows=sub))
            h1 = jnp.dot(xb, w1b[...], preferred_element_type=F32)
            h3 = jnp.dot(xb, w3b[...], preferred_element_type=F32)
            act = (jax.nn.silu(h1) * h3).astype(BF16)
            y = jnp.dot(act, w2b[...], preferred_element_type=F32)
            _to_row_tiles(y_ref, _pack_bf16_pairs(y), base=c * sub * PACK_TILES)

    @pl.when(n >= nused_ref[0])
    def _():
        y_ref[...] = jnp.zeros_like(y_ref)


def _moe(layer, block_expert, n_used, xs, w1, w3, w2, bm):
    n_blocks = xs.shape[0] // (bm * PACK_TILES)
    wspec = lambda a, b: pl.BlockSpec((None, None, a, b), lambda n, be, nu: (layer, be[n], 0, 0))
    grid_spec = pltpu.PrefetchScalarGridSpec(
        num_scalar_prefetch=2,
        grid=(n_blocks,),
        in_specs=[pl.BlockSpec((bm * PACK_TILES, LANES), lambda n, be, nu: (jnp.minimum(n, nu[0] - 1), 0)),
                  wspec(D_MODEL, EXPERT_DIM), wspec(D_MODEL, EXPERT_DIM), wspec(EXPERT_DIM, D_MODEL)],
        out_specs=pl.BlockSpec((bm * PACK_TILES, LANES), lambda n, be, nu: (n, 0)),
        scratch_shapes=[pltpu.VMEM((D_MODEL, EXPERT_DIM), BF16),
                        pltpu.VMEM((D_MODEL, EXPERT_DIM), BF16),
                        pltpu.VMEM((EXPERT_DIM, D_MODEL), BF16)],
    )
    return pl.pallas_call(
        _moe_kernel,
        grid_spec=grid_spec,
        out_shape=jax.ShapeDtypeStruct((n_blocks * bm * PACK_TILES, LANES), jnp.uint32),
        compiler_params=_cparams(("arbitrary",)),
        name="moe_experts",
    )(block_expert, n_used, xs, w1, w3, w2)


def _final_kernel(dest_hbm, gate_hbm, y_hbm, x_ref, sp_ref, g_ref, b_ref, o_ref,
                  dest_smem, gate_smem, ybuf, routed, sem_idx, sem_rows):
    i = pl.program_id(0)
    n = pl.num_programs(0)
    tm = x_ref.shape[0]
    nbuf = ybuf.shape[0]
    ahead = nbuf - 1

    def idx_copies(tile, s):
        return (pltpu.make_async_copy(dest_hbm.at[tile], dest_smem.at[s], sem_idx.at[s]),
                pltpu.make_async_copy(gate_hbm.at[tile], gate_smem.at[s], sem_idx.at[s]))

    def start_idx(tile, s):
        for cp in idx_copies(tile, s):
            cp.start()

    def wait_idx(tile, s):
        for cp in idx_copies(tile, s):
            cp.wait()

    def issue_row(s, r):
        for j in range(TOPK):
            d = dest_smem[s, j, r]
            pltpu.make_async_copy(y_hbm.at[pl.ds(d * PACK_TILES, PACK_TILES)],
                                  ybuf.at[s, j, pl.ds(r * PACK_TILES, PACK_TILES)], sem_rows.at[s]).start(priority=j % 2)

    def issue_rows_loop(s):
        def issue(r, carry):
            issue_row(s, r)
            return carry

        lax.fori_loop(0, tm, issue, 0)

    @pl.when(i == 0)
    def _():
        for k in range(ahead):
            @pl.when(k < n)
            def _(k=k):
                start_idx(k, k)
                wait_idx(k, k)
                issue_rows_loop(k)

        @pl.when(ahead < n)
        def _():
            start_idx(ahead, ahead)

    def finish():
        routed2d = jnp.concatenate([_from_row_tiles(routed.at[0], PACK_TILES),
                                    _from_row_tiles(routed.at[1], PACK_TILES)], axis=1)
        z = DEEPNORM_ALPHA * x_ref[...] + routed2d + sp_ref[...]
        o_ref[...] = _layer_norm(z, g_ref[...], b_ref[...])

    def step(slot):
        for j in range(TOPK):
            pltpu.make_async_copy(y_hbm.at[pl.ds(0, tm * PACK_TILES)], ybuf.at[slot, j], sem_rows.at[slot]).wait()

        def combine(r, carry):
            rows = pl.ds(pl.multiple_of(r * PACK_TILES, PACK_TILES), PACK_TILES)
            acc_lo = acc_hi = None
            for j in range(TOPK):
                w = ybuf[slot, j, rows, :]
                g = gate_smem[slot, j, r]
                lo = g * lax.bitcast_convert_type(w << 16, F32)
                hi = g * lax.bitcast_convert_type(w & jnp.uint32(0xFFFF0000), F32)
                acc_lo = lo if acc_lo is None else acc_lo + lo
                acc_hi = hi if acc_hi is None else acc_hi + hi
            routed[0, rows, :] = acc_lo
            routed[1, rows, :] = acc_hi
            return carry

        lax.fori_loop(0, tm, combine, 0, unroll=8)

        @pl.when(i + nbuf < n)
        def _():
            start_idx(i + nbuf, slot)

        @pl.when(i + ahead < n)
        def _():
            s_new = (slot + ahead) % nbuf
            wait_idx(i + ahead, s_new)
            for r in range(tm):
                issue_row(s_new, r)
            finish()

        @pl.when(i + ahead >= n)
        def _():
            finish()

    for k in range(nbuf):
        pl.when(i % nbuf == k)(lambda k=k: step(k))


def _final(dest_tiles, gate_tiles, y_sorted, x1, sp, ln_g, ln_b):
    t = x1.shape[0]
    tm = min(FIN_TM, t)
    full = lambda a, b: pl.BlockSpec((a, b), lambda i: (0, 0))
    rowb = pl.BlockSpec((tm, D_MODEL), lambda i: (i, 0))
    return pl.pallas_call(
        _final_kernel,
        grid=(t // tm,),
        in_specs=[pl.BlockSpec(memory_space=pl.ANY), pl.BlockSpec(memory_space=pl.ANY),
                  pl.BlockSpec(memory_space=pl.ANY), rowb, rowb, full(1, D_MODEL), full(1, D_MODEL)],
        out_specs=rowb,
        out_shape=jax.ShapeDtypeStruct((t, D_MODEL), F32),
        scratch_shapes=[pltpu.SMEM((FIN_BUFS, TOPK, tm), jnp.int32),
                        pltpu.SMEM((FIN_BUFS, TOPK, tm), F32),
                        pltpu.VMEM((FIN_BUFS, TOPK, tm * PACK_TILES, LANES), jnp.uint32),
                        pltpu.VMEM((2, tm * PACK_TILES, LANES), F32),
                        pltpu.SemaphoreType.DMA((FIN_BUFS,)),
                        pltpu.SemaphoreType.DMA((FIN_BUFS,))],
        compiler_params=_cparams(("arbitrary",)),
        name="combine_ln",
    )(dest_tiles, gate_tiles, y_sorted, x1, sp, ln_g, ln_b)


def _dispatch_plan(idx_tiles, rank_tiles, counts, bm, n_blocks):
    padded = ((counts + bm - 1) // bm) * bm
    pend = jnp.cumsum(padded).astype(jnp.int32)
    pstart = pend - padded
    experts = jnp.arange(N_EXPERTS, dtype=jnp.int32)
    start_of = jnp.sum(jnp.where(idx_tiles[..., None] == experts, pstart, 0), axis=-1)
    dest_tiles = (start_of + rank_tiles).astype(jnp.int32)
    block_row = jnp.arange(n_blocks, dtype=jnp.int32) * bm
    block_expert = jnp.minimum(jnp.sum((pend[None, :] <= block_row[:, None]).astype(jnp.int32), axis=1),
                               N_EXPERTS - 1)
    n_used = (pend[-1] // bm).reshape(1)
    return pend, padded.astype(jnp.int32), dest_tiles, block_expert, n_used


def _rotary_lane_tables(positions):
    inv_freq = ROPE_THETA ** (-jnp.arange(0, ROT_DIM, 2, dtype=F32) / ROT_DIM)
    half = ROT_DIM // 2
    dim = jnp.arange(LANES, dtype=jnp.int32) % HEAD_DIM
    lane_freq = jnp.where(dim < ROT_DIM, inv_freq[dim % half], 0.0)
    ang = positions.astype(F32).reshape(-1)[:, None] * lane_freq
    sin = jnp.sin(ang)
    sa = jnp.where((dim >= half) & (dim < ROT_DIM), sin, 0.0)
    sb = jnp.where(dim < half, -sin, 0.0)
    return jnp.cos(ang), sa, sb


def _permute_in_cols(w):
    off = [0]
    for s in IN_SIZES:
        off.append(off[-1] + s)
    aq, ak, av, hq, hf, hi, hg, lx, lg = [w[:, off[k]:off[k + 1]] for k in range(9)]
    parts = []
    for hd in range(HG_HEADS):
        sl = slice(hd * HG_HEAD_DIM, (hd + 1) * HG_HEAD_DIM)
        parts += [hq[:, sl], hf[:, sl], hi[:, sl], hg[:, sl]]
    parts += [aq, lx, lg, ak, av]
    return jnp.concatenate(parts, axis=1)


def _block_diag(w):
    hds, d, _ = w.shape
    eye = jnp.eye(hds, dtype=w.dtype)
    return (eye[:, None, :, None] * w[:, :, None, :]).reshape(hds * d, hds * d)


def kernel(x, p, positions, w_in, w_out, attn_sinks, attn_norm, hg_lb_logits, hg_norm, lru_conv_w, lru_conv_b, lru_wa, lru_ba, lru_wx, lru_bx, lru_lambda, lru_norm, ln1_g, ln1_b, router_w, router_b, exp_w1, exp_w3, exp_w2, sh_w1, sh_w3, sh_w2, ple_w, ple_gate_w, ple_gate_b, ln2_g, ln2_b):
    bsz, seq, _ = x.shape
    t = bsz * seq
    depth = w_in.shape[0]
    lb_sm = jax.nn.softmax(hg_lb_logits.astype(F32), axis=0)
    hg_lb = jnp.maximum(jnp.cumsum(lb_sm, axis=0) - lb_sm[0], 0.0)
    cos_t, sa_t, sb_t = _rotary_lane_tables(positions)
    p3 = p.reshape(depth, t, PLE_DIM)
    bm = min(MOE_BM, t)
    n_blocks = t * TOPK // bm + N_EXPERTS
    row = lambda v: v.reshape(1, -1)

    xc = x.reshape(t, D_MODEL)
    for i in range(depth):
        h = _inproj(xc, _permute_in_cols(w_in[i].astype(BF16)))
        ya = _attention(h, attn_sinks[i], cos_t, sa_t, sb_t, row(attn_norm[i]), bsz, seq)
        yh = _hgrn2(h, row(hg_lb[i]), row(hg_norm[i]), bsz, seq)
        yl = _rglru(h, lru_conv_w[i], row(lru_conv_b[i]), _block_diag(lru_wa[i]).astype(BF16), row(lru_ba[i]),
                    _block_diag(lru_wx[i]).astype(BF16), row(lru_bx[i]), row(lru_lambda[i]), row(lru_norm[i]),
                    bsz, seq)
        x1, idx_tiles, rank_tiles, gate_tiles, counts = _outproj_router(
            ya, yh, yl, xc, w_out[i].astype(BF16), row(ln1_g[i]), row(ln1_b[i]),
            router_w[i].T.astype(BF16), router_b[i].reshape(N_EXPERTS, 1))
        pend, padded, dest_tiles, block_expert, n_used = _dispatch_plan(idx_tiles, rank_tiles, counts[:, 0], bm, n_blocks)
        xs, sp = _dispatch(i, pend, padded, dest_tiles, x1, p3, sh_w1[i].astype(BF16), sh_w3[i].astype(BF16),
                           sh_w2[i].astype(BF16), ple_w[i].astype(BF16), ple_gate_w[i].astype(BF16),
                           row(ple_gate_b[i]), n_blocks * bm, bm)
        y_sorted = _moe(i, block_expert, n_used, xs, exp_w1, exp_w3, exp_w2, bm)
        xc = _final(dest_tiles, gate_tiles, y_sorted, x1, sp, row(ln2_g[i]), row(ln2_b[i]))
    return xc.reshape(bsz, seq, D_MODEL)
```
